```python
import jax
import jax.numpy as jnp
from jax import lax

D_MODEL = 2048
BATCH = 2
SEQ = 8192
DEPTH = 1

MIX_WIDTH = D_MODEL
HEAD_DIM = 128
NSA_HEADS = 8
NSA_KV_HEADS = 2
NSA_WIDTH = NSA_HEADS * HEAD_DIM
KV_WIDTH = NSA_KV_HEADS * HEAD_DIM
N_BRANCH = 3
CMP_BLK = 32
CMP_STRIDE = 16
CMP_HIDDEN = 256
SLC_BLK = 64
N_SEL = 16
WINDOW = 512
NSA_Q_BLK = 64
ROPE_THETA = 500000.0
ROT_DIM = HEAD_DIM // 4
POOL_WIDTH = MIX_WIDTH - NSA_WIDTH
POOL_WINDOWS = (2, 4, 8, 16)
N_POOL_GROUPS = 4
POOL_GC = POOL_WIDTH // N_POOL_GROUPS
IN_COLS = NSA_WIDTH + 6 * KV_WIDTH + NSA_HEADS * N_BRANCH + POOL_WIDTH
N_EXPERTS = 32
TOP_K = 4
D_FF = D_MODEL
SWIGLU_ALPHA = 1.702
SWIGLU_LIMIT = 7.0
MOE_BLK = 128
RMS_EPS = 1e-5
QK_EPS = 1e-6
MASK_VALUE = -1e30
FORCE_VALUE = 1e30

kernel_name = 'hybrid_nsa_pool_moe_layer'


def rmsnorm(x, g, eps=RMS_EPS):
    xf = x.astype(jnp.float32)
    y = xf * lax.rsqrt(jnp.mean(xf * xf, axis=-1, keepdims=True) + eps)
    return (y * g.astype(jnp.float32)).astype(x.dtype)


def partial_rope(x, pos):
    half = ROT_DIM // 2
    inv_freq = ROPE_THETA ** (-jnp.arange(0, ROT_DIM, 2, dtype=jnp.float32) / ROT_DIM)
    ang = pos[:, None] * inv_freq[None, :]
    cos = jnp.cos(ang)[None, :, None, :]
    sin = jnp.sin(ang)[None, :, None, :]
    xf = x.astype(jnp.float32)
    x1, x2, rest = xf[..., :half], xf[..., half:ROT_DIM], xf[..., ROT_DIM:]
    y = jnp.concatenate([x1 * cos - x2 * sin, x2 * cos + x1 * sin, rest], axis=-1)
    return y.astype(x.dtype)


def masked_softmax(s, mask):
    p = jax.nn.softmax(jnp.where(mask, s, MASK_VALUE), axis=-1)
    return jnp.where(mask, p, 0.0)


def compress_blocks(kv, pos_emb, w1, b1, w2, b2):
    B, S, G, D = kv.shape
    chunks = kv.reshape(B, S // CMP_STRIDE, CMP_STRIDE, G, D)
    blocks = jnp.concatenate([chunks[:, :-1], chunks[:, 1:]], axis=2)
    blocks = blocks + pos_emb[None, None, :, None, :]
    hid = jax.nn.gelu(jnp.einsum('bnlgd,ldh->bngh', blocks, w1) + b1)
    return jnp.einsum('bngh,hd->bngd', hid, w2) + b2


def nsa_mixer(q, k_c, v_c, k_s, v_s, k_w, v_w, g_logit,
              q_norm_g, k_norm_cmp_g, k_norm_slc_g, k_norm_win_g,
              cmp_k_pos, cmp_k_w1, cmp_k_b1, cmp_k_w2, cmp_k_b2,
              cmp_v_pos, cmp_v_w1, cmp_v_b1, cmp_v_w2, cmp_v_b2):
    B, S, _ = q.shape
    H, G, D = NSA_HEADS, NSA_KV_HEADS, HEAD_DIM
    R = H // G
    Q = NSA_Q_BLK
    pos = jnp.arange(S, dtype=jnp.float32)
    q = partial_rope(rmsnorm(q.reshape(B, S, H, D), q_norm_g, QK_EPS), pos)
    k_s = partial_rope(rmsnorm(k_s.reshape(B, S, G, D), k_norm_slc_g, QK_EPS), pos)
    k_w = partial_rope(rmsnorm(k_w.reshape(B, S, G, D), k_norm_win_g, QK_EPS), pos)
    v_s = v_s.reshape(B, S, G, D)
    v_w = v_w.reshape(B, S, G, D)
    n_cmp = S // CMP_STRIDE - 1
    c_end = jnp.arange(n_cmp, dtype=jnp.int32) * CMP_STRIDE + (CMP_BLK - 1)
    k_c = compress_blocks(k_c.reshape(B, S, G, D), cmp_k_pos, cmp_k_w1, cmp_k_b1, cmp_k_w2, cmp_k_b2)
    k_c = partial_rope(rmsnorm(k_c, k_norm_cmp_g, QK_EPS), c_end.astype(jnp.float32))
    v_c = compress_blocks(v_c.reshape(B, S, G, D), cmp_v_pos, cmp_v_w1, cmp_v_b1, cmp_v_w2, cmp_v_b2)
    gates = jax.nn.sigmoid(g_logit.astype(jnp.float32)).astype(q.dtype).reshape(B, S, G, R, N_BRANCH)
    n_slc = S // SLC_BLK
    n_sel = min(N_SEL, n_slc)
    k_blocks = k_s.reshape(B, n_slc, SLC_BLK, G, D).transpose(0, 3, 1, 2, 4)
    v_blocks = v_s.reshape(B, n_slc, SLC_BLK, G, D).transpose(0, 3, 1, 2, 4)
    pad = ((0, 0), (WINDOW, 0), (0, 0), (0, 0))
    k_wp = jnp.pad(k_w, pad)
    v_wp = jnp.pad(v_w, pad)
    scale = HEAD_DIM ** -0.5
    gather_blocks = jax.vmap(jax.vmap(lambda blocks, idx: blocks[idx]))
    blk_ids = jnp.arange(n_slc, dtype=jnp.int32)
    in_blk = jnp.arange(SLC_BLK, dtype=jnp.int32)
    win_off = jnp.arange(Q + WINDOW, dtype=jnp.int32) - WINDOW

    def attend_block(qi):
        qs = qi * Q
        t = qs + jnp.arange(Q, dtype=jnp.int32)
        qb = lax.dynamic_slice_in_dim(q, qs, Q, axis=1).reshape(B, Q, G, R, D)
        s_c = jnp.einsum('bqgrd,bngd->bgrqn', qb, k_c).astype(jnp.float32) * scale
        p_c = masked_softmax(s_c, c_end[None, :] <= t[:, None])
        o_c = jnp.einsum('bgrqn,bngd->bqgrd', p_c.astype(v_c.dtype), v_c)
        imp = p_c.sum(axis=2)
        chunk = (jnp.pad(imp, ((0, 0), (0, 0), (0, 0), (0, 1)))
                 + jnp.pad(imp, ((0, 0), (0, 0), (0, 0), (1, 0))))
        blk_score = chunk.reshape(B, G, Q, n_slc, SLC_BLK // CMP_STRIDE).sum(-1)
        cur = (t // SLC_BLK)[:, None]
        forced = (blk_ids == 0) | (blk_ids == cur) | (blk_ids == cur - 1)
        blk_score = jnp.where(forced, FORCE_VALUE,
                              jnp.where(blk_ids <= cur, blk_score, MASK_VALUE))
        _, sel = lax.top_k(blk_score, n_sel)
        k_sel = gather_blocks(k_blocks, sel).reshape(B, G, Q, n_sel * SLC_BLK, D)
        v_sel = gather_blocks(v_blocks, sel).reshape(B, G, Q, n_sel * SLC_BLK, D)
        kpos = (sel[..., None] * SLC_BLK + in_blk).reshape(B, G, Q, n_sel * SLC_BLK)
        s_s = jnp.einsum('bqgrd,bgqkd->bgrqk', qb, k_sel).astype(jnp.float32) * scale
        p_s = masked_softmax(s_s, (kpos <= t[:, None])[:, :, None])
        o_s = jnp.einsum('bgrqk,bgqkd->bqgrd', p_s.astype(v_sel.dtype), v_sel)
        kw = lax.dynamic_slice_in_dim(k_wp, qs, Q + WINDOW, axis=1)
        vw = lax.dynamic_slice_in_dim(v_wp, qs, Q + WINDOW, axis=1)
        kpos_w = qs + win_off
        diff = t[:, None] - kpos_w[None, :]
        mask_w = (diff >= 0) & (diff < WINDOW) & (kpos_w[None, :] >= 0)
        s_w = jnp.einsum('bqgrd,bkgd->bgrqk', qb, kw).astype(jnp.float32) * scale
        p_w = masked_softmax(s_w, mask_w)
        o_w = jnp.einsum('bgrqk,bkgd->bqgrd', p_w.astype(vw.dtype), vw)
        gb = lax.dynamic_slice_in_dim(gates, qs, Q, axis=1)
        o = gb[..., 0:1] * o_c + gb[..., 1:2] * o_s + gb[..., 2:3] * o_w
        return o.reshape(B, Q, H * D)

    out = lax.map(attend_block, jnp.arange(S // Q, dtype=jnp.int32))
    return out.transpose(1, 0, 2, 3).reshape(B, S, H * D)


def pool_mixer(v, pool_w, pool_scale):
    B, S, _ = v.shape
    vg = v.reshape(B, S, N_POOL_GROUPS, POOL_GC).astype(jnp.float32)
    cs = jnp.cumsum(vg, axis=1)
    t = jnp.arange(S, dtype=jnp.int32)
    outs = []
    for gi, w in enumerate(POOL_WINDOWS):
        c = cs[:, :, gi]
        lag = jnp.pad(c, ((0, 0), (w, 0), (0, 0)))[:, :S]
        cnt = jnp.minimum(t + 1, w).astype(jnp.float32)[None, :, None]
        outs.append((c - lag) / cnt - vg[:, :, gi])
    d = jnp.stack(outs, axis=2).astype(v.dtype)
    y = jnp.einsum('bsgc,gce->bsge', d, pool_w)
    return y.reshape(B, S, POOL_WIDTH) * pool_scale


def moe_ffn(h, w_router, b_router, w_gate_up, b_gate_up, w_down, b_down):
    T, D = h.shape
    logits = jnp.einsum('td,de->te', h, w_router).astype(jnp.float32) + b_router.astype(jnp.float32)
    top_logit, top_idx = lax.top_k(logits, TOP_K)
    top_w = jax.nn.softmax(top_logit, axis=-1)
    n_assign = T * TOP_K
    flat_e = top_idx.reshape(n_assign)
    flat_tok = jnp.arange(n_assign, dtype=jnp.int32) // TOP_K
    flat_w = top_w.reshape(n_assign)
    order = jnp.argsort(flat_e)
    e_sorted = flat_e[order]
    counts = jnp.bincount(flat_e, length=N_EXPERTS)
    start = jnp.cumsum(counts) - counts
    padded = (counts + MOE_BLK - 1) // MOE_BLK * MOE_BLK
    pad_end = jnp.cumsum(padded)
    pad_start = pad_end - padded
    dest = pad_start[e_sorted] + (jnp.arange(n_assign, dtype=jnp.int32) - start[e_sorted])
    n_rows = n_assign + N_EXPERTS * MOE_BLK
    n_blocks = n_rows // MOE_BLK
    tok_buf = jnp.zeros((n_rows,), jnp.int32).at[dest].set(flat_tok[order])
    w_buf = jnp.zeros((n_rows,), jnp.float32).at[dest].set(flat_w[order])
    blk_exp = jnp.minimum(
        jnp.searchsorted(pad_end, jnp.arange(n_blocks, dtype=jnp.int32) * MOE_BLK, side='right'),
        N_EXPERTS - 1)

    def expert_block(args):
        tok, wt, e = args
        xb = h[tok]
        gu = xb @ w_gate_up[e] + b_gate_up[e]
        gate = jnp.minimum(gu[:, :D_FF], SWIGLU_LIMIT)
        up = jnp.clip(gu[:, D_FF:], -SWIGLU_LIMIT, SWIGLU_LIMIT)
        act = (up + 1.0) * gate * jax.nn.sigmoid(SWIGLU_ALPHA * gate)
        y = act @ w_down[e] + b_down[e]
        return y * wt[:, None].astype(y.dtype)

    y_buf = lax.map(expert_block, (tok_buf.reshape(n_blocks, MOE_BLK),
                                   w_buf.reshape(n_blocks, MOE_BLK), blk_exp))
    return jnp.zeros_like(h).at[tok_buf].add(y_buf.reshape(n_rows, D))


def hybrid_layer(x, norm1_g, w_in, q_norm_g, k_norm_cmp_g, k_norm_slc_g, k_norm_win_g,
                 cmp_k_pos, cmp_k_w1, cmp_k_b1, cmp_k_w2, cmp_k_b2,
                 cmp_v_pos, cmp_v_w1, cmp_v_b1, cmp_v_w2, cmp_v_b2,
                 pool_w, pool_scale, w_out, norm2_g,
                 w_router, b_router, w_gate_up, b_gate_up, w_down, b_down):
    B, S, _ = x.shape
    h = rmsnorm(x, norm1_g)
    z = jnp.einsum('bsd,dc->bsc', h, w_in)
    sizes = [NSA_WIDTH] + [KV_WIDTH] * 6 + [NSA_HEADS * N_BRANCH, POOL_WIDTH]
    cuts = [sum(sizes[:i + 1]) for i in range(len(sizes) - 1)]
    q, k_c, v_c, k_s, v_s, k_w, v_w, g_logit, pool_in = jnp.split(z, cuts, axis=-1)
    nsa_out = nsa_mixer(q, k_c, v_c, k_s, v_s, k_w, v_w, g_logit,
                        q_norm_g, k_norm_cmp_g, k_norm_slc_g, k_norm_win_g,
                        cmp_k_pos, cmp_k_w1, cmp_k_b1, cmp_k_w2, cmp_k_b2,
                        cmp_v_pos, cmp_v_w1, cmp_v_b1, cmp_v_w2, cmp_v_b2)
    pool_out = pool_mixer(pool_in, pool_w, pool_scale)
    mix = jnp.concatenate([nsa_out, pool_out], axis=-1)
    x = x + jnp.einsum('bsc,cd->bsd', mix, w_out)
    h2 = rmsnorm(x, norm2_g).reshape(B * S, D_MODEL)
    y = moe_ffn(h2, w_router, b_router, w_gate_up, b_gate_up, w_down, b_down)
    return x + y.reshape(B, S, D_MODEL)


def _normal(k, shape, scale):
    return scale * jax.random.normal(k, shape, jnp.float32)


def setup_inputs(seed: int = 0) -> dict:
    key = jax.random.key(seed)
    ks = iter(jax.random.split(key, 27))
    L = DEPTH
    gain = lambda k, n: 1.0 + _normal(k, (L, n), 0.1)
    return {
        'x': _normal(next(ks), (BATCH, SEQ, D_MODEL), 1.0),
        'norm1_g': gain(next(ks), D_MODEL),
        'w_in': _normal(next(ks), (L, D_MODEL, IN_COLS), D_MODEL ** -0.5),
        'q_norm_g': gain(next(ks), HEAD_DIM),
        'k_norm_cmp_g': gain(next(ks), HEAD_DIM),
        'k_norm_slc_g': gain(next(ks), HEAD_DIM),
        'k_norm_win_g': gain(next(ks), HEAD_DIM),
        'cmp_k_pos': _normal(next(ks), (L, CMP_BLK, HEAD_DIM), 0.1),
        'cmp_k_w1': _normal(next(ks), (L, CMP_BLK, HEAD_DIM, CMP_HIDDEN), (CMP_BLK * HEAD_DIM) ** -0.5),
        'cmp_k_b1': _normal(next(ks), (L, CMP_HIDDEN), 0.02),
        'cmp_k_w2': _normal(next(ks), (L, CMP_HIDDEN, HEAD_DIM), CMP_HIDDEN ** -0.5),
        'cmp_k_b2': _normal(next(ks), (L, HEAD_DIM), 0.02),
        'cmp_v_pos': _normal(next(ks), (L, CMP_BLK, HEAD_DIM), 0.1),
        'cmp_v_w1': _normal(next(ks), (L, CMP_BLK, HEAD_DIM, CMP_HIDDEN), (CMP_BLK * HEAD_DIM) ** -0.5),
        'cmp_v_b1': _normal(next(ks), (L, CMP_HIDDEN), 0.02),
        'cmp_v_w2': _normal(next(ks), (L, CMP_HIDDEN, HEAD_DIM), CMP_HIDDEN ** -0.5),
        'cmp_v_b2': _normal(next(ks), (L, HEAD_DIM), 0.02),
        'pool_w': _normal(next(ks), (L, N_POOL_GROUPS, POOL_GC, POOL_GC), POOL_GC ** -0.5),
        'pool_scale': gain(next(ks), POOL_WIDTH),
        'w_out': _normal(next(ks), (L, MIX_WIDTH, D_MODEL), MIX_WIDTH ** -0.5),
        'norm2_g': gain(next(ks), D_MODEL),
        'w_router': _normal(next(ks), (L, D_MODEL, N_EXPERTS), D_MODEL ** -0.5),
        'b_router': _normal(next(ks), (L, N_EXPERTS), 0.01),
        'w_gate_up': _normal(next(ks), (L, N_EXPERTS, D_MODEL, 2 * D_FF), D_MODEL ** -0.5),
        'b_gate_up': _normal(next(ks), (L, N_EXPERTS, 2 * D_FF), 0.01),
        'w_down': _normal(next(ks), (L, N_EXPERTS, D_FF, D_MODEL), D_FF ** -0.5),
        'b_down': _normal(next(ks), (L, N_EXPERTS, D_MODEL), 0.01),
    }


def reference(x, norm1_g, w_in, q_norm_g, k_norm_cmp_g, k_norm_slc_g, k_norm_win_g,
              cmp_k_pos, cmp_k_w1, cmp_k_b1, cmp_k_w2, cmp_k_b2,
              cmp_v_pos, cmp_v_w1, cmp_v_b1, cmp_v_w2, cmp_v_b2,
              pool_w, pool_scale, w_out, norm2_g,
              w_router, b_router, w_gate_up, b_gate_up, w_down, b_down):
    layer_params = (norm1_g, w_in, q_norm_g, k_norm_cmp_g, k_norm_slc_g, k_norm_win_g,
                    cmp_k_pos, cmp_k_w1, cmp_k_b1, cmp_k_w2, cmp_k_b2,
                    cmp_v_pos, cmp_v_w1, cmp_v_b1, cmp_v_w2, cmp_v_b2,
                    pool_w, pool_scale, w_out, norm2_g,
                    w_router, b_router, w_gate_up, b_gate_up, w_down, b_down)
    for l in range(DEPTH):
        x = hybrid_layer(x, *[p[l] for p in layer_params])
    return x
```

```python
import functools

import jax
import jax.numpy as jnp
from jax import lax
from jax.experimental import pallas as pl
from jax.experimental.pallas import tpu as pltpu

F32 = jnp.float32
BF16 = jnp.bfloat16
I32 = jnp.int32
U32 = jnp.uint32

D_MODEL = 2048
HEAD_DIM = 128
NSA_HEADS = 8
NSA_KV_HEADS = 2
HEADS_PER_GROUP = NSA_HEADS // NSA_KV_HEADS
NSA_WIDTH = NSA_HEADS * HEAD_DIM
N_BRANCH = 3
CMP_BLK = 32
CMP_STRIDE = 16
CMP_HIDDEN = 256
SLC_BLK = 64
SLC_SHIFT = SLC_BLK.bit_length() - 1
N_SEL = 16
WINDOW = 512
ROPE_THETA = 500000.0
ROT_DIM = HEAD_DIM // 4
POOL_WIDTH = 1024
POOL_WINDOWS = (2, 4, 8, 16)
POOL_GC = 256
N_EXPERTS = 32
TOP_K = 4
D_FF = 2048
SWIGLU_ALPHA = 1.702
SWIGLU_LIMIT = 7.0
RMS_EPS = 1e-5
QK_EPS = 1e-6
MASK_VALUE = -1e30
FORCE_VALUE = 1e30

LANES = 128
N_KV_COLS = 6 * NSA_KV_HEADS
GATE_PAD = 16
N_BLK_PAD = 128
POOL_HALO = 16

TM_IN = 256
TM_PREP = 256
TQ = 256
TK = 256
TM_MIX = 256
TT_ROUTE = 256
TM_G = 256
TN_G1 = 1024
TN_G2 = 1024
TT_DISP = 256
TT_COMB = 128
VMEM_LIMIT = 56 * 1024 * 1024


def _cparams(n_axes, vmem=VMEM_LIMIT):
    return pltpu.CompilerParams(
        dimension_semantics=("arbitrary",) * n_axes, vmem_limit_bytes=vmem)


def _dot(a, b):
    return jnp.dot(a, b, preferred_element_type=F32)


def _dot_nt(a, b):
    return lax.dot_general(a, b, (((1,), (1,)), ((), ())), preferred_element_type=F32)


def _inproj_kernel(x_ref, g_ref, w_ref, q_ref, kvc_ref, kv8_ref, pool_ref, gate_ref):
    x = x_ref[...]
    y = x * lax.rsqrt(jnp.mean(x * x, axis=-1, keepdims=True) + RMS_EPS)
    h = (y * g_ref[...]).astype(BF16)
    q_ref[...] = _dot(h, w_ref[:, 0:NSA_WIDTH])
    base = NSA_WIDTH
    for c in range(4):
        kvc_ref[c] = _dot(h, w_ref[:, base + c * LANES: base + (c + 1) * LANES])
    base += 4 * LANES
    for c in range(8):
        kv8_ref[c] = _dot(h, w_ref[:, base + c * LANES: base + (c + 1) * LANES])
    base += 8 * LANES
    pool_ref[...] = _dot(h, w_ref[:, base: base + POOL_WIDTH])
    base += POOL_WIDTH
    gate_ref[...] = _dot(h, w_ref[:, base: base + LANES])


def _inproj(x2, g1, w_perm):
    T = x2.shape[0]
    n_cols = w_perm.shape[1]
    return pl.pallas_call(
        _inproj_kernel,
        grid=(T // TM_IN,),
        in_specs=[
            pl.BlockSpec((TM_IN, D_MODEL), lambda i: (i, 0)),
            pl.BlockSpec((1, D_MODEL), lambda i: (0, 0)),
            pl.BlockSpec((D_MODEL, n_cols), lambda i: (0, 0)),
        ],
        out_specs=[
            pl.BlockSpec((TM_IN, NSA_WIDTH), lambda i: (i, 0)),
            pl.BlockSpec((4, TM_IN, LANES), lambda i: (0, i, 0)),
            pl.BlockSpec((8, TM_IN, LANES), lambda i: (0, i, 0)),
            pl.BlockSpec((TM_IN, POOL_WIDTH), lambda i: (i, 0)),
            pl.BlockSpec((TM_IN, LANES), lambda i: (i, 0)),
        ],
        out_shape=[
            jax.ShapeDtypeStruct((T, NSA_WIDTH), F32),
            jax.ShapeDtypeStruct((4, T, LANES), F32),
            jax.ShapeDtypeStruct((8, T, LANES), F32),
            jax.ShapeDtypeStruct((T, POOL_WIDTH), F32),
            jax.ShapeDtypeStruct((T, LANES), F32),
        ],
        compiler_params=_cparams(1),
        name="inproj",
    )(x2, g1, w_perm)


def _norm_rope(x, g, cos, sin_lo, sin_hi):
    half = ROT_DIM // 2
    y = x * lax.rsqrt(jnp.mean(x * x, axis=-1, keepdims=True) + QK_EPS) * g
    return (y * cos + pltpu.roll(y, LANES - half, 1) * sin_lo
            + pltpu.roll(y, half, 1) * sin_hi)


def _prep_kernel(q_ref, kv_ref, gate_ref, cos_ref, slo_ref, shi_ref, qg_ref, ksg_ref, kwg_ref,
                 qn_ref, ks_ref, vst_ref, kw_ref, vwt_ref, gt_ref):
    cos, slo, shi = cos_ref[...], slo_ref[...], shi_ref[...]
    scale = HEAD_DIM ** -0.5
    for h in range(NSA_HEADS):
        cs = slice(h * HEAD_DIM, (h + 1) * HEAD_DIM)
        qn_ref[:, cs] = (_norm_rope(q_ref[:, cs], qg_ref[...], cos, slo, shi) * scale).astype(BF16)
    i = pl.program_id(1)
    row = i * TM_PREP + lax.broadcasted_iota(I32, (TM_PREP, N_BLK_PAD), 0)
    lane = lax.broadcasted_iota(I32, (TM_PREP, N_BLK_PAD), 1)
    onehot = jnp.where((row >> SLC_SHIFT) == lane, 1.0, 0.0).astype(BF16)
    for g in range(NSA_KV_HEADS):
        ks_ref[g, :, 0:HEAD_DIM] = _norm_rope(kv_ref[0 + g], ksg_ref[...], cos, slo, shi).astype(BF16)
        ks_ref[g, :, HEAD_DIM:HEAD_DIM + N_BLK_PAD] = onehot
        vst_ref[g, 0] = kv_ref[2 + g].T.astype(BF16)
        kw_ref[g] = _norm_rope(kv_ref[4 + g], kwg_ref[...], cos, slo, shi).astype(BF16)
        vwt_ref[g, 0] = kv_ref[6 + g].T.astype(BF16)
    sig_t = jax.nn.sigmoid(gate_ref[...]).T
    for g in range(NSA_KV_HEADS):
        gt_ref[g] = sig_t[g * GATE_PAD:(g + 1) * GATE_PAD]


def _prep(q, kv8, gate, cos, slo, shi, qg, ksg, kwg, B, S):
    nS = S // TM_PREP
    G = NSA_KV_HEADS
    tab = pl.BlockSpec((TM_PREP, LANES), lambda b, i: (i, 0))
    vec = pl.BlockSpec((1, HEAD_DIM), lambda b, i: (0, 0))
    return pl.pallas_call(
        _prep_kernel,
        grid=(B, nS),
        in_specs=[
            pl.BlockSpec((TM_PREP, NSA_WIDTH), lambda b, i: (b * nS + i, 0)),
            pl.BlockSpec((8, TM_PREP, LANES), lambda b, i: (0, b * nS + i, 0)),
            pl.BlockSpec((TM_PREP, LANES), lambda b, i: (b * nS + i, 0)),
            tab, tab, tab, vec, vec, vec,
        ],
        out_specs=[
            pl.BlockSpec((TM_PREP, NSA_WIDTH), lambda b, i: (b * nS + i, 0)),
            pl.BlockSpec((G, TM_PREP, 2 * HEAD_DIM), lambda b, i: (b, i, 0)),
            pl.BlockSpec((G, 1, HEAD_DIM, TM_PREP), lambda b, i: (b, i, 0, 0)),
            pl.BlockSpec((G, TM_PREP, HEAD_DIM), lambda b, i: (b, i, 0)),
            pl.BlockSpec((G, 1, HEAD_DIM, TM_PREP), lambda b, i: (b, i, 0, 0)),
            pl.BlockSpec((G, GATE_PAD, TM_PREP), lambda b, i: (b, 0, i)),
        ],
        out_shape=[
            jax.ShapeDtypeStruct((B * S, NSA_WIDTH), BF16),
            jax.ShapeDtypeStruct((B * G, S, 2 * HEAD_DIM), BF16),
            jax.ShapeDtypeStruct((B * G, nS, HEAD_DIM, TM_PREP), BF16),
            jax.ShapeDtypeStruct((B * G, S, HEAD_DIM), BF16),
            jax.ShapeDtypeStruct((B * G, nS, HEAD_DIM, TM_PREP), BF16),
            jax.ShapeDtypeStruct((B * G, GATE_PAD, S), F32),
        ],
        compiler_params=_cparams(2),
        name="prep",
    )(q, kv8, gate, cos, slo, shi, qg, ksg, kwg)


def _compress_body(x_ref, pos_ref, w1_ref, b1_ref, w2_ref, b2_ref):
    x = x_ref[0, 0]
    nc = x.shape[0]
    xa = (x + pos_ref[0:1, :]).astype(BF16)
    xb = (x + pos_ref[1:2, :]).astype(BF16)
    first = _dot(xa, w1_ref[0])
    second = _dot(xb, w1_ref[1])
    nxt = pltpu.roll(second, nc - 1, 0)
    hid = jax.nn.gelu(first + nxt + b1_ref[...])
    return _dot(hid.astype(BF16), w2_ref[...]) + b2_ref[...]


def _compress_k_kernel(x_ref, pos_ref, w1_ref, b1_ref, w2_ref, b2_ref,
                       g_ref, cos_ref, slo_ref, shi_ref, o_ref):
    out = _compress_body(x_ref, pos_ref, w1_ref, b1_ref, w2_ref, b2_ref)
    o_ref[0] = _norm_rope(out, g_ref[...], cos_ref[...], slo_ref[...], shi_ref[...]).astype(BF16)


def _compress_v_kernel(x_ref, pos_ref, w1_ref, b1_ref, w2_ref, b2_ref, o_ref):
    out = _compress_body(x_ref, pos_ref, w1_ref, b1_ref, w2_ref, b2_ref)
    o_ref[0] = out.T.astype(BF16)


def _compress(kvc4, which, pos2, w1, b1, w2, b2, B, S, rope=None):
    NC = S // CMP_STRIDE
    G = NSA_KV_HEADS
    CW = CMP_STRIDE * HEAD_DIM
    common_specs = [
        pl.BlockSpec((1, 1, NC, CW), lambda b, g: (which * G + g, b, 0, 0)),
        pl.BlockSpec((2, CW), lambda b, g: (0, 0)),
        pl.BlockSpec((2, CW, CMP_HIDDEN), lambda b, g: (0, 0, 0)),
        pl.BlockSpec((1, CMP_HIDDEN), lambda b, g: (0, 0)),
        pl.BlockSpec((CMP_HIDDEN, HEAD_DIM), lambda b, g: (0, 0)),
        pl.BlockSpec((1, HEAD_DIM), lambda b, g: (0, 0)),
    ]
    if rope is not None:
        gk, cos, slo, shi = rope
        tab = pl.BlockSpec((NC, LANES), lambda b, g: (0, 0))
        return pl.pallas_call(
            _compress_k_kernel,
            grid=(B, G),
            in_specs=common_specs + [pl.BlockSpec((1, HEAD_DIM), lambda b, g: (0, 0)), tab, tab, tab],
            out_specs=pl.BlockSpec((1, NC, HEAD_DIM), lambda b, g: (b * G + g, 0, 0)),
            out_shape=jax.ShapeDtypeStruct((B * G, NC, HEAD_DIM), BF16),
            compiler_params=_cparams(2),
            name="compress_k",
        )(kvc4, pos2, w1, b1, w2, b2, gk, cos, slo, shi)
    return pl.pallas_call(
        _compress_v_kernel,
        grid=(B, G),
        in_specs=common_specs,
        out_specs=pl.BlockSpec((1, HEAD_DIM, NC), lambda b, g: (b * G + g, 0, 0)),
        out_shape=jax.ShapeDtypeStruct((B * G, HEAD_DIM, NC), BF16),
        compiler_params=_cparams(2),
        name="compress_v",
    )(kvc4, pos2, w1, b1, w2, b2)


def _nsa_kernel(q_ref, kc_ref, vct_ref, ks_ref, vst_ref, kw_ref, vwt_ref, gt_ref, o_ref,
                qaug_ref, m_ref, l_ref, acc_ref, *, S):
    R = HEADS_PER_GROUP
    NQ = R * TQ
    NC = S // CMP_STRIDE
    qi = pl.program_id(2)
    qs = qi * TQ

    q = q_ref[...]
    q_stack = jnp.concatenate([q[:, r * HEAD_DIM:(r + 1) * HEAD_DIM] for r in range(R)], axis=0)
    col = lax.broadcasted_iota(I32, (1, NQ), 1)
    t_row = qs + (col & (TQ - 1))

    s_c = _dot_nt(kc_ref[0], q_stack)
    c_end = lax.broadcasted_iota(I32, (NC, 1), 0) * CMP_STRIDE + (CMP_BLK - 1)
    mask_c = c_end <= t_row
    s_c = jnp.where(mask_c, s_c, MASK_VALUE)
    m_c = jnp.max(s_c, axis=0, keepdims=True)
    e_c = jnp.where(mask_c, jnp.exp(s_c - m_c), 0.0)
    l_c = jnp.sum(e_c, axis=0, keepdims=True)
    p_c = e_c * jnp.where(l_c > 0.0, 1.0 / l_c, 0.0)
    o_c = _dot(vct_ref[0], p_c.astype(BF16))

    imp = p_c[:, 0:TQ]
    for r in range(1, R):
        imp = imp + p_c[:, r * TQ:(r + 1) * TQ]
    jj = lax.broadcasted_iota(I32, (N_BLK_PAD, NC), 0) * (SLC_BLK // CMP_STRIDE)
    nn = lax.broadcasted_iota(I32, (N_BLK_PAD, NC), 1)
    per = SLC_BLK // CMP_STRIDE
    fold = (jnp.where((nn >= jj) & (nn < jj + per), 1.0, 0.0)
            + jnp.where((nn >= jj - 1) & (nn < jj + per - 1), 1.0, 0.0)).astype(BF16)
    imp_hi = imp.astype(BF16)
    imp_lo = (imp - imp_hi.astype(F32)).astype(BF16)
    blk_score = _dot(fold, imp_hi) + _dot(fold, imp_lo)

    jb = lax.broadcasted_iota(I32, (N_BLK_PAD, TQ), 0)
    cur = (qs + lax.broadcasted_iota(I32, (N_BLK_PAD, TQ), 1)) >> SLC_SHIFT
    forced = (jb == 0) | (jb == cur) | (jb == cur - 1)
    val = jnp.where(forced, FORCE_VALUE, jnp.where(jb <= cur, blk_score, MASK_VALUE))
    jbf = jb.astype(F32)
    bias = jnp.full((N_BLK_PAD, TQ), MASK_VALUE, F32)
    for _ in range(min(N_SEL, S // SLC_BLK)):
        mx = jnp.max(val, axis=0, keepdims=True)
        first = jnp.min(jnp.where(val == mx, jbf, float(N_BLK_PAD)), axis=0, keepdims=True)
        pick = jbf == first
        bias = jnp.where(pick, 0.0, bias)
        val = jnp.where(pick, -jnp.inf, val)
    bias_q = bias.T.astype(BF16)
    qaug_ref[:, 0:HEAD_DIM] = q_stack
    qaug_ref[:, HEAD_DIM:HEAD_DIM + N_BLK_PAD] = jnp.concatenate([bias_q] * R, axis=0)

    m_ref[...] = jnp.full((1, NQ), MASK_VALUE, F32)
    l_ref[...] = jnp.zeros((1, NQ), F32)
    acc_ref[...] = jnp.zeros((HEAD_DIM, NQ), F32)

    def sel_tile(kj, causal):
        k = ks_ref[0, pl.ds(pl.multiple_of(kj * TK, TK), TK), :]
        s = _dot_nt(k, qaug_ref[...])
        if causal:
            kpos = kj * TK + lax.broadcasted_iota(I32, (TK, 1), 0)
            s = jnp.where(kpos <= t_row, s, MASK_VALUE)
        m_old = m_ref[...]
        m_new = jnp.maximum(m_old, jnp.max(s, axis=0, keepdims=True))
        alpha = jnp.exp(m_old - m_new)
        p = jnp.exp(s - m_new)
        l_ref[...] = alpha * l_ref[...] + jnp.sum(p, axis=0, keepdims=True)
        acc_ref[...] = alpha * acc_ref[...] + _dot(vst_ref[0, kj], p.astype(BF16))
        m_ref[...] = m_new

    def past_tile(kj, carry):
        sel_tile(kj, False)
        return carry

    lax.fori_loop(0, qi, past_tile, 0)
    sel_tile(qi, True)
    o_s = acc_ref[...] * (1.0 / l_ref[...])

    kt0 = jnp.maximum(qi - WINDOW // TK, 0)
    n_wt = WINDOW // TK + 1
    kw = kw_ref[0, pl.ds(pl.multiple_of(kt0 * TK, TK), n_wt * TK), :]
    s_w = _dot_nt(kw, q_stack)
    diff = t_row - (kt0 * TK + lax.broadcasted_iota(I32, (n_wt * TK, 1), 0))
    s_w = jnp.where((diff >= 0) & (diff < WINDOW), s_w, MASK_VALUE)
    e_w = jnp.exp(s_w - jnp.max(s_w, axis=0, keepdims=True))
    p_w = (e_w * (1.0 / jnp.sum(e_w, axis=0, keepdims=True))).astype(BF16)
    o_w = _dot(vwt_ref[0, kt0], p_w[0:TK])
    for w in range(1, n_wt):
        o_w = o_w + _dot(vwt_ref[0, kt0 + w], p_w[w * TK:(w + 1) * TK])

    for r in range(R):
        sl = slice(r * TQ, (r + 1) * TQ)
        g0 = gt_ref[0, r * N_BRANCH + 0:r * N_BRANCH + 1, :]
        g1 = gt_ref[0, r * N_BRANCH + 1:r * N_BRANCH + 2, :]
        g2 = gt_ref[0, r * N_BRANCH + 2:r * N_BRANCH + 3, :]
        o = g0 * o_c[:, sl] + g1 * o_s[:, sl] + g2 * o_w[:, sl]
        o_ref[:, r * HEAD_DIM:(r + 1) * HEAD_DIM] = o.T.astype(BF16)


def _nsa(qn, kc, vct, ks, vst, kw, vwt, gt, B, S):
    G = NSA_KV_HEADS
    R = HEADS_PER_GROUP
    nQ = S // TQ
    nK = S // TK
    NC = S // CMP_STRIDE
    bg = lambda b, g, i: b * G + g
    return pl.pallas_call(
        functools.partial(_nsa_kernel, S=S),
        grid=(B, G, nQ),
        in_specs=[
            pl.BlockSpec((TQ, R * HEAD_DIM), lambda b, g, i: (b * nQ + i, g)),
            pl.BlockSpec((1, NC, HEAD_DIM), lambda b, g, i: (bg(b, g, i), 0, 0)),
            pl.BlockSpec((1, HEAD_DIM, NC), lambda b, g, i: (bg(b, g, i), 0, 0)),
            pl.BlockSpec((1, S, 2 * HEAD_DIM), lambda b, g, i: (bg(b, g, i), 0, 0)),
            pl.BlockSpec((1, nK, HEAD_DIM, TK), lambda b, g, i: (bg(b, g, i), 0, 0, 0)),
            pl.BlockSpec((1, S, HEAD_DIM), lambda b, g, i: (bg(b, g, i), 0, 0)),
            pl.BlockSpec((1, nK, HEAD_DIM, TK), lambda b, g, i: (bg(b, g, i), 0, 0, 0)),
            pl.BlockSpec((1, GATE_PAD, TQ), lambda b, g, i: (bg(b, g, i), 0, i)),
        ],
        out_specs=pl.BlockSpec((TQ, R * HEAD_DIM), lambda b, g, i: (b * nQ + i, g)),
        out_shape=jax.ShapeDtypeStruct((B * S, NSA_WIDTH), BF16),
        scratch_shapes=[
            pltpu.VMEM((R * TQ, 2 * HEAD_DIM), BF16),
            pltpu.VMEM((1, R * TQ), F32),
            pltpu.VMEM((1, R * TQ), F32),
            pltpu.VMEM((HEAD_DIM, R * TQ), F32),
        ],
        compiler_params=_cparams(3),
        name="nsa",
    )(qn, kc, vct, ks, vst, kw, vwt, gt)


def _mix_kernel(nsa_ref, pool_ref, halo_ref, x_ref, wo_ref, pw_ref, ps_ref, g2_ref,
                wrh_ref, wrl_ref, br_ref, x1_ref, h2p_ref, logit_ref, ext_ref, *, S):
    nS = S // TM_MIX
    i = pl.program_id(0)
    t0 = lax.rem(i, nS) * TM_MIX
    ext_ref[0:POOL_HALO, :] = jnp.where(t0 == 0, 0.0, halo_ref[...])
    ext_ref[POOL_HALO:POOL_HALO + TM_MIX, :] = pool_ref[...]
    t = t0 + lax.broadcasted_iota(I32, (TM_MIX, POOL_GC), 0)
    acc = _dot(nsa_ref[...], wo_ref[0:NSA_WIDTH, :])
    for gi, w in enumerate(POOL_WINDOWS):
        cs = slice(gi * POOL_GC, (gi + 1) * POOL_GC)
        v = pool_ref[:, cs]
        tot = v
        for k in range(1, w):
            tot = tot + ext_ref[POOL_HALO - k:POOL_HALO - k + TM_MIX, cs]
        cnt = jnp.minimum(t + 1, w).astype(F32)
        d = tot / cnt - v
        y = _dot(d.astype(BF16), pw_ref[gi]) * ps_ref[:, cs]
        acc = acc + _dot(y.astype(BF16), wo_ref[NSA_WIDTH + gi * POOL_GC:NSA_WIDTH + (gi + 1) * POOL_GC, :])
    x1 = x_ref[...] + acc
    x1_ref[...] = x1
    h2 = x1 * lax.rsqrt(jnp.mean(x1 * x1, axis=-1, keepdims=True) + RMS_EPS) * g2_ref[...]
    half = D_MODEL // 2
    h2p_ref[...] = pltpu.pack_elementwise([h2[:, 0:half], h2[:, half:D_MODEL]], packed_dtype=BF16)
    hi = h2.astype(BF16)
    lo = (h2 - hi.astype(F32)).astype(BF16)
    logit_ref[...] = (_dot(hi, wrh_ref[...]) + _dot(lo, wrh_ref[...]) + _dot(hi, wrl_ref[...])
                      + br_ref[...])


def _mix(nsa_out, pool_in, x2, wo, pw, ps, g2, wrh, wrl, br, S):
    T = x2.shape[0]
    hb = TM_MIX // POOL_HALO
    full = lambda shape: pl.BlockSpec(shape, lambda i: (0,) * len(shape))
    return pl.pallas_call(
        functools.partial(_mix_kernel, S=S),
        grid=(T // TM_MIX,),
        in_specs=[
            pl.BlockSpec((TM_MIX, NSA_WIDTH), lambda i: (i, 0)),
            pl.BlockSpec((TM_MIX, POOL_WIDTH), lambda i: (i, 0)),
            pl.BlockSpec((POOL_HALO, POOL_WIDTH), lambda i: (jnp.maximum(i * hb - 1, 0), 0)),
            pl.BlockSpec((TM_MIX, D_MODEL), lambda i: (i, 0)),
            full((D_MODEL, D_MODEL)),
            full((len(POOL_WINDOWS), POOL_GC, POOL_GC)),
            full((1, POOL_WIDTH)),
            full((1, D_MODEL)),
            full((D_MODEL, LANES)),
            full((D_MODEL, LANES)),
            full((1, LANES)),
        ],
        out_specs=[
            pl.BlockSpec((TM_MIX, D_MODEL), lambda i: (i, 0)),
            pl.BlockSpec((TM_MIX, D_MODEL // 2), lambda i: (i, 0)),
            pl.BlockSpec((TM_MIX, LANES), lambda i: (i, 0)),
        ],
        out_shape=[
            jax.ShapeDtypeStruct((T, D_MODEL), F32),
            jax.ShapeDtypeStruct((T, D_MODEL // 2), U32),
            jax.ShapeDtypeStruct((T, LANES), F32),
        ],
        scratch_shapes=[pltpu.VMEM((POOL_HALO + TM_MIX, POOL_WIDTH), F32)],
        compiler_params=_cparams(1),
        name="mix",
    )(nsa_out, pool_in, pool_in, x2, wo, pw, ps, g2, wrh, wrl, br)


def _route_kernel(logit_ref, dest_ref, wt_ref, meta_ref, cnt_ref, run_ref, *, n_tiles):
    phase = pl.program_id(0)
    i = pl.program_id(1)
    TT = TT_ROUTE
    E = N_EXPERTS

    @pl.when((phase == 0) & (i == 0))
    def _():
        cnt_ref[...] = jnp.zeros_like(cnt_ref)
        run_ref[...] = jnp.zeros_like(run_ref)

    lt = logit_ref[...].T[0:E, :]
    ef = lax.broadcasted_iota(I32, (E, TT), 0).astype(F32)
    work = lt
    ids, vals, hots = [], [], []
    for _ in range(TOP_K):
        mx = jnp.max(work, axis=0, keepdims=True)
        eid = jnp.min(jnp.where(work == mx, ef, float(E)), axis=0, keepdims=True)
        hot = ef == eid
        ids.append(eid)
        vals.append(mx)
        hots.append(hot)
        work = jnp.where(hot, -jnp.inf, work)
    member = jnp.where(hots[0] | hots[1] | hots[2] | hots[3], 1.0, 0.0)
    tile_cnt = jnp.sum(member, axis=1, keepdims=True)

    @pl.when(phase == 0)
    def _():
        cnt_ref[...] = cnt_ref[...] + tile_cnt

    @pl.when(phase == 1)
    def _():
        ex = [jnp.exp(v - vals[0]) for v in vals]
        den = ex[0] + ex[1] + ex[2] + ex[3]
        cnt = jnp.broadcast_to(cnt_ref[...], (E, LANES))
        padded = jnp.ceil(cnt * (1.0 / TM_G)) * TM_G
        erow = lax.broadcasted_iota(I32, (E, LANES), 0)
        start = jnp.zeros((E, LANES), F32)
        running = jnp.zeros((1, LANES), F32)
        for e in range(E):
            start = jnp.where(erow == e, running, start)
            running = running + padded[e:e + 1, :]
        end = start + padded
        tri = jnp.where(lax.broadcasted_iota(I32, (TT, TT), 0) < lax.broadcasted_iota(I32, (TT, TT), 1),
                        1.0, 0.0).astype(BF16)
        before = _dot(member.astype(BF16), tri) + run_ref[...]
        slot = before + start[:, 0:1]
        for k in range(TOP_K):
            d = jnp.sum(jnp.where(hots[k], slot, 0.0), axis=0, keepdims=True)
            dest_ref[k:k + 1, :] = d.astype(I32)
            wt_ref[k:k + 1, :] = ex[k] / den
        run_ref[...] = run_ref[...] + tile_cnt

        @pl.when(i == 0)
        def _():
            lanes = meta_ref.shape[1]
            endw = jnp.broadcast_to(end[:, 0:1], (E, lanes))
            tile_row = (lax.broadcasted_iota(I32, (E, lanes), 1) * TM_G).astype(F32)
            owner = jnp.sum(jnp.where(endw <= tile_row, 1.0, 0.0), axis=0, keepdims=True)
            meta_ref[0:1, :] = jnp.minimum(owner, float(E - 1)).astype(I32)
            meta_ref[1:2, :] = jnp.where(tile_row[0:1, :] < running[:, 0:1], 1, 0).astype(I32)
            meta_ref[2:8, :] = jnp.zeros((6, lanes), I32)


def _route(logits, n_tiles):
    T = logits.shape[0]
    nT = T // TT_ROUTE
    lanes = -(-n_tiles // LANES) * LANES
    return pl.pallas_call(
        functools.partial(_route_kernel, n_tiles=n_tiles),
        grid=(2, nT),
        in_specs=[pl.BlockSpec((TT_ROUTE, LANES), lambda p, i: (i, 0))],
        out_specs=[
            pl.BlockSpec((TOP_K, TT_ROUTE), lambda p, i: (0, i * p)),
            pl.BlockSpec((TOP_K, TT_ROUTE), lambda p, i: (0, i * p)),
            pl.BlockSpec((8, lanes), lambda p, i: (0, 0)),
        ],
        out_shape=[
            jax.ShapeDtypeStruct((TOP_K, T), I32),
            jax.ShapeDtypeStruct((TOP_K, T), F32),
            jax.ShapeDtypeStruct((8, lanes), I32),
        ],
        scratch_shapes=[pltpu.VMEM((N_EXPERTS, 1), F32), pltpu.VMEM((N_EXPERTS, 1), F32)],
        compiler_params=_cparams(2),
        name="route",
    )(logits)


def _dispatch_kernel(dest_ref, h_ref, zeros_ref, xs_ref, sem):
    del zeros_ref

    def row_copy(t, k):
        return pltpu.make_async_copy(h_ref.at[pl.ds(t, 1), :],
                                     xs_ref.at[pl.ds(dest_ref[k, t], 1), :], sem)

    def issue(t, carry):
        for k in range(TOP_K):
            row_copy(t, k).start()
        return carry

    lax.fori_loop(0, TT_DISP, issue, 0)

    def drain(t, carry):
        for k in range(TOP_K):
            row_copy(t, k).wait()
        return carry

    lax.fori_loop(0, TT_DISP, drain, 0)


def _dispatch(dest, h2p, n_rows):
    T, W = h2p.shape
    zeros = jnp.zeros((n_rows, W), U32)
    return pl.pallas_call(
        _dispatch_kernel,
        grid=(T // TT_DISP,),
        in_specs=[
            pl.BlockSpec((TOP_K, TT_DISP), lambda i: (0, i), memory_space=pltpu.SMEM),
            pl.BlockSpec((TT_DISP, W), lambda i: (i, 0)),
            pl.BlockSpec(memory_space=pl.ANY),
        ],
        out_specs=pl.BlockSpec(memory_space=pl.ANY),
        out_shape=jax.ShapeDtypeStruct((n_rows, W), U32),
        scratch_shapes=[pltpu.SemaphoreType.DMA],
        input_output_aliases={2: 0},
        compiler_params=_cparams(1),
        name="dispatch",
    )(dest, h2p, zeros)


def _expert_changed(te_ref, i):
    return (i == 0) | (te_ref[i] != te_ref[jnp.maximum(i - 1, 0)])


def _gemm1_kernel(te_ref, tv_ref, x_ref, wg_ref, wu_ref, bg_ref, bu_ref, act_ref, wbf_ref):
    i = pl.program_id(1)
    valid = tv_ref[i] > 0

    @pl.when(valid & _expert_changed(te_ref, i))
    def _():
        wbf_ref[:, 0:TN_G1] = wg_ref[0].astype(BF16)
        wbf_ref[:, TN_G1:2 * TN_G1] = wu_ref[0].astype(BF16)

    @pl.when(valid)
    def _():
        half = D_MODEL // 2
        xp = x_ref[...]
        lo = pltpu.unpack_elementwise(xp, index=0, packed_dtype=BF16, unpacked_dtype=F32).astype(BF16)
        hi = pltpu.unpack_elementwise(xp, index=1, packed_dtype=BF16, unpacked_dtype=F32).astype(BF16)
        gu = _dot(lo, wbf_ref[0:half, :]) + _dot(hi, wbf_ref[half:D_MODEL, :])
        gate = jnp.minimum(gu[:, 0:TN_G1] + bg_ref[0], SWIGLU_LIMIT)
        up = jnp.clip(gu[:, TN_G1:2 * TN_G1] + bu_ref[0], -SWIGLU_LIMIT, SWIGLU_LIMIT)
        act = (up + 1.0) * gate * jax.nn.sigmoid(SWIGLU_ALPHA * gate)
        act_ref[...] = act.astype(BF16)

    @pl.when(jnp.logical_not(valid))
    def _():
        act_ref[...] = jnp.zeros_like(act_ref)


def _gemm1(te, tv, xs, w_gu, b_gu3, n_tiles):
    n_rows = xs.shape[0]
    nJ = D_FF // TN_G1
    grid_spec = pltpu.PrefetchScalarGridSpec(
        num_scalar_prefetch=2,
        grid=(nJ, n_tiles),
        in_specs=[
            pl.BlockSpec((TM_G, D_MODEL // 2), lambda j, i, te, tv: (i, 0)),
            pl.BlockSpec((1, D_MODEL, TN_G1), lambda j, i, te, tv: (te[i], 0, j)),
            pl.BlockSpec((1, D_MODEL, TN_G1), lambda j, i, te, tv: (te[i], 0, nJ + j)),
            pl.BlockSpec((1, 1, TN_G1), lambda j, i, te, tv: (te[i], 0, j)),
            pl.BlockSpec((1, 1, TN_G1), lambda j, i, te, tv: (te[i], 0, nJ + j)),
        ],
        out_specs=pl.BlockSpec((TM_G, TN_G1), lambda j, i, te, tv: (i, j)),
        scratch_shapes=[pltpu.VMEM((D_MODEL, 2 * TN_G1), BF16)],
    )
    return pl.pallas_call(
        _gemm1_kernel,
        grid_spec=grid_spec,
        out_shape=jax.ShapeDtypeStruct((n_rows, D_FF), BF16),
        compiler_params=_cparams(2),
        name="gemm1",
    )(te, tv, xs, w_gu, w_gu, b_gu3, b_gu3)


def _gemm2_kernel(te_ref, tv_ref, a_ref, w_ref, b_ref, y_ref, wbf_ref):
    i = pl.program_id(1)
    valid = tv_ref[i] > 0

    @pl.when(valid & _expert_changed(te_ref, i))
    def _():
        wbf_ref[...] = w_ref[0].astype(BF16)

    @pl.when(valid)
    def _():
        y_ref[...] = _dot(a_ref[...], wbf_ref[...]) + b_ref[0]

    @pl.when(jnp.logical_not(valid))
    def _():
        y_ref[...] = jnp.zeros_like(y_ref)


def _gemm2(te, tv, act, w_d, b_d3, n_tiles):
    n_rows = act.shape[0]
    nJ = D_MODEL // TN_G2
    grid_spec = pltpu.PrefetchScalarGridSpec(
        num_scalar_prefetch=2,
        grid=(nJ, n_tiles),
        in_specs=[
            pl.BlockSpec((TM_G, D_FF), lambda j, i, te, tv: (i, 0)),
            pl.BlockSpec((1, D_FF, TN_G2), lambda j, i, te, tv: (te[i], 0, j)),
            pl.BlockSpec((1, 1, TN_G2), lambda j, i, te, tv: (te[i], 0, j)),
        ],
        out_specs=pl.BlockSpec((TM_G, TN_G2), lambda j, i, te, tv: (i, j)),
        scratch_shapes=[pltpu.VMEM((D_FF, TN_G2), BF16)],
    )
    return pl.pallas_call(
        _gemm2_kernel,
        grid_spec=grid_spec,
        out_shape=jax.ShapeDtypeStruct((n_rows, D_MODEL), F32),
        compiler_params=_cparams(2),
        name="gemm2",
    )(te, tv, act, w_d, b_d3)


def _combine_kernel(dest_ref, wt_ref, x1_ref, y_ref, o_ref, buf_ref, sem):
    def row_copy(t, k):
        return pltpu.make_async_copy(y_ref.at[pl.ds(dest_ref[k, t], 1), :],
                                     buf_ref.at[k, pl.ds(t, 1), :], sem)

    def issue(t, carry):
        for k in range(TOP_K):
            row_copy(t, k).start()
        return carry

    lax.fori_loop(0, TT_COMB, issue, 0)

    def drain(t, carry):
        for k in range(TOP_K):
            row_copy(t, k).wait()
        return carry

    lax.fori_loop(0, TT_COMB, drain, 0)
    out = x1_ref[...]
    for k in range(TOP_K):
        out = out + buf_ref[k] * wt_ref[:, k:k + 1]
    o_ref[...] = out


def _combine(dest, wt_tok, x1, y):
    T = x1.shape[0]
    return pl.pallas_call(
        _combine_kernel,
        grid=(T // TT_COMB,),
        in_specs=[
            pl.BlockSpec((TOP_K, TT_COMB), lambda i: (0, i), memory_space=pltpu.SMEM),
            pl.BlockSpec((TT_COMB, TOP_K), lambda i: (i, 0)),
            pl.BlockSpec((TT_COMB, D_MODEL), lambda i: (i, 0)),
            pl.BlockSpec(memory_space=pl.ANY),
        ],
        out_specs=pl.BlockSpec((TT_COMB, D_MODEL), lambda i: (i, 0)),
        out_shape=jax.ShapeDtypeStruct((T, D_MODEL), F32),
        scratch_shapes=[pltpu.VMEM((TOP_K, TT_COMB, D_MODEL), F32), pltpu.SemaphoreType.DMA],
        compiler_params=_cparams(1),
        name="combine",
    )(dest, wt_tok, x1, y)


def _rope_tables(pos):
    half = ROT_DIM // 2
    inv_freq = ROPE_THETA ** (-jnp.arange(0, ROT_DIM, 2, dtype=F32) / ROT_DIM)
    ang = pos[:, None] * inv_freq[None, :]
    cos, sin = jnp.cos(ang), jnp.sin(ang)
    n = pos.shape[0]
    ones = jnp.ones((n, LANES - ROT_DIM), F32)
    zeros = jnp.zeros((n, LANES - ROT_DIM), F32)
    zh = jnp.zeros((n, half), F32)
    return (jnp.concatenate([cos, cos, ones], axis=1),
            jnp.concatenate([-sin, zh, zeros], axis=1),
            jnp.concatenate([zh, sin, zeros], axis=1))


def _permute_w_in(w_in):
    kv_end = NSA_WIDTH + N_KV_COLS * HEAD_DIM
    n_gate = NSA_HEADS * N_BRANCH
    per_g = HEADS_PER_GROUP * N_BRANCH
    gate = w_in[:, kv_end:kv_end + n_gate]
    pieces = [w_in[:, :kv_end], w_in[:, kv_end + n_gate:]]
    zpad = jnp.zeros((D_MODEL, GATE_PAD - per_g), w_in.dtype)
    for g in range(NSA_KV_HEADS):
        pieces += [gate[:, g * per_g:(g + 1) * per_g], zpad]
    pieces.append(jnp.zeros((D_MODEL, LANES - NSA_KV_HEADS * GATE_PAD), w_in.dtype))
    return jnp.concatenate(pieces, axis=1).astype(BF16)


def _layer(x, norm1_g, w_in, q_norm_g, k_norm_cmp_g, k_norm_slc_g, k_norm_win_g,
           cmp_k_pos, cmp_k_w1, cmp_k_b1, cmp_k_w2, cmp_k_b2,
           cmp_v_pos, cmp_v_w1, cmp_v_b1, cmp_v_w2, cmp_v_b2,
           pool_w, pool_scale, w_out, norm2_g,
           w_router, b_router, w_gate_up, b_gate_up, w_down, b_down):
    B, S, _ = x.shape
    T = B * S
    assert S % TQ == 0 and S >= WINDOW + TQ and S // SLC_BLK <= N_BLK_PAD
    assert (S // CMP_STRIDE) % LANES == 0 and T % TT_ROUTE == 0
    NC = S // CMP_STRIDE
    x2 = x.reshape(T, D_MODEL)
    row = lambda v: v.reshape(1, -1)

    q, kvc, kv8, pool_in, gate = _inproj(x2, row(norm1_g), _permute_w_in(w_in))

    cos, slo, shi = _rope_tables(jnp.arange(S, dtype=F32))
    qn, ks, vst, kw, vwt, gt = _prep(q, kv8, gate, cos, slo, shi, row(q_norm_g),
                                     row(k_norm_slc_g), row(k_norm_win_g), B, S)

    c_end = (jnp.arange(NC, dtype=I32) * CMP_STRIDE + (CMP_BLK - 1)).astype(F32)
    ccos, cslo, cshi = _rope_tables(c_end)
    kvc4 = kvc.reshape(4, B, NC, CMP_STRIDE * HEAD_DIM)
    cw = CMP_STRIDE * HEAD_DIM
    kc = _compress(kvc4, 0, cmp_k_pos.reshape(2, cw), cmp_k_w1.reshape(2, cw, CMP_HIDDEN).astype(BF16),
                   row(cmp_k_b1), cmp_k_w2.astype(BF16), row(cmp_k_b2), B, S,
                   rope=(row(k_norm_cmp_g), ccos, cslo, cshi))
    vct = _compress(kvc4, 1, cmp_v_pos.reshape(2, cw), cmp_v_w1.reshape(2, cw, CMP_HIDDEN).astype(BF16),
                    row(cmp_v_b1), cmp_v_w2.astype(BF16), row(cmp_v_b2), B, S)

    nsa_out = _nsa(qn, kc, vct, ks, vst, kw, vwt, gt, B, S)

    wr_pad = jnp.pad(w_router, ((0, 0), (0, LANES - N_EXPERTS)))
    wr_hi = wr_pad.astype(BF16)
    wr_lo = (wr_pad - wr_hi.astype(F32)).astype(BF16)
    br_pad = jnp.concatenate([b_router.astype(F32), jnp.full((LANES - N_EXPERTS,), MASK_VALUE, F32)])
    x1, h2p, logits = _mix(nsa_out, pool_in, x2, w_out.astype(BF16), pool_w.astype(BF16),
                           row(pool_scale), row(norm2_g), wr_hi, wr_lo, row(br_pad), S)

    n_tiles = T * TOP_K // TM_G + N_EXPERTS
    dest, wts, meta = _route(logits, n_tiles)
    te, tv = meta[0, :n_tiles], meta[1, :n_tiles]
    xs = _dispatch(dest, h2p, n_tiles * TM_G)
    act = _gemm1(te, tv, xs, w_gate_up, b_gate_up.reshape(N_EXPERTS, 1, 2 * D_FF), n_tiles)
    y = _gemm2(te, tv, act, w_down, b_down.reshape(N_EXPERTS, 1, D_MODEL), n_tiles)
    out = _combine(dest, wts.T, x1, y)
    return out.reshape(B, S, D_MODEL)


def kernel(x, norm1_g, w_in, q_norm_g, k_norm_cmp_g, k_norm_slc_g, k_norm_win_g, cmp_k_pos, cmp_k_w1, cmp_k_b1, cmp_k_w2, cmp_k_b2, cmp_v_pos, cmp_v_w1, cmp_v_b1, cmp_v_w2, cmp_v_b2, pool_w, pool_scale, w_out, norm2_g, w_router, b_router, w_gate_up, b_gate_up, w_down, b_down):
    params = (norm1_g, w_in, q_norm_g, k_norm_cmp_g, k_norm_slc_g, k_norm_win_g,
              cmp_k_pos, cmp_k_w1, cmp_k_b1, cmp_k_w2, cmp_k_b2,
              cmp_v_pos, cmp_v_w1, cmp_v_b1, cmp_v_w2, cmp_v_b2,
              pool_w, pool_scale, w_out, norm2_g,
              w_router, b_router, w_gate_up, b_gate_up, w_down, b_down)
    depth = norm1_g.shape[0]
    for l in range(depth):
        x = _layer(x, *[p.reshape(p.shape[1:]) if depth == 1 else p[l] for p in params])
    return x
```

```python
import functools

import jax
import jax.numpy as jnp
from jax import lax
from jax.experimental import pallas as pl
from jax.experimental.pallas import tpu as pltpu

F32 = jnp.float32
BF16 = jnp.bfloat16
I32 = jnp.int32
U32 = jnp.uint32

D_MODEL = 2048
HEAD_DIM = 128
NSA_HEADS = 8
NSA_KV_HEADS = 2
HEADS_PER_GROUP = NSA_HEADS // NSA_KV_HEADS
NSA_WIDTH = NSA_HEADS * HEAD_DIM
N_BRANCH = 3
CMP_BLK = 32
CMP_STRIDE = 16
CMP_HIDDEN = 256
SLC_BLK = 64
SLC_SHIFT = SLC_BLK.bit_length() - 1
N_SEL = 16
WINDOW = 512
ROPE_THETA = 500000.0
ROT_DIM = HEAD_DIM // 4
POOL_WIDTH = 1024
POOL_WINDOWS = (2, 4, 8, 16)
POOL_GC = 256
N_EXPERTS = 32
TOP_K = 4
D_FF = 2048
SWIGLU_ALPHA = 1.702
SWIGLU_LIMIT = 7.0
RMS_EPS = 1e-5
QK_EPS = 1e-6
MASK_VALUE = -1e30
FORCE_VALUE = 1e30
LOG2_E = 1.4426950408889634

LANES = 128
N_KV_COLS = 6 * NSA_KV_HEADS
GATE_PAD = 16
N_BLK_PAD = 128
POOL_HALO = 16

TM_IN = 256
TQ = 256
TK = 256
SEL_UNROLL = 4
TM_MIX = 256
MIX_SUB = 256
TT_ROUTE = 256
TM_G = 256
TN_G1 = 1024
TN_G2 = 1024
TT_DISP = 256
TT_COMB = 128
VMEM_LIMIT = 56 * 1024 * 1024


def _cparams(n_axes, vmem=VMEM_LIMIT):
    return pltpu.CompilerParams(
        dimension_semantics=("arbitrary",) * n_axes, vmem_limit_bytes=vmem)


def _dot(a, b):
    return jnp.dot(a, b, preferred_element_type=F32)


def _dot_nt(a, b):
    return lax.dot_general(a, b, (((1,), (1,)), ((), ())), preferred_element_type=F32)


def _inproj_kernel(x_ref, g_ref, w_ref, cos_ref, slo_ref, shi_ref, qg_ref, ksg_ref, kwg_ref,
                   qn_ref, kvc_ref, ks_ref, vst_ref, kw_ref, vwt_ref, gt_ref, pool_ref, *, nS):
    x = x_ref[...]
    y = x * lax.rsqrt(jnp.mean(x * x, axis=-1, keepdims=True) + RMS_EPS)
    h = (y * g_ref[...]).astype(BF16)
    cos, slo, shi = cos_ref[...], slo_ref[...], shi_ref[...]
    G = NSA_KV_HEADS
    head = lambda z, c: z[:, c * HEAD_DIM:(c + 1) * HEAD_DIM]

    scale = HEAD_DIM ** -0.5 * LOG2_E
    zq = _dot(h, w_ref[:, 0:NSA_WIDTH])
    for c in range(NSA_HEADS):
        qn_ref[:, c * HEAD_DIM:(c + 1) * HEAD_DIM] = (
            _norm_rope(head(zq, c), qg_ref[...], cos, slo, shi) * scale).astype(BF16)

    pair = lambda p: _dot(h, w_ref[:, NSA_WIDTH + p * G * HEAD_DIM:NSA_WIDTH + (p + 1) * G * HEAD_DIM])
    zkc, zvc = pair(0), pair(1)
    for g in range(G):
        kvc_ref[g] = head(zkc, g)
        kvc_ref[G + g] = head(zvc, g)
    zks, zvs, zkw, zvw = pair(2), pair(3), pair(4), pair(5)
    row = lax.rem(pl.program_id(0), nS) * TM_IN + lax.broadcasted_iota(I32, (TM_IN, N_BLK_PAD), 0)
    lane = lax.broadcasted_iota(I32, (TM_IN, N_BLK_PAD), 1)
    onehot = jnp.where((row >> SLC_SHIFT) == lane, 1.0, 0.0).astype(BF16)
    for g in range(G):
        ks_ref[g, :, 0:HEAD_DIM] = _norm_rope(head(zks, g), ksg_ref[...], cos, slo, shi).astype(BF16)
        ks_ref[g, :, HEAD_DIM:HEAD_DIM + N_BLK_PAD] = onehot
        vst_ref[g, 0] = head(zvs, g).T.astype(BF16)
        kw_ref[g] = _norm_rope(head(zkw, g), kwg_ref[...], cos, slo, shi).astype(BF16)
        vwt_ref[g, 0] = head(zvw, g).T.astype(BF16)

    base = NSA_WIDTH + N_KV_COLS * HEAD_DIM
    pool_ref[...] = _dot(h, w_ref[:, base:base + POOL_WIDTH])
    sig_t = jax.nn.sigmoid(_dot(h, w_ref[:, base + POOL_WIDTH:base + POOL_WIDTH + LANES])).T
    for g in range(G):
        gt_ref[g] = sig_t[g * GATE_PAD:(g + 1) * GATE_PAD]


def _inproj(x2, g1, w_perm, cos, slo, shi, qg, ksg, kwg, B, S):
    T = x2.shape[0]
    n_cols = w_perm.shape[1]
    nS = S // TM_IN
    G = NSA_KV_HEADS
    full = lambda shape: pl.BlockSpec(shape, lambda i: (0,) * len(shape))
    tab = pl.BlockSpec((TM_IN, LANES), lambda i: (lax.rem(i, nS), 0))
    bi = lambda i: (i // nS, lax.rem(i, nS))
    return pl.pallas_call(
        functools.partial(_inproj_kernel, nS=nS),
        grid=(T // TM_IN,),
        in_specs=[
            pl.BlockSpec((TM_IN, D_MODEL), lambda i: (i, 0)),
            full((1, D_MODEL)),
            full((D_MODEL, n_cols)),
            tab, tab, tab,
            full((1, HEAD_DIM)), full((1, HEAD_DIM)), full((1, HEAD_DIM)),
        ],
        out_specs=[
            pl.BlockSpec((TM_IN, NSA_WIDTH), lambda i: (i, 0)),
            pl.BlockSpec((2 * G, TM_IN, LANES), lambda i: (0, i, 0)),
            pl.BlockSpec((G, TM_IN, 2 * HEAD_DIM), lambda i: (*bi(i), 0)),
            pl.BlockSpec((G, 1, HEAD_DIM, TM_IN), lambda i: (*bi(i), 0, 0)),
            pl.BlockSpec((G, TM_IN, HEAD_DIM), lambda i: (*bi(i), 0)),
            pl.BlockSpec((G, 1, HEAD_DIM, TM_IN), lambda i: (*bi(i), 0, 0)),
            pl.BlockSpec((G, GATE_PAD, TM_IN), lambda i: (i // nS, 0, lax.rem(i, nS))),
            pl.BlockSpec((TM_IN, POOL_WIDTH), lambda i: (i, 0)),
        ],
        out_shape=[
            jax.ShapeDtypeStruct((T, NSA_WIDTH), BF16),
            jax.ShapeDtypeStruct((2 * G, T, LANES), F32),
            jax.ShapeDtypeStruct((B * G, S, 2 * HEAD_DIM), BF16),
            jax.ShapeDtypeStruct((B * G, nS, HEAD_DIM, TM_IN), BF16),
            jax.ShapeDtypeStruct((B * G, S, HEAD_DIM), BF16),
            jax.ShapeDtypeStruct((B * G, nS, HEAD_DIM, TM_IN), BF16),
            jax.ShapeDtypeStruct((B * G, GATE_PAD, S), F32),
            jax.ShapeDtypeStruct((T, POOL_WIDTH), F32),
        ],
        compiler_params=_cparams(1),
        name="inproj",
    )(x2, g1, w_perm, cos, slo, shi, qg, ksg, kwg)


def _norm_rope(x, g, cos, sin_lo, sin_hi):
    half = ROT_DIM // 2
    y = x * lax.rsqrt(jnp.mean(x * x, axis=-1, keepdims=True) + QK_EPS) * g
    return (y * cos + pltpu.roll(y, LANES - half, 1) * sin_lo
            + pltpu.roll(y, half, 1) * sin_hi)


def _compress_body(x_ref, pos_ref, w1_ref, b1_ref, w2_ref, b2_ref):
    x = x_ref[0, 0]
    nc = x.shape[0]
    xa = (x + pos_ref[0:1, :]).astype(BF16)
    xb = (x + pos_ref[1:2, :]).astype(BF16)
    first = _dot(xa, w1_ref[0])
    second = _dot(xb, w1_ref[1])
    nxt = pltpu.roll(second, nc - 1, 0)
    hid = jax.nn.gelu(first + nxt + b1_ref[...])
    return _dot(hid.astype(BF16), w2_ref[...]) + b2_ref[...]


def _compress_k_kernel(x_ref, pos_ref, w1_ref, b1_ref, w2_ref, b2_ref,
                       g_ref, cos_ref, slo_ref, shi_ref, o_ref):
    out = _compress_body(x_ref, pos_ref, w1_ref, b1_ref, w2_ref, b2_ref)
    o_ref[0] = _norm_rope(out, g_ref[...], cos_ref[...], slo_ref[...], shi_ref[...]).astype(BF16)


def _compress_v_kernel(x_ref, pos_ref, w1_ref, b1_ref, w2_ref, b2_ref, o_ref):
    out = _compress_body(x_ref, pos_ref, w1_ref, b1_ref, w2_ref, b2_ref)
    o_ref[0] = out.T.astype(BF16)


def _compress(kvc4, which, pos2, w1, b1, w2, b2, B, S, rope=None):
    NC = S // CMP_STRIDE
    G = NSA_KV_HEADS
    CW = CMP_STRIDE * HEAD_DIM
    common_specs = [
        pl.BlockSpec((1, 1, NC, CW), lambda b, g: (which * G + g, b, 0, 0)),
        pl.BlockSpec((2, CW), lambda b, g: (0, 0)),
        pl.BlockSpec((2, CW, CMP_HIDDEN), lambda b, g: (0, 0, 0)),
        pl.BlockSpec((1, CMP_HIDDEN), lambda b, g: (0, 0)),
        pl.BlockSpec((CMP_HIDDEN, HEAD_DIM), lambda b, g: (0, 0)),
        pl.BlockSpec((1, HEAD_DIM), lambda b, g: (0, 0)),
    ]
    if rope is not None:
        gk, cos, slo, shi = rope
        tab = pl.BlockSpec((NC, LANES), lambda b, g: (0, 0))
        return pl.pallas_call(
            _compress_k_kernel,
            grid=(B, G),
            in_specs=common_specs + [pl.BlockSpec((1, HEAD_DIM), lambda b, g: (0, 0)), tab, tab, tab],
            out_specs=pl.BlockSpec((1, NC, HEAD_DIM), lambda b, g: (b * G + g, 0, 0)),
            out_shape=jax.ShapeDtypeStruct((B * G, NC, HEAD_DIM), BF16),
            compiler_params=_cparams(2),
            name="compress_k",
        )(kvc4, pos2, w1, b1, w2, b2, gk, cos, slo, shi)
    return pl.pallas_call(
        _compress_v_kernel,
        grid=(B, G),
        in_specs=common_specs,
        out_specs=pl.BlockSpec((1, HEAD_DIM, NC), lambda b, g: (b * G + g, 0, 0)),
        out_shape=jax.ShapeDtypeStruct((B * G, HEAD_DIM, NC), BF16),
        compiler_params=_cparams(2),
        name="compress_v",
    )(kvc4, pos2, w1, b1, w2, b2)


BIAS_DIAG, BIAS_FAR, BIAS_ALL, BIAS_NONE = 0, 1, 2, 3


def _tile_bias_table():
    k = lax.broadcasted_iota(I32, (TK, TQ), 0)
    t = lax.broadcasted_iota(I32, (TK, TQ), 1)
    neg = jnp.full((TK, TQ), MASK_VALUE, F32)
    zero = jnp.zeros((TK, TQ), F32)
    return jnp.stack([jnp.where(k <= t, zero, neg), jnp.where(k > t, zero, neg), neg, zero])


def _add_tile_bias(s, b):
    return jnp.concatenate([s[:, r * TQ:(r + 1) * TQ] + b for r in range(HEADS_PER_GROUP)], axis=1)


def _nsa_kernel(q_ref, kc_ref, vct_ref, ks_ref, vst_ref, kw_ref, vwt_ref, gt_ref, bias_ref, o_ref,
                qaug_ref, s_ref, cm_ref, m_ref, l_ref, acc_ref, oc_ref, ow_ref, *, S):
    R = HEADS_PER_GROUP
    NQ = R * TQ
    NC = S // CMP_STRIDE
    qi = pl.program_id(2)
    qs = qi * TQ

    q = q_ref[...].astype(F32)
    q_t = jnp.concatenate([q[:, r * HEAD_DIM:(r + 1) * HEAD_DIM].T for r in range(R)], axis=1).astype(BF16)
    col = lax.broadcasted_iota(I32, (1, NQ), 1)
    t_row = qs + (col & (TQ - 1))

    s_c = _dot(kc_ref[0], q_t)
    c_end = lax.broadcasted_iota(I32, (NC, 1), 0) * CMP_STRIDE + (CMP_BLK - 1)
    s_c = jnp.where(c_end <= t_row, s_c, MASK_VALUE)
    e_c = jnp.exp2(s_c - jnp.max(s_c, axis=0, keepdims=True))
    l_c = jnp.sum(e_c, axis=0, keepdims=True)
    p_c = e_c * jnp.where(t_row >= CMP_BLK - 1, 1.0 / l_c, 0.0)
    oc_ref[...] = _dot(vct_ref[0], p_c.astype(BF16))

    imp = p_c[:, 0:TQ]
    for r in range(1, R):
        imp = imp + p_c[:, r * TQ:(r + 1) * TQ]
    jj = lax.broadcasted_iota(I32, (N_BLK_PAD, NC), 0) * (SLC_BLK // CMP_STRIDE)
    nn = lax.broadcasted_iota(I32, (N_BLK_PAD, NC), 1)
    per = SLC_BLK // CMP_STRIDE
    fold = (jnp.where((nn >= jj) & (nn < jj + per), 1.0, 0.0)
            + jnp.where((nn >= jj - 1) & (nn < jj + per - 1), 1.0, 0.0)).astype(BF16)
    imp_hi = imp.astype(BF16)
    imp_lo = (imp - imp_hi.astype(F32)).astype(BF16)
    blk_score = _dot(fold, imp_hi) + _dot(fold, imp_lo)

    jb = lax.broadcasted_iota(I32, (N_BLK_PAD, TQ), 0)
    cur = (qs + lax.broadcasted_iota(I32, (N_BLK_PAD, TQ), 1)) >> SLC_SHIFT
    forced = (jb == 0) | (jb == cur) | (jb == cur - 1)
    val = jnp.where(forced, FORCE_VALUE, jnp.where(jb <= cur, blk_score, MASK_VALUE))
    jbf = jb.astype(F32)
    bias = jnp.where(forced, 0.0, MASK_VALUE)
    val = jnp.where(forced, -jnp.inf, val)
    for _ in range(min(N_SEL, S // SLC_BLK) - 3):
        mx = jnp.max(val, axis=0, keepdims=True)
        first = jnp.min(jnp.where(val == mx, jbf, float(N_BLK_PAD)), axis=0, keepdims=True)
        pick = jbf == first
        bias = jnp.where(pick, 0.0, bias)
        val = jnp.where(pick, -jnp.inf, val)
    qaug_ref[0:HEAD_DIM, :] = q_t
    qaug_ref[HEAD_DIM:HEAD_DIM + N_BLK_PAD, :] = jnp.concatenate([bias.astype(BF16)] * R, axis=1)

    w_tiles = ((jnp.maximum(qi - 2, 0), jnp.where(qi >= 2, BIAS_FAR, BIAS_ALL)),
               (jnp.maximum(qi - 1, 0), jnp.where(qi >= 1, BIAS_NONE, BIAS_ALL)),
               (qi, BIAS_DIAG))
    s_w = []
    for kt, bi in w_tiles:
        kwt = kw_ref[0, pl.ds(pl.multiple_of(kt * TK, TK), TK), :]
        s_w.append(_add_tile_bias(_dot(kwt, q_t), bias_ref[bi]))
    m_w = jnp.max(s_w[0], axis=0, keepdims=True)
    for s in s_w[1:]:
        m_w = jnp.maximum(m_w, jnp.max(s, axis=0, keepdims=True))
    l_w = jnp.zeros((1, NQ), F32)
    o_w = jnp.zeros((HEAD_DIM, NQ), F32)
    for (kt, _), s in zip(w_tiles, s_w):
        e = jnp.exp2(s - m_w)
        l_w = l_w + jnp.sum(e, axis=0, keepdims=True)
        o_w = o_w + _dot(vwt_ref[0, kt], e.astype(BF16))
    ow_ref[...] = o_w * (1.0 / l_w)

    m_ref[...] = jnp.full((1, NQ), MASK_VALUE, F32)
    l_ref[...] = jnp.zeros((1, NQ), F32)
    acc_ref[...] = jnp.zeros((HEAD_DIM, NQ), F32)

    def produce(kj, slot, causal):
        k = ks_ref[0, pl.ds(pl.multiple_of(kj * TK, TK), TK), :]
        s = _dot(k, qaug_ref[...])
        if causal:
            s = _add_tile_bias(s, bias_ref[BIAS_DIAG])
        s_ref[slot] = s
        cm_ref[slot] = jnp.max(s, axis=0, keepdims=True)

    def consume(kj, slot):
        m_old = m_ref[...]
        m_new = jnp.maximum(m_old, cm_ref[slot])
        alpha = jnp.exp2(m_old - m_new)
        p = jnp.exp2(s_ref[slot] - m_new)
        l_ref[...] = alpha * l_ref[...] + jnp.sum(p, axis=0, keepdims=True)
        acc_ref[...] = alpha * acc_ref[...] + _dot(vst_ref[0, kj], p.astype(BF16))
        m_ref[...] = m_new

    def stage(kj, slot, causal_next):
        produce(kj + 1, 1 - slot, causal_next)
        consume(kj, slot)

    @pl.when(qi == 0)
    def _():
        produce(0, 0, True)

    @pl.when(qi > 0)
    def _():
        produce(0, 0, False)

    def stage_group(pp, carry):
        for u in range(SEL_UNROLL):
            stage(SEL_UNROLL * pp + u, u & 1, False)
        return carry

    n_plain = jnp.maximum(qi - 1, 0)
    n_grouped = n_plain // SEL_UNROLL * SEL_UNROLL
    lax.fori_loop(0, n_plain // SEL_UNROLL, stage_group, 0)
    for u in range(SEL_UNROLL - 1):
        @pl.when(n_plain - n_grouped > u)
        def _(u=u):
            stage(n_grouped + u, u & 1, False)

    @pl.when((qi >= 1) & ((n_plain & 1) == 0))
    def _():
        stage(qi - 1, 0, True)
        consume(qi, 1)

    @pl.when((qi >= 1) & ((n_plain & 1) == 1))
    def _():
        stage(qi - 1, 1, True)
        consume(qi, 0)

    @pl.when(qi == 0)
    def _():
        consume(0, 0)

    inv_l = 1.0 / l_ref[...]

    for r in range(R):
        sl = slice(r * TQ, (r + 1) * TQ)
        g0 = gt_ref[0, r * N_BRANCH + 0:r * N_BRANCH + 1, :]
        g1 = gt_ref[0, r * N_BRANCH + 1:r * N_BRANCH + 2, :]
        g2 = gt_ref[0, r * N_BRANCH + 2:r * N_BRANCH + 3, :]
        o = (g0 * oc_ref[:, sl] + (g1 * inv_l[:, sl]) * acc_ref[:, sl]
             + g2 * ow_ref[:, sl])
        o_ref[:, r * HEAD_DIM:(r + 1) * HEAD_DIM] = o.T.astype(BF16)


def _nsa(qn, kc, vct, ks, vst, kw, vwt, gt, B, S):
    G = NSA_KV_HEADS
    R = HEADS_PER_GROUP
    nQ = S // TQ
    nK = S // TK
    NC = S // CMP_STRIDE
    bg = lambda b, g, i: b * G + g
    return pl.pallas_call(
        functools.partial(_nsa_kernel, S=S),
        grid=(B, G, nQ),
        in_specs=[
            pl.BlockSpec((TQ, R * HEAD_DIM), lambda b, g, i: (b * nQ + i, g)),
            pl.BlockSpec((1, NC, HEAD_DIM), lambda b, g, i: (bg(b, g, i), 0, 0)),
            pl.BlockSpec((1, HEAD_DIM, NC), lambda b, g, i: (bg(b, g, i), 0, 0)),
            pl.BlockSpec((1, S, 2 * HEAD_DIM), lambda b, g, i: (bg(b, g, i), 0, 0)),
            pl.BlockSpec((1, nK, HEAD_DIM, TK), lambda b, g, i: (bg(b, g, i), 0, 0, 0)),
            pl.BlockSpec((1, S, HEAD_DIM), lambda b, g, i: (bg(b, g, i), 0, 0)),
            pl.BlockSpec((1, nK, HEAD_DIM, TK), lambda b, g, i: (bg(b, g, i), 0, 0, 0)),
            pl.BlockSpec((1, GATE_PAD, TQ), lambda b, g, i: (bg(b, g, i), 0, i)),
            pl.BlockSpec((4, TK, TQ), lambda b, g, i: (0, 0, 0)),
        ],
        out_specs=pl.BlockSpec((TQ, R * HEAD_DIM), lambda b, g, i: (b * nQ + i, g)),
        out_shape=jax.ShapeDtypeStruct((B * S, NSA_WIDTH), BF16),
        scratch_shapes=[
            pltpu.VMEM((2 * HEAD_DIM, R * TQ), BF16),
            pltpu.VMEM((2, TK, R * TQ), F32),
            pltpu.VMEM((2, 1, R * TQ), F32),
            pltpu.VMEM((1, R * TQ), F32),
            pltpu.VMEM((1, R * TQ), F32),
            pltpu.VMEM((HEAD_DIM, R * TQ), F32),
            pltpu.VMEM((HEAD_DIM, R * TQ), F32),
            pltpu.VMEM((HEAD_DIM, R * TQ), F32),
        ],
        compiler_params=_cparams(3),
        name="nsa",
    )(qn, kc, vct, ks, vst, kw, vwt, gt, _tile_bias_table())


def _mix_kernel(nsa_ref, pool_ref, halo_ref, x_ref, wo_ref, pw_ref, ps_ref, g2_ref,
                wr2_ref, br_ref, x1_ref, h2p_ref, logit_ref, ext_ref, *, S):
    nS = S // TM_MIX
    i = pl.program_id(0)
    t0 = lax.rem(i, nS) * TM_MIX
    ext_ref[0:POOL_HALO, :] = jnp.where(t0 == 0, 0.0, halo_ref[...])
    ext_ref[POOL_HALO:POOL_HALO + TM_MIX, :] = pool_ref[...]
    for sb in range(TM_MIX // MIX_SUB):
        r0 = sb * MIX_SUB
        rows = slice(r0, r0 + MIX_SUB)
        t = t0 + r0 + lax.broadcasted_iota(I32, (MIX_SUB, POOL_GC), 0)
        acc = _dot(nsa_ref[rows, :], wo_ref[0:NSA_WIDTH, :])
        for gi, w in enumerate(POOL_WINDOWS):
            cs = slice(gi * POOL_GC, (gi + 1) * POOL_GC)
            v = pool_ref[rows, cs]
            tot = v
            for k in range(1, w):
                tot = tot + ext_ref[POOL_HALO + r0 - k:POOL_HALO + r0 - k + MIX_SUB, cs]
            cnt = jnp.minimum(t + 1, w).astype(F32)
            d = tot / cnt - v
            y = _dot(d.astype(BF16), pw_ref[gi]) * ps_ref[:, cs]
            acc = acc + _dot(y.astype(BF16),
                             wo_ref[NSA_WIDTH + gi * POOL_GC:NSA_WIDTH + (gi + 1) * POOL_GC, :])
        x1 = x_ref[rows, :] + acc
        x1_ref[rows, :] = x1
        h2 = x1 * lax.rsqrt(jnp.mean(x1 * x1, axis=-1, keepdims=True) + RMS_EPS) * g2_ref[...]
        half = D_MODEL // 2
        h2p_ref[rows, :] = pltpu.pack_elementwise([h2[:, 0:half], h2[:, half:D_MODEL]], packed_dtype=BF16)
        hi = h2.astype(BF16)
        lo = (h2 - hi.astype(F32)).astype(BF16)
        hl = _dot(hi, wr2_ref[...])
        logit_ref[rows, :] = (hl[:, 0:LANES] + hl[:, LANES:2 * LANES] + _dot(lo, wr2_ref[:, 0:LANES])
                              + br_ref[...])


def _mix(nsa_out, pool_in, x2, wo, pw, ps, g2, wr2, br, S):
    T = x2.shape[0]
    hb = TM_MIX // POOL_HALO
    full = lambda shape: pl.BlockSpec(shape, lambda i: (0,) * len(shape))
    return pl.pallas_call(
        functools.partial(_mix_kernel, S=S),
        grid=(T // TM_MIX,),
        in_specs=[
            pl.BlockSpec((TM_MIX, NSA_WIDTH), lambda i: (i, 0)),
            pl.BlockSpec((TM_MIX, POOL_WIDTH), lambda i: (i, 0)),
            pl.BlockSpec((POOL_HALO, POOL_WIDTH), lambda i: (jnp.maximum(i * hb - 1, 0), 0)),
            pl.BlockSpec((TM_MIX, D_MODEL), lambda i: (i, 0)),
            full((D_MODEL, D_MODEL)),
            full((len(POOL_WINDOWS), POOL_GC, POOL_GC)),
            full((1, POOL_WIDTH)),
            full((1, D_MODEL)),
            full((D_MODEL, 2 * LANES)),
            full((1, LANES)),
        ],
        out_specs=[
            pl.BlockSpec((TM_MIX, D_MODEL), lambda i: (i, 0)),
            pl.BlockSpec((TM_MIX, D_MODEL // 2), lambda i: (i, 0)),
            pl.BlockSpec((TM_MIX, LANES), lambda i: (i, 0)),
        ],
        out_shape=[
            jax.ShapeDtypeStruct((T, D_MODEL), F32),
            jax.ShapeDtypeStruct((T, D_MODEL // 2), U32),
            jax.ShapeDtypeStruct((T, LANES), F32),
        ],
        scratch_shapes=[pltpu.VMEM((POOL_HALO + TM_MIX, POOL_WIDTH), F32)],
        compiler_params=_cparams(1),
        name="mix",
    )(nsa_out, pool_in, pool_in, x2, wo, pw, ps, g2, wr2, br)


def _route_kernel(logit_ref, dest_ref, wt_ref, meta_ref, cnt_ref, run_ref, *, n_tiles):
    phase = pl.program_id(0)
    i = pl.program_id(1)
    TT = TT_ROUTE
    E = N_EXPERTS

    @pl.when((phase == 0) & (i == 0))
    def _():
        cnt_ref[...] = jnp.zeros_like(cnt_ref)
        run_ref[...] = jnp.zeros_like(run_ref)

    lt = logit_ref[...].T[0:E, :]
    ef = lax.broadcasted_iota(I32, (E, TT), 0).astype(F32)
    work = lt
    ids, vals, hots = [], [], []
    for _ in range(TOP_K):
        mx = jnp.max(work, axis=0, keepdims=True)
        eid = jnp.min(jnp.where(work == mx, ef, float(E)), axis=0, keepdims=True)
        hot = ef == eid
        ids.append(eid)
        vals.append(mx)
        hots.append(hot)
        work = jnp.where(hot, -jnp.inf, work)
    member = jnp.where(hots[0] | hots[1] | hots[2] | hots[3], 1.0, 0.0)
    tile_cnt = jnp.sum(member, axis=1, keepdims=True)

    @pl.when(phase == 0)
    def _():
        cnt_ref[...] = cnt_ref[...] + tile_cnt

    @pl.when(phase == 1)
    def _():
        ex = [jnp.exp(v - vals[0]) for v in vals]
        den = ex[0] + ex[1] + ex[2] + ex[3]
        cnt = jnp.broadcast_to(cnt_ref[...], (E, LANES))
        padded = jnp.ceil(cnt * (1.0 / TM_G)) * TM_G
        erow = lax.broadcasted_iota(I32, (E, LANES), 0)
        start = jnp.zeros((E, LANES), F32)
        running = jnp.zeros((1, LANES), F32)
        for e in range(E):
            start = jnp.where(erow == e, running, start)
            running = running + padded[e:e + 1, :]
        end = start + padded
        tri = jnp.where(lax.broadcasted_iota(I32, (TT, TT), 0) < lax.broadcasted_iota(I32, (TT, TT), 1),
                        1.0, 0.0).astype(BF16)
        before = _dot(member.astype(BF16), tri) + run_ref[...]
        slot = before + start[:, 0:1]
        for k in range(TOP_K):
            d = jnp.sum(jnp.where(hots[k], slot, 0.0), axis=0, keepdims=True)
            dest_ref[k:k + 1, :] = d.astype(I32)
            wt_ref[k:k + 1, :] = ex[k] / den
        run_ref[...] = run_ref[...] + tile_cnt

        @pl.when(i == 0)
        def _():
            lanes = meta_ref.shape[1]
            endw = jnp.broadcast_to(end[:, 0:1], (E, lanes))
            tile_row = (lax.broadcasted_iota(I32, (E, lanes), 1) * TM_G).astype(F32)
            owner = jnp.minimum(jnp.sum(jnp.where(endw <= tile_row, 1.0, 0.0), axis=0, keepdims=True),
                                float(E - 1))
            nxt = jnp.full((E, LANES), -1.0, F32)
            later = jnp.full((1, LANES), -1.0, F32)
            for e in reversed(range(E)):
                nxt = jnp.where(erow == e, later, nxt)
                later = jnp.where(cnt[e:e + 1, :] > 0.0, float(e), later)
            erow_w = lax.broadcasted_iota(I32, (E, lanes), 0).astype(F32)
            nxt_tile = jnp.sum(jnp.where(erow_w == owner, jnp.broadcast_to(nxt[:, 0:1], (E, lanes)), 0.0),
                               axis=0, keepdims=True)
            meta_ref[0:1, :] = owner.astype(I32)
            meta_ref[1:2, :] = jnp.where(tile_row[0:1, :] < running[:, 0:1], 1, 0).astype(I32)
            meta_ref[2:3, :] = nxt_tile.astype(I32)
            lane_w = lax.broadcasted_iota(I32, (E, lanes), 1).astype(F32)
            meta_ref[3:4, :] = jnp.sum(jnp.where(erow_w == lane_w, endw, 0.0), axis=0,
                                       keepdims=True).astype(I32)
            meta_ref[4:8, :] = jnp.zeros((4, lanes), I32)


def _route(logits, n_tiles):
    T = logits.shape[0]
    nT = T // TT_ROUTE
    lanes = -(-n_tiles // LANES) * LANES
    return pl.pallas_call(
        functools.partial(_route_kernel, n_tiles=n_tiles),
        grid=(2, nT),
        in_specs=[pl.BlockSpec((TT_ROUTE, LANES), lambda p, i: (i, 0))],
        out_specs=[
            pl.BlockSpec((TOP_K, TT_ROUTE), lambda p, i: (0, i * p)),
            pl.BlockSpec((TOP_K, TT_ROUTE), lambda p, i: (0, i * p)),
            pl.BlockSpec((8, lanes), lambda p, i: (0, 0)),
        ],
        out_shape=[
            jax.ShapeDtypeStruct((TOP_K, T), I32),
            jax.ShapeDtypeStruct((TOP_K, T), F32),
            jax.ShapeDtypeStruct((8, lanes), I32),
        ],
        scratch_shapes=[pltpu.VMEM((N_EXPERTS, 1), F32), pltpu.VMEM((N_EXPERTS, 1), F32)],
        compiler_params=_cparams(2),
        name="route",
    )(logits)


def _dispatch_kernel(ends_ref, dest_ref, h_ref, xs_ref, zero_ref, sem, zsem):
    @pl.when(pl.program_id(0) == 0)
    def _():
        zero_ref[...] = jnp.zeros_like(zero_ref)

        def tail_copy(e):
            first = pl.multiple_of(ends_ref[e] - TM_G, TM_G)
            return pltpu.make_async_copy(zero_ref, xs_ref.at[pl.ds(first, TM_G), :], zsem)

        def nonempty(e):
            return ends_ref[e] > (ends_ref[e - 1] if e else 0)

        for e in range(N_EXPERTS):
            @pl.when(nonempty(e))
            def _(e=e):
                tail_copy(e).start()
        for e in range(N_EXPERTS):
            @pl.when(nonempty(e))
            def _(e=e):
                tail_copy(e).wait()

        def unused_copy(tile):
            first = pl.multiple_of(tile * TM_G, TM_G)
            return pltpu.make_async_copy(zero_ref, xs_ref.at[pl.ds(first, TM_G), :], zsem)

        n_used = lax.div(ends_ref[N_EXPERTS - 1], TM_G)
        n_all = xs_ref.shape[0] // TM_G

        def start_unused(tile, carry):
            unused_copy(tile).start()
            return carry

        def wait_unused(tile, carry):
            unused_copy(tile).wait()
            return carry

        lax.fori_loop(n_used, n_all, start_unused, 0)
        lax.fori_loop(n_used, n_all, wait_unused, 0)

    def row_copy(t, k):
        return pltpu.make_async_copy(h_ref.at[pl.ds(t, 1), :],
                                     xs_ref.at[pl.ds(dest_ref[k, t], 1), :], sem)

    def issue(t, carry):
        for k in range(TOP_K):
            row_copy(t, k).start()
        return carry

    lax.fori_loop(0, TT_DISP, issue, 0)
    all_rows = xs_ref.at[pl.ds(0, TOP_K * TT_DISP), :]
    pltpu.make_async_copy(all_rows, all_rows, sem).wait()


def _dispatch(ends, dest, h2p, n_rows):
    T, W = h2p.shape
    grid_spec = pltpu.PrefetchScalarGridSpec(
        num_scalar_prefetch=1,
        grid=(T // TT_DISP,),
        in_specs=[
            pl.BlockSpec((TOP_K, TT_DISP), lambda i, ends: (0, i), memory_space=pltpu.SMEM),
            pl.BlockSpec((TT_DISP, W), lambda i, ends: (i, 0)),
        ],
        out_specs=pl.BlockSpec(memory_space=pl.ANY),
        scratch_shapes=[pltpu.VMEM((TM_G, W), U32), pltpu.SemaphoreType.DMA, pltpu.SemaphoreType.DMA],
    )
    return pl.pallas_call(
        _dispatch_kernel,
        grid_spec=grid_spec,
        out_shape=jax.ShapeDtypeStruct((n_rows, W), U32),
        compiler_params=_cparams(1),
        name="dispatch",
    )(ends, dest, h2p)


def _expert_changed(te_ref, i):
    return (i == 0) | (te_ref[i] != te_ref[jnp.maximum(i - 1, 0)])


def _stream_expert_weights(te_ref, tv_ref, nx_ref, copies, wst_ref, wbf_ref, slot_ref):
    j = pl.program_id(0)
    i = pl.program_id(1)

    @pl.when((j == 0) & (i == 0))
    def _():
        slot_ref[0] = 0
        for c in copies(te_ref[0], 0, 0):
            c.start()

    @pl.when((tv_ref[i] > 0) & _expert_changed(te_ref, i))
    def _():
        slot = slot_ref[0]
        e_next = nx_ref[i]

        @pl.when(e_next >= 0)
        def _():
            for c in copies(e_next, j, 1 - slot):
                c.start()

        @pl.when((e_next < 0) & (j + 1 < pl.num_programs(0)))
        def _():
            for c in copies(te_ref[0], j + 1, 1 - slot):
                c.start()

        for c in copies(te_ref[i], j, slot):
            c.wait()
        wbf_ref[...] = wst_ref[slot].astype(BF16)
        slot_ref[0] = 1 - slot


def _gemm1_kernel(te_ref, tv_ref, nx_ref, x_ref, bg_ref, bu_ref, w_hbm, act_ref,
                  wst_ref, wbf_ref, slot_ref, sem):
    i = pl.program_id(1)
    valid = tv_ref[i] > 0

    def copies(e, j, slot):
        col = pl.multiple_of(j * TN_G1, TN_G1)
        return (pltpu.make_async_copy(w_hbm.at[e, :, pl.ds(col, TN_G1)],
                                      wst_ref.at[slot, :, pl.ds(0, TN_G1)], sem.at[slot]),
                pltpu.make_async_copy(w_hbm.at[e, :, pl.ds(D_FF + col, TN_G1)],
                                      wst_ref.at[slot, :, pl.ds(TN_G1, TN_G1)], sem.at[slot]))

    _stream_expert_weights(te_ref, tv_ref, nx_ref, copies, wst_ref, wbf_ref, slot_ref)

    @pl.when(valid)
    def _():
        half = D_MODEL // 2
        xp = x_ref[...]
        lo = pltpu.unpack_elementwise(xp, index=0, packed_dtype=BF16, unpacked_dtype=F32).astype(BF16)
        hi = pltpu.unpack_elementwise(xp, index=1, packed_dtype=BF16, unpacked_dtype=F32).astype(BF16)
        gu = _dot(lo, wbf_ref[0:half, :]) + _dot(hi, wbf_ref[half:D_MODEL, :])
        gate = jnp.minimum(gu[:, 0:TN_G1] + bg_ref[0], SWIGLU_LIMIT)
        up = jnp.clip(gu[:, TN_G1:2 * TN_G1] + bu_ref[0], -SWIGLU_LIMIT, SWIGLU_LIMIT)
        act = (up + 1.0) * gate * jax.nn.sigmoid(SWIGLU_ALPHA * gate)
        act_ref[...] = act.astype(BF16)

    @pl.when(jnp.logical_not(valid))
    def _():
        act_ref[...] = jnp.zeros_like(act_ref)


def _gemm1(te, tv, nx, xs, w_gu, b_gu3, n_tiles):
    n_rows = xs.shape[0]
    nJ = D_FF // TN_G1
    grid_spec = pltpu.PrefetchScalarGridSpec(
        num_scalar_prefetch=3,
        grid=(nJ, n_tiles),
        in_specs=[
            pl.BlockSpec((TM_G, D_MODEL // 2), lambda j, i, te, tv, nx: (i, 0)),
            pl.BlockSpec((1, 1, TN_G1), lambda j, i, te, tv, nx: (te[i], 0, j)),
            pl.BlockSpec((1, 1, TN_G1), lambda j, i, te, tv, nx: (te[i], 0, nJ + j)),
            pl.BlockSpec(memory_space=pl.ANY),
        ],
        out_specs=pl.BlockSpec((TM_G, TN_G1), lambda j, i, te, tv, nx: (i, j)),
        scratch_shapes=[
            pltpu.VMEM((2, D_MODEL, 2 * TN_G1), F32),
            pltpu.VMEM((D_MODEL, 2 * TN_G1), BF16),
            pltpu.SMEM((1,), I32),
            pltpu.SemaphoreType.DMA((2,)),
        ],
    )
    return pl.pallas_call(
        _gemm1_kernel,
        grid_spec=grid_spec,
        out_shape=jax.ShapeDtypeStruct((n_rows, D_FF), BF16),
        compiler_params=_cparams(2),
        name="gemm1",
    )(te, tv, nx, xs, b_gu3, b_gu3, w_gu)


def _gemm2_kernel(te_ref, tv_ref, nx_ref, a_ref, b_ref, w_hbm, y_ref, wst_ref, wbf_ref, slot_ref, sem):
    i = pl.program_id(1)
    valid = tv_ref[i] > 0

    def copies(e, j, slot):
        col = pl.multiple_of(j * TN_G2, TN_G2)
        return (pltpu.make_async_copy(w_hbm.at[e, :, pl.ds(col, TN_G2)], wst_ref.at[slot], sem.at[slot]),)

    _stream_expert_weights(te_ref, tv_ref, nx_ref, copies, wst_ref, wbf_ref, slot_ref)

    @pl.when(valid)
    def _():
        y_ref[...] = _dot(a_ref[...], wbf_ref[...]) + b_ref[0]

    @pl.when(jnp.logical_not(valid))
    def _():
        y_ref[...] = jnp.zeros_like(y_ref)


def _gemm2(te, tv, nx, act, w_d, b_d3, n_tiles):
    n_rows = act.shape[0]
    nJ = D_MODEL // TN_G2
    grid_spec = pltpu.PrefetchScalarGridSpec(
        num_scalar_prefetch=3,
        grid=(nJ, n_tiles),
        in_specs=[
            pl.BlockSpec((TM_G, D_FF), lambda j, i, te, tv, nx: (i, 0)),
            pl.BlockSpec((1, 1, TN_G2), lambda j, i, te, tv, nx: (te[i], 0, j)),
            pl.BlockSpec(memory_space=pl.ANY),
        ],
        out_specs=pl.BlockSpec((TM_G, TN_G2), lambda j, i, te, tv, nx: (i, j)),
        scratch_shapes=[
            pltpu.VMEM((2, D_FF, TN_G2), F32),
            pltpu.VMEM((D_FF, TN_G2), BF16),
            pltpu.SMEM((1,), I32),
            pltpu.SemaphoreType.DMA((2,)),
        ],
    )
    return pl.pallas_call(
        _gemm2_kernel,
        grid_spec=grid_spec,
        out_shape=jax.ShapeDtypeStruct((n_rows, D_MODEL), F32),
        compiler_params=_cparams(2),
        name="gemm2",
    )(te, tv, nx, act, b_d3, w_d)


def _combine_kernel(dest_ref, dnext_ref, wt_ref, x1_ref, y_ref, o_ref, buf_ref, sem):
    i = pl.program_id(0)
    n = pl.num_programs(0)

    def gather(d_ref, slot):
        def issue(t, carry):
            for k in range(TOP_K):
                pltpu.make_async_copy(y_ref.at[pl.ds(d_ref[k, t], 1), :],
                                      buf_ref.at[slot, k, pl.ds(t, 1), :], sem.at[slot]).start()
            return carry
        lax.fori_loop(0, TT_COMB, issue, 0)

    def finish(slot):
        pltpu.make_async_copy(buf_ref.at[slot], buf_ref.at[slot], sem.at[slot]).wait()
        out = x1_ref[...]
        for k in range(TOP_K):
            out = out + buf_ref[slot, k] * wt_ref[:, k:k + 1]
        o_ref[...] = out

    @pl.when(i == 0)
    def _():
        gather(dest_ref, 0)

    for slot in range(2):
        @pl.when((i & 1) == slot)
        def _(slot=slot):
            @pl.when(i + 1 < n)
            def _():
                gather(dnext_ref, 1 - slot)
            finish(slot)


def _combine(dest, wt_tok, x1, y):
    T = x1.shape[0]
    nT = T // TT_COMB
    return pl.pallas_call(
        _combine_kernel,
        grid=(nT,),
        in_specs=[
            pl.BlockSpec((TOP_K, TT_COMB), lambda i: (0, i), memory_space=pltpu.SMEM),
            pl.BlockSpec((TOP_K, TT_COMB), lambda i: (0, jnp.minimum(i + 1, nT - 1)),
                         memory_space=pltpu.SMEM),
            pl.BlockSpec((TT_COMB, TOP_K), lambda i: (i, 0)),
            pl.BlockSpec((TT_COMB, D_MODEL), lambda i: (i, 0)),
            pl.BlockSpec(memory_space=pl.ANY),
        ],
        out_specs=pl.BlockSpec((TT_COMB, D_MODEL), lambda i: (i, 0)),
        out_shape=jax.ShapeDtypeStruct((T, D_MODEL), F32),
        scratch_shapes=[pltpu.VMEM((2, TOP_K, TT_COMB, D_MODEL), F32), pltpu.SemaphoreType.DMA((2,))],
        compiler_params=_cparams(1),
        name="combine",
    )(dest, dest, wt_tok, x1, y)


def _rope_tables(pos):
    half = ROT_DIM // 2
    inv_freq = ROPE_THETA ** (-jnp.arange(0, ROT_DIM, 2, dtype=F32) / ROT_DIM)
    ang = pos[:, None] * inv_freq[None, :]
    cos, sin = jnp.cos(ang), jnp.sin(ang)
    n = pos.shape[0]
    ones = jnp.ones((n, LANES - ROT_DIM), F32)
    zeros = jnp.zeros((n, LANES - ROT_DIM), F32)
    zh = jnp.zeros((n, half), F32)
    return (jnp.concatenate([cos, cos, ones], axis=1),
            jnp.concatenate([-sin, zh, zeros], axis=1),
            jnp.concatenate([zh, sin, zeros], axis=1))


def _permute_w_in(w_in):
    kv_end = NSA_WIDTH + N_KV_COLS * HEAD_DIM
    n_gate = NSA_HEADS * N_BRANCH
    per_g = HEADS_PER_GROUP * N_BRANCH
    gate = w_in[:, kv_end:kv_end + n_gate]
    pieces = [w_in[:, :kv_end], w_in[:, kv_end + n_gate:]]
    zpad = jnp.zeros((D_MODEL, GATE_PAD - per_g), w_in.dtype)
    for g in range(NSA_KV_HEADS):
        pieces += [gate[:, g * per_g:(g + 1) * per_g], zpad]
    pieces.append(jnp.zeros((D_MODEL, LANES - NSA_KV_HEADS * GATE_PAD), w_in.dtype))
    return jnp.concatenate(pieces, axis=1).astype(BF16)


def _layer(x, norm1_g, w_in, q_norm_g, k_norm_cmp_g, k_norm_slc_g, k_norm_win_g,
           cmp_k_pos, cmp_k_w1, cmp_k_b1, cmp_k_w2, cmp_k_b2,
           cmp_v_pos, cmp_v_w1, cmp_v_b1, cmp_v_w2, cmp_v_b2,
           pool_w, pool_scale, w_out, norm2_g,
           w_router, b_router, w_gate_up, b_gate_up, w_down, b_down):
    B, S, _ = x.shape
    T = B * S
    assert S % TQ == 0 and TQ == TK == TM_IN and WINDOW == 2 * TK and S // SLC_BLK <= N_BLK_PAD
    assert (S // CMP_STRIDE) % LANES == 0 and T % TT_ROUTE == 0
    NC = S // CMP_STRIDE
    x2 = x.reshape(T, D_MODEL)
    row = lambda v: v.reshape(1, -1)

    cos, slo, shi = _rope_tables(jnp.arange(S, dtype=F32))
    qn, kvc, ks, vst, kw, vwt, gt, pool_in = _inproj(
        x2, row(norm1_g), _permute_w_in(w_in), cos, slo, shi,
        row(q_norm_g), row(k_norm_slc_g), row(k_norm_win_g), B, S)

    c_end = (jnp.arange(NC, dtype=I32) * CMP_STRIDE + (CMP_BLK - 1)).astype(F32)
    ccos, cslo, cshi = _rope_tables(c_end)
    kvc4 = kvc.reshape(4, B, NC, CMP_STRIDE * HEAD_DIM)
    cw = CMP_STRIDE * HEAD_DIM
    kc = _compress(kvc4, 0, cmp_k_pos.reshape(2, cw), cmp_k_w1.reshape(2, cw, CMP_HIDDEN).astype(BF16),
                   row(cmp_k_b1), cmp_k_w2.astype(BF16), row(cmp_k_b2), B, S,
                   rope=(row(k_norm_cmp_g), ccos, cslo, cshi))
    vct = _compress(kvc4, 1, cmp_v_pos.reshape(2, cw), cmp_v_w1.reshape(2, cw, CMP_HIDDEN).astype(BF16),
                    row(cmp_v_b1), cmp_v_w2.astype(BF16), row(cmp_v_b2), B, S)

    nsa_out = _nsa(qn, kc, vct, ks, vst, kw, vwt, gt, B, S)

    wr_pad = jnp.pad(w_router, ((0, 0), (0, LANES - N_EXPERTS)))
    wr_hi = wr_pad.astype(BF16)
    wr_lo = (wr_pad - wr_hi.astype(F32)).astype(BF16)
    br_pad = jnp.concatenate([b_router.astype(F32), jnp.full((LANES - N_EXPERTS,), MASK_VALUE, F32)])
    x1, h2p, logits = _mix(nsa_out, pool_in, x2, w_out.astype(BF16), pool_w.astype(BF16),
                           row(pool_scale), row(norm2_g), jnp.concatenate([wr_hi, wr_lo], axis=1),
                           row(br_pad), S)

    n_tiles = T * TOP_K // TM_G + N_EXPERTS
    dest, wts, meta = _route(logits, n_tiles)
    te, tv, nx = meta[0, :n_tiles], meta[1, :n_tiles], meta[2, :n_tiles]
    xs = _dispatch(meta[3, :N_EXPERTS], dest, h2p, n_tiles * TM_G)
    act = _gemm1(te, tv, nx, xs, w_gate_up, b_gate_up.reshape(N_EXPERTS, 1, 2 * D_FF), n_tiles)
    y = _gemm2(te, tv, nx, act, w_down, b_down.reshape(N_EXPERTS, 1, D_MODEL), n_tiles)
    out = _combine(dest, wts.T, x1, y)
    return out.reshape(B, S, D_MODEL)


def kernel(x, norm1_g, w_in, q_norm_g, k_norm_cmp_g, k_norm_slc_g, k_norm_win_g, cmp_k_pos, cmp_k_w1, cmp_k_b1, cmp_k_w2, cmp_k_b2, cmp_v_pos, cmp_v_w1, cmp_v_b1, cmp_v_w2, cmp_v_b2, pool_w, pool_scale, w_out, norm2_g, w_router, b_router, w_gate_up, b_gate_up, w_down, b_down):
    params = (norm1_g, w_in, q_norm_g, k_norm_cmp_g, k_norm_slc_g, k_norm_win_g,
              cmp_k_pos, cmp_k_w1, cmp_k_b1, cmp_k_w2, cmp_k_b2,
              cmp_v_pos, cmp_v_w1, cmp_v_b1, cmp_v_w2, cmp_v_b2,
              pool_w, pool_scale, w_out, norm2_g,
              w_router, b_router, w_gate_up, b_gate_up, w_down, b_down)
    depth = norm1_g.shape[0]
    for l in range(depth):
        x = _layer(x, *[p.reshape(p.shape[1:]) if depth == 1 else p[l] for p in params])
    return x
```

```python
import functools

import jax
import jax.numpy as jnp
from jax import lax
from jax.experimental import pallas as pl
from jax.experimental.pallas import tpu as pltpu

F32 = jnp.float32
BF16 = jnp.bfloat16
I32 = jnp.int32
U32 = jnp.uint32

D_MODEL = 2048
HEAD_DIM = 128
NSA_HEADS = 8
NSA_KV_HEADS = 2
HEADS_PER_GROUP = NSA_HEADS // NSA_KV_HEADS
NSA_WIDTH = NSA_HEADS * HEAD_DIM
N_BRANCH = 3
CMP_BLK = 32
CMP_STRIDE = 16
CMP_HIDDEN = 256
SLC_BLK = 64
SLC_SHIFT = SLC_BLK.bit_length() - 1
N_SEL = 16
WINDOW = 512
ROPE_THETA = 500000.0
ROT_DIM = HEAD_DIM // 4
POOL_WIDTH = 1024
POOL_WINDOWS = (2, 4, 8, 16)
POOL_GC = 256
N_EXPERTS = 32
TOP_K = 4
D_FF = 2048
SWIGLU_ALPHA = 1.702
SWIGLU_LIMIT = 7.0
RMS_EPS = 1e-5
QK_EPS = 1e-6
MASK_VALUE = -1e30
FORCE_VALUE = 1e30
LOG2_E = 1.4426950408889634

LANES = 128
N_KV_COLS = 6 * NSA_KV_HEADS
GATE_PAD = 16
N_BLK_PAD = 128
POOL_HALO = 16

TM_IN = 256
TQ = 256
TK = 256
SEL_UNROLL = 4
TM_MIX = 256
MIX_SUB = 256
TT_ROUTE = 256
TM_G = 256
TN_G1 = 1024
TN_G2 = 2048
XW = D_MODEL // 2 // LANES
YW = D_MODEL // LANES
TT_DISP = 256
TT_COMB = 128
VMEM_LIMIT = 56 * 1024 * 1024


def _cparams(n_axes, vmem=VMEM_LIMIT):
    return pltpu.CompilerParams(
        dimension_semantics=("arbitrary",) * n_axes, vmem_limit_bytes=vmem)


def _dot(a, b):
    return jnp.dot(a, b, preferred_element_type=F32)


def _slab(ref, token, n_tokens, width):
    first = pl.multiple_of(token * width, width)
    return ref.at[pl.ds(first, n_tokens * width), :]


def _dot_nt(a, b):
    return lax.dot_general(a, b, (((1,), (1,)), ((), ())), preferred_element_type=F32)


def _inproj_kernel(x_ref, g_ref, w_ref, cos_ref, slo_ref, shi_ref, qg_ref, ksg_ref, kwg_ref,
                   qn_ref, kvc_ref, ks_ref, vst_ref, kw_ref, vwt_ref, gt_ref, pool_ref, *, nS):
    x = x_ref[...]
    y = x * lax.rsqrt(jnp.mean(x * x, axis=-1, keepdims=True) + RMS_EPS)
    h = (y * g_ref[...]).astype(BF16)
    cos, slo, shi = cos_ref[...], slo_ref[...], shi_ref[...]
    G = NSA_KV_HEADS
    head = lambda z, c: z[:, c * HEAD_DIM:(c + 1) * HEAD_DIM]

    scale = HEAD_DIM ** -0.5 * LOG2_E
    zq = _dot(h, w_ref[:, 0:NSA_WIDTH])
    for c in range(NSA_HEADS):
        qn_ref[:, c * HEAD_DIM:(c + 1) * HEAD_DIM] = (
            _norm_rope(head(zq, c), qg_ref[...], cos, slo, shi) * scale).astype(BF16)

    pair = lambda p: _dot(h, w_ref[:, NSA_WIDTH + p * G * HEAD_DIM:NSA_WIDTH + (p + 1) * G * HEAD_DIM])
    zkc, zvc = pair(0), pair(1)
    for g in range(G):
        kvc_ref[g] = head(zkc, g)
        kvc_ref[G + g] = head(zvc, g)
    zks, zvs, zkw, zvw = pair(2), pair(3), pair(4), pair(5)
    row = lax.rem(pl.program_id(0), nS) * TM_IN + lax.broadcasted_iota(I32, (TM_IN, N_BLK_PAD), 0)
    lane = lax.broadcasted_iota(I32, (TM_IN, N_BLK_PAD), 1)
    onehot = jnp.where((row >> SLC_SHIFT) == lane, 1.0, 0.0).astype(BF16)
    for g in range(G):
        ks_ref[g, :, 0:HEAD_DIM] = _norm_rope(head(zks, g), ksg_ref[...], cos, slo, shi).astype(BF16)
        ks_ref[g, :, HEAD_DIM:HEAD_DIM + N_BLK_PAD] = onehot
        vst_ref[g, 0] = head(zvs, g).T.astype(BF16)
        kw_ref[g] = _norm_rope(head(zkw, g), kwg_ref[...], cos, slo, shi).astype(BF16)
        vwt_ref[g, 0] = head(zvw, g).T.astype(BF16)

    base = NSA_WIDTH + N_KV_COLS * HEAD_DIM
    pool_ref[...] = _dot(h, w_ref[:, base:base + POOL_WIDTH])
    sig_t = jax.nn.sigmoid(_dot(h, w_ref[:, base + POOL_WIDTH:base + POOL_WIDTH + LANES])).T
    for g in range(G):
        gt_ref[g] = sig_t[g * GATE_PAD:(g + 1) * GATE_PAD]


def _inproj(x2, g1, w_perm, cos, slo, shi, qg, ksg, kwg, B, S):
    T = x2.shape[0]
    n_cols = w_perm.shape[1]
    nS = S // TM_IN
    G = NSA_KV_HEADS
    full = lambda shape: pl.BlockSpec(shape, lambda i: (0,) * len(shape))
    tab = pl.BlockSpec((TM_IN, LANES), lambda i: (lax.rem(i, nS), 0))
    bi = lambda i: (i // nS, lax.rem(i, nS))
    return pl.pallas_call(
        functools.partial(_inproj_kernel, nS=nS),
        grid=(T // TM_IN,),
        in_specs=[
            pl.BlockSpec((TM_IN, D_MODEL), lambda i: (i, 0)),
            full((1, D_MODEL)),
            full((D_MODEL, n_cols)),
            tab, tab, tab,
            full((1, HEAD_DIM)), full((1, HEAD_DIM)), full((1, HEAD_DIM)),
        ],
        out_specs=[
            pl.BlockSpec((TM_IN, NSA_WIDTH), lambda i: (i, 0)),
            pl.BlockSpec((2 * G, TM_IN, LANES), lambda i: (0, i, 0)),
            pl.BlockSpec((G, TM_IN, 2 * HEAD_DIM), lambda i: (*bi(i), 0)),
            pl.BlockSpec((G, 1, HEAD_DIM, TM_IN), lambda i: (*bi(i), 0, 0)),
            pl.BlockSpec((G, TM_IN, HEAD_DIM), lambda i: (*bi(i), 0)),
            pl.BlockSpec((G, 1, HEAD_DIM, TM_IN), lambda i: (*bi(i), 0, 0)),
            pl.BlockSpec((G, GATE_PAD, TM_IN), lambda i: (i // nS, 0, lax.rem(i, nS))),
            pl.BlockSpec((TM_IN, POOL_WIDTH), lambda i: (i, 0)),
        ],
        out_shape=[
            jax.ShapeDtypeStruct((T, NSA_WIDTH), BF16),
            jax.ShapeDtypeStruct((2 * G, T, LANES), F32),
            jax.ShapeDtypeStruct((B * G, S, 2 * HEAD_DIM), BF16),
            jax.ShapeDtypeStruct((B * G, nS, HEAD_DIM, TM_IN), BF16),
            jax.ShapeDtypeStruct((B * G, S, HEAD_DIM), BF16),
            jax.ShapeDtypeStruct((B * G, nS, HEAD_DIM, TM_IN), BF16),
            jax.ShapeDtypeStruct((B * G, GATE_PAD, S), F32),
            jax.ShapeDtypeStruct((T, POOL_WIDTH), F32),
        ],
        compiler_params=_cparams(1),
        name="inproj",
    )(x2, g1, w_perm, cos, slo, shi, qg, ksg, kwg)


def _norm_rope(x, g, cos, sin_lo, sin_hi):
    half = ROT_DIM // 2
    y = x * lax.rsqrt(jnp.mean(x * x, axis=-1, keepdims=True) + QK_EPS) * g
    return (y * cos + pltpu.roll(y, LANES - half, 1) * sin_lo
            + pltpu.roll(y, half, 1) * sin_hi)


def _compress_body(x_ref, pos_ref, w1_ref, b1_ref, w2_ref, b2_ref):
    x = x_ref[0, 0]
    nc = x.shape[0]
    xa = (x + pos_ref[0:1, :]).astype(BF16)
    xb = (x + pos_ref[1:2, :]).astype(BF16)
    first = _dot(xa, w1_ref[0])
    second = _dot(xb, w1_ref[1])
    nxt = pltpu.roll(second, nc - 1, 0)
    hid = jax.nn.gelu(first + nxt + b1_ref[...])
    return _dot(hid.astype(BF16), w2_ref[...]) + b2_ref[...]


def _compress_k_kernel(x_ref, pos_ref, w1_ref, b1_ref, w2_ref, b2_ref,
                       g_ref, cos_ref, slo_ref, shi_ref, o_ref):
    out = _compress_body(x_ref, pos_ref, w1_ref, b1_ref, w2_ref, b2_ref)
    o_ref[0] = _norm_rope(out, g_ref[...], cos_ref[...], slo_ref[...], shi_ref[...]).astype(BF16)


def _compress_v_kernel(x_ref, pos_ref, w1_ref, b1_ref, w2_ref, b2_ref, o_ref):
    out = _compress_body(x_ref, pos_ref, w1_ref, b1_ref, w2_ref, b2_ref)
    o_ref[0] = out.T.astype(BF16)


def _compress(kvc4, which, pos2, w1, b1, w2, b2, B, S, rope=None):
    NC = S // CMP_STRIDE
    G = NSA_KV_HEADS
    CW = CMP_STRIDE * HEAD_DIM
    common_specs = [
        pl.BlockSpec((1, 1, NC, CW), lambda b, g: (which * G + g, b, 0, 0)),
        pl.BlockSpec((2, CW), lambda b, g: (0, 0)),
        pl.BlockSpec((2, CW, CMP_HIDDEN), lambda b, g: (0, 0, 0)),
        pl.BlockSpec((1, CMP_HIDDEN), lambda b, g: (0, 0)),
        pl.BlockSpec((CMP_HIDDEN, HEAD_DIM), lambda b, g: (0, 0)),
        pl.BlockSpec((1, HEAD_DIM), lambda b, g: (0, 0)),
    ]
    if rope is not None:
        gk, cos, slo, shi = rope
        tab = pl.BlockSpec((NC, LANES), lambda b, g: (0, 0))
        return pl.pallas_call(
            _compress_k_kernel,
            grid=(B, G),
            in_specs=common_specs + [pl.BlockSpec((1, HEAD_DIM), lambda b, g: (0, 0)), tab, tab, tab],
            out_specs=pl.BlockSpec((1, NC, HEAD_DIM), lambda b, g: (b * G + g, 0, 0)),
            out_shape=jax.ShapeDtypeStruct((B * G, NC, HEAD_DIM), BF16),
            compiler_params=_cparams(2),
            name="compress_k",
        )(kvc4, pos2, w1, b1, w2, b2, gk, cos, slo, shi)
    return pl.pallas_call(
        _compress_v_kernel,
        grid=(B, G),
        in_specs=common_specs,
        out_specs=pl.BlockSpec((1, HEAD_DIM, NC), lambda b, g: (b * G + g, 0, 0)),
        out_shape=jax.ShapeDtypeStruct((B * G, HEAD_DIM, NC), BF16),
        compiler_params=_cparams(2),
        name="compress_v",
    )(kvc4, pos2, w1, b1, w2, b2)


BIAS_DIAG, BIAS_FAR, BIAS_ALL, BIAS_NONE = 0, 1, 2, 3


def _tile_bias_table():
    k = lax.broadcasted_iota(I32, (TK, TQ), 0)
    t = lax.broadcasted_iota(I32, (TK, TQ), 1)
    neg = jnp.full((TK, TQ), MASK_VALUE, F32)
    zero = jnp.zeros((TK, TQ), F32)
    return jnp.stack([jnp.where(k <= t, zero, neg), jnp.where(k > t, zero, neg), neg, zero])


def _add_tile_bias(s, b):
    return jnp.concatenate([s[:, r * TQ:(r + 1) * TQ] + b for r in range(HEADS_PER_GROUP)], axis=1)


def _nsa_kernel(q_ref, kc_ref, vct_ref, ks_ref, vst_ref, kw_ref, vwt_ref, gt_ref, bias_ref, o_ref,
                qaug_ref, s_ref, cm_ref, m_ref, l_ref, acc_ref, oc_ref, ow_ref, *, S):
    R = HEADS_PER_GROUP
    NQ = R * TQ
    NC = S // CMP_STRIDE
    qi = pl.program_id(2)
    qs = qi * TQ

    q = q_ref[...].astype(F32)
    q_t = jnp.concatenate([q[:, r * HEAD_DIM:(r + 1) * HEAD_DIM].T for r in range(R)], axis=1).astype(BF16)
    col = lax.broadcasted_iota(I32, (1, NQ), 1)
    t_row = qs + (col & (TQ - 1))

    s_c = _dot(kc_ref[0], q_t)
    c_end = lax.broadcasted_iota(I32, (NC, 1), 0) * CMP_STRIDE + (CMP_BLK - 1)
    s_c = jnp.where(c_end <= t_row, s_c, MASK_VALUE)
    e_c = jnp.exp2(s_c - jnp.max(s_c, axis=0, keepdims=True))
    l_c = jnp.sum(e_c, axis=0, keepdims=True)
    p_c = e_c * jnp.where(t_row >= CMP_BLK - 1, 1.0 / l_c, 0.0)
    oc_ref[...] = _dot(vct_ref[0], p_c.astype(BF16))

    imp = p_c[:, 0:TQ]
    for r in range(1, R):
        imp = imp + p_c[:, r * TQ:(r + 1) * TQ]
    jj = lax.broadcasted_iota(I32, (N_BLK_PAD, NC), 0) * (SLC_BLK // CMP_STRIDE)
    nn = lax.broadcasted_iota(I32, (N_BLK_PAD, NC), 1)
    per = SLC_BLK // CMP_STRIDE
    fold = (jnp.where((nn >= jj) & (nn < jj + per), 1.0, 0.0)
            + jnp.where((nn >= jj - 1) & (nn < jj + per - 1), 1.0, 0.0)).astype(BF16)
    imp_hi = imp.astype(BF16)
    imp_lo = (imp - imp_hi.astype(F32)).astype(BF16)
    blk_score = _dot(fold, imp_hi) + _dot(fold, imp_lo)

    jb = lax.broadcasted_iota(I32, (N_BLK_PAD, TQ), 0)
    cur = (qs + lax.broadcasted_iota(I32, (N_BLK_PAD, TQ), 1)) >> SLC_SHIFT
    forced = (jb == 0) | (jb == cur) | (jb == cur - 1)
    val = jnp.where(forced, FORCE_VALUE, jnp.where(jb <= cur, blk_score, MASK_VALUE))
    jbf = jb.astype(F32)
    bias = jnp.where(forced, 0.0, MASK_VALUE)
    val = jnp.where(forced, -jnp.inf, val)
    for _ in range(min(N_SEL, S // SLC_BLK) - 3):
        mx = jnp.max(val, axis=0, keepdims=True)
        first = jnp.min(jnp.where(val == mx, jbf, float(N_BLK_PAD)), axis=0, keepdims=True)
        pick = jbf == first
        bias = jnp.where(pick, 0.0, bias)
        val = jnp.where(pick, -jnp.inf, val)
    qaug_ref[0:HEAD_DIM, :] = q_t
    qaug_ref[HEAD_DIM:HEAD_DIM + N_BLK_PAD, :] = jnp.concatenate([bias.astype(BF16)] * R, axis=1)

    w_tiles = ((jnp.maximum(qi - 2, 0), jnp.where(qi >= 2, BIAS_FAR, BIAS_ALL)),
               (jnp.maximum(qi - 1, 0), jnp.where(qi >= 1, BIAS_NONE, BIAS_ALL)),
               (qi, BIAS_DIAG))
    s_w = []
    for kt, bi in w_tiles:
        kwt = kw_ref[0, pl.ds(pl.multiple_of(kt * TK, TK), TK), :]
        s_w.append(_add_tile_bias(_dot(kwt, q_t), bias_ref[bi]))
    m_w = jnp.max(s_w[0], axis=0, keepdims=True)
    for s in s_w[1:]:
        m_w = jnp.maximum(m_w, jnp.max(s, axis=0, keepdims=True))
    l_w = jnp.zeros((1, NQ), F32)
    o_w = jnp.zeros((HEAD_DIM, NQ), F32)
    for (kt, _), s in zip(w_tiles, s_w):
        e = jnp.exp2(s - m_w)
        l_w = l_w + jnp.sum(e, axis=0, keepdims=True)
        o_w = o_w + _dot(vwt_ref[0, kt], e.astype(BF16))
    ow_ref[...] = o_w * (1.0 / l_w)

    m_ref[...] = jnp.full((1, NQ), MASK_VALUE, F32)
    l_ref[...] = jnp.zeros((1, NQ), F32)
    acc_ref[...] = jnp.zeros((HEAD_DIM, NQ), F32)

    def produce(kj, slot, causal):
        k = ks_ref[0, pl.ds(pl.multiple_of(kj * TK, TK), TK), :]
        s = _dot(k, qaug_ref[...])
        if causal:
            s = _add_tile_bias(s, bias_ref[BIAS_DIAG])
        s_ref[slot] = s
        cm_ref[slot] = jnp.max(s, axis=0, keepdims=True)

    def consume(kj, slot):
        m_old = m_ref[...]
        m_new = jnp.maximum(m_old, cm_ref[slot])
        alpha = jnp.exp2(m_old - m_new)
        p = jnp.exp2(s_ref[slot] - m_new)
        l_ref[...] = alpha * l_ref[...] + jnp.sum(p, axis=0, keepdims=True)
        acc_ref[...] = alpha * acc_ref[...] + _dot(vst_ref[0, kj], p.astype(BF16))
        m_ref[...] = m_new

    def stage(kj, slot, causal_next):
        produce(kj + 1, 1 - slot, causal_next)
        consume(kj, slot)

    @pl.when(qi == 0)
    def _():
        produce(0, 0, True)

    @pl.when(qi > 0)
    def _():
        produce(0, 0, False)

    def stage_group(pp, carry):
        for u in range(SEL_UNROLL):
            stage(SEL_UNROLL * pp + u, u & 1, False)
        return carry

    n_plain = jnp.maximum(qi - 1, 0)
    n_grouped = n_plain // SEL_UNROLL * SEL_UNROLL
    lax.fori_loop(0, n_plain // SEL_UNROLL, stage_group, 0)
    for u in range(SEL_UNROLL - 1):
        @pl.when(n_plain - n_grouped > u)
        def _(u=u):
            stage(n_grouped + u, u & 1, False)

    @pl.when((qi >= 1) & ((n_plain & 1) == 0))
    def _():
        stage(qi - 1, 0, True)
        consume(qi, 1)

    @pl.when((qi >= 1) & ((n_plain & 1) == 1))
    def _():
        stage(qi - 1, 1, True)
        consume(qi, 0)

    @pl.when(qi == 0)
    def _():
        consume(0, 0)

    inv_l = 1.0 / l_ref[...]

    for r in range(R):
        sl = slice(r * TQ, (r + 1) * TQ)
        g0 = gt_ref[0, r * N_BRANCH + 0:r * N_BRANCH + 1, :]
        g1 = gt_ref[0, r * N_BRANCH + 1:r * N_BRANCH + 2, :]
        g2 = gt_ref[0, r * N_BRANCH + 2:r * N_BRANCH + 3, :]
        o = (g0 * oc_ref[:, sl] + (g1 * inv_l[:, sl]) * acc_ref[:, sl]
             + g2 * ow_ref[:, sl])
        o_ref[:, r * HEAD_DIM:(r + 1) * HEAD_DIM] = o.T.astype(BF16)


def _nsa(qn, kc, vct, ks, vst, kw, vwt, gt, B, S):
    G = NSA_KV_HEADS
    R = HEADS_PER_GROUP
    nQ = S // TQ
    nK = S // TK
    NC = S // CMP_STRIDE
    bg = lambda b, g, i: b * G + g
    return pl.pallas_call(
        functools.partial(_nsa_kernel, S=S),
        grid=(B, G, nQ),
        in_specs=[
            pl.BlockSpec((TQ, R * HEAD_DIM), lambda b, g, i: (b * nQ + i, g)),
            pl.BlockSpec((1, NC, HEAD_DIM), lambda b, g, i: (bg(b, g, i), 0, 0)),
            pl.BlockSpec((1, HEAD_DIM, NC), lambda b, g, i: (bg(b, g, i), 0, 0)),
            pl.BlockSpec((1, S, 2 * HEAD_DIM), lambda b, g, i: (bg(b, g, i), 0, 0)),
            pl.BlockSpec((1, nK, HEAD_DIM, TK), lambda b, g, i: (bg(b, g, i), 0, 0, 0)),
            pl.BlockSpec((1, S, HEAD_DIM), lambda b, g, i: (bg(b, g, i), 0, 0)),
            pl.BlockSpec((1, nK, HEAD_DIM, TK), lambda b, g, i: (bg(b, g, i), 0, 0, 0)),
            pl.BlockSpec((1, GATE_PAD, TQ), lambda b, g, i: (bg(b, g, i), 0, i)),
            pl.BlockSpec((4, TK, TQ), lambda b, g, i: (0, 0, 0)),
        ],
        out_specs=pl.BlockSpec((TQ, R * HEAD_DIM), lambda b, g, i: (b * nQ + i, g)),
        out_shape=jax.ShapeDtypeStruct((B * S, NSA_WIDTH), BF16),
        scratch_shapes=[
            pltpu.VMEM((2 * HEAD_DIM, R * TQ), BF16),
            pltpu.VMEM((2, TK, R * TQ), F32),
            pltpu.VMEM((2, 1, R * TQ), F32),
            pltpu.VMEM((1, R * TQ), F32),
            pltpu.VMEM((1, R * TQ), F32),
            pltpu.VMEM((HEAD_DIM, R * TQ), F32),
            pltpu.VMEM((HEAD_DIM, R * TQ), F32),
            pltpu.VMEM((HEAD_DIM, R * TQ), F32),
        ],
        compiler_params=_cparams(3),
        name="nsa",
    )(qn, kc, vct, ks, vst, kw, vwt, gt, _tile_bias_table())


def _mix_kernel(nsa_ref, pool_ref, halo_ref, x_ref, wo_ref, pw_ref, ps_ref, g2_ref,
                wr2_ref, br_ref, x1_ref, h2p_ref, logit_ref, ext_ref, *, S):
    nS = S // TM_MIX
    i = pl.program_id(0)
    t0 = lax.rem(i, nS) * TM_MIX
    ext_ref[0:POOL_HALO, :] = jnp.where(t0 == 0, 0.0, halo_ref[...])
    ext_ref[POOL_HALO:POOL_HALO + TM_MIX, :] = pool_ref[...]
    for sb in range(TM_MIX // MIX_SUB):
        r0 = sb * MIX_SUB
        rows = slice(r0, r0 + MIX_SUB)
        t = t0 + r0 + lax.broadcasted_iota(I32, (MIX_SUB, POOL_GC), 0)
        acc = _dot(nsa_ref[rows, :], wo_ref[0:NSA_WIDTH, :])
        for gi, w in enumerate(POOL_WINDOWS):
            cs = slice(gi * POOL_GC, (gi + 1) * POOL_GC)
            v = pool_ref[rows, cs]
            tot = v
            for k in range(1, w):
                tot = tot + ext_ref[POOL_HALO + r0 - k:POOL_HALO + r0 - k + MIX_SUB, cs]
            cnt = jnp.minimum(t + 1, w).astype(F32)
            d = tot / cnt - v
            y = _dot(d.astype(BF16), pw_ref[gi]) * ps_ref[:, cs]
            acc = acc + _dot(y.astype(BF16),
                             wo_ref[NSA_WIDTH + gi * POOL_GC:NSA_WIDTH + (gi + 1) * POOL_GC, :])
        x1 = x_ref[rows, :] + acc
        x1_ref[rows, :] = x1
        h2 = x1 * lax.rsqrt(jnp.mean(x1 * x1, axis=-1, keepdims=True) + RMS_EPS) * g2_ref[...]
        half = D_MODEL // 2
        packed = pltpu.pack_elementwise([h2[:, 0:half], h2[:, half:D_MODEL]], packed_dtype=BF16)
        for c in range(XW):
            h2p_ref[pl.ds(r0 * XW + c, MIX_SUB, stride=XW), :] = packed[:, c * LANES:(c + 1) * LANES]
        hi = h2.astype(BF16)
        lo = (h2 - hi.astype(F32)).astype(BF16)
        hl = _dot(hi, wr2_ref[...])
        logit_ref[rows, :] = (hl[:, 0:LANES] + hl[:, LANES:2 * LANES] + _dot(lo, wr2_ref[:, 0:LANES])
                              + br_ref[...])


def _mix(nsa_out, pool_in, x2, wo, pw, ps, g2, wr2, br, S):
    T = x2.shape[0]
    hb = TM_MIX // POOL_HALO
    full = lambda shape: pl.BlockSpec(shape, lambda i: (0,) * len(shape))
    return pl.pallas_call(
        functools.partial(_mix_kernel, S=S),
        grid=(T // TM_MIX,),
        in_specs=[
            pl.BlockSpec((TM_MIX, NSA_WIDTH), lambda i: (i, 0)),
            pl.BlockSpec((TM_MIX, POOL_WIDTH), lambda i: (i, 0)),
            pl.BlockSpec((POOL_HALO, POOL_WIDTH), lambda i: (jnp.maximum(i * hb - 1, 0), 0)),
            pl.BlockSpec((TM_MIX, D_MODEL), lambda i: (i, 0)),
            full((D_MODEL, D_MODEL)),
            full((len(POOL_WINDOWS), POOL_GC, POOL_GC)),
            full((1, POOL_WIDTH)),
            full((1, D_MODEL)),
            full((D_MODEL, 2 * LANES)),
            full((1, LANES)),
        ],
        out_specs=[
            pl.BlockSpec((TM_MIX, D_MODEL), lambda i: (i, 0)),
            pl.BlockSpec((TM_MIX * XW, LANES), lambda i: (i, 0)),
            pl.BlockSpec((TM_MIX, LANES), lambda i: (i, 0)),
        ],
        out_shape=[
            jax.ShapeDtypeStruct((T, D_MODEL), F32),
            jax.ShapeDtypeStruct((T * XW, LANES), U32),
            jax.ShapeDtypeStruct((T, LANES), F32),
        ],
        scratch_shapes=[pltpu.VMEM((POOL_HALO + TM_MIX, POOL_WIDTH), F32)],
        compiler_params=_cparams(1),
        name="mix",
    )(nsa_out, pool_in, pool_in, x2, wo, pw, ps, g2, wr2, br)


def _route_kernel(logit_ref, dest_ref, wt_ref, meta_ref, cnt_ref, run_ref, *, n_tiles):
    phase = pl.program_id(0)
    i = pl.program_id(1)
    TT = TT_ROUTE
    E = N_EXPERTS

    @pl.when((phase == 0) & (i == 0))
    def _():
        cnt_ref[...] = jnp.zeros_like(cnt_ref)
        run_ref[...] = jnp.zeros_like(run_ref)

    lt = logit_ref[...].T[0:E, :]
    ef = lax.broadcasted_iota(I32, (E, TT), 0).astype(F32)
    work = lt
    ids, vals, hots = [], [], []
    for _ in range(TOP_K):
        mx = jnp.max(work, axis=0, keepdims=True)
        eid = jnp.min(jnp.where(work == mx, ef, float(E)), axis=0, keepdims=True)
        hot = ef == eid
        ids.append(eid)
        vals.append(mx)
        hots.append(hot)
        work = jnp.where(hot, -jnp.inf, work)
    member = jnp.where(hots[0] | hots[1] | hots[2] | hots[3], 1.0, 0.0)
    tile_cnt = jnp.sum(member, axis=1, keepdims=True)

    @pl.when(phase == 0)
    def _():
        cnt_ref[...] = cnt_ref[...] + tile_cnt

    @pl.when(phase == 1)
    def _():
        ex = [jnp.exp(v - vals[0]) for v in vals]
        den = ex[0] + ex[1] + ex[2] + ex[3]
        cnt = jnp.broadcast_to(cnt_ref[...], (E, LANES))
        padded = jnp.ceil(cnt * (1.0 / TM_G)) * TM_G
        erow = lax.broadcasted_iota(I32, (E, LANES), 0)
        start = jnp.zeros((E, LANES), F32)
        running = jnp.zeros((1, LANES), F32)
        for e in range(E):
            start = jnp.where(erow == e, running, start)
            running = running + padded[e:e + 1, :]
        end = start + padded
        tri = jnp.where(lax.broadcasted_iota(I32, (TT, TT), 0) < lax.broadcasted_iota(I32, (TT, TT), 1),
                        1.0, 0.0).astype(BF16)
        before = _dot(member.astype(BF16), tri) + run_ref[...]
        slot = before + start[:, 0:1]
        for k in range(TOP_K):
            d = jnp.sum(jnp.where(hots[k], slot, 0.0), axis=0, keepdims=True)
            dest_ref[k:k + 1, :] = d.astype(I32)
            wt_ref[k:k + 1, :] = ex[k] / den
        run_ref[...] = run_ref[...] + tile_cnt

        @pl.when(i == 0)
        def _():
            lanes = meta_ref.shape[1]
            endw = jnp.broadcast_to(end[:, 0:1], (E, lanes))
            tile_row = (lax.broadcasted_iota(I32, (E, lanes), 1) * TM_G).astype(F32)
            owner = jnp.minimum(jnp.sum(jnp.where(endw <= tile_row, 1.0, 0.0), axis=0, keepdims=True),
                                float(E - 1))
            nxt = jnp.full((E, LANES), -1.0, F32)
            later = jnp.full((1, LANES), -1.0, F32)
            for e in reversed(range(E)):
                nxt = jnp.where(erow == e, later, nxt)
                later = jnp.where(cnt[e:e + 1, :] > 0.0, float(e), later)
            erow_w = lax.broadcasted_iota(I32, (E, lanes), 0).astype(F32)
            nxt_tile = jnp.sum(jnp.where(erow_w == owner, jnp.broadcast_to(nxt[:, 0:1], (E, lanes)), 0.0),
                               axis=0, keepdims=True)
            meta_ref[0:1, :] = owner.astype(I32)
            meta_ref[1:2, :] = jnp.where(tile_row[0:1, :] < running[:, 0:1], 1, 0).astype(I32)
            meta_ref[2:3, :] = nxt_tile.astype(I32)
            lane_w = lax.broadcasted_iota(I32, (E, lanes), 1).astype(F32)
            meta_ref[3:4, :] = jnp.sum(jnp.where(erow_w == lane_w, endw, 0.0), axis=0,
                                       keepdims=True).astype(I32)
            meta_ref[4:8, :] = jnp.zeros((4, lanes), I32)


def _route(logits, n_tiles):
    T = logits.shape[0]
    nT = T // TT_ROUTE
    lanes = -(-n_tiles // LANES) * LANES
    return pl.pallas_call(
        functools.partial(_route_kernel, n_tiles=n_tiles),
        grid=(2, nT),
        in_specs=[pl.BlockSpec((TT_ROUTE, LANES), lambda p, i: (i, 0))],
        out_specs=[
            pl.BlockSpec((TOP_K, TT_ROUTE), lambda p, i: (0, i * p)),
            pl.BlockSpec((TOP_K, TT_ROUTE), lambda p, i: (0, i * p)),
            pl.BlockSpec((8, lanes), lambda p, i: (0, 0)),
        ],
        out_shape=[
            jax.ShapeDtypeStruct((TOP_K, T), I32),
            jax.ShapeDtypeStruct((TOP_K, T), F32),
            jax.ShapeDtypeStruct((8, lanes), I32),
        ],
        scratch_shapes=[pltpu.VMEM((N_EXPERTS, 1), F32), pltpu.VMEM((N_EXPERTS, 1), F32)],
        compiler_params=_cparams(2),
        name="route",
    )(logits)


def _dispatch_kernel(ends_ref, dest_ref, h_ref, xs_ref, zero_ref, sem, zsem):
    @pl.when(pl.program_id(0) == 0)
    def _():
        zero_ref[...] = jnp.zeros_like(zero_ref)

        def tail_copy(e):
            return pltpu.make_async_copy(zero_ref, _slab(xs_ref, ends_ref[e] - TM_G, TM_G, XW), zsem)

        def nonempty(e):
            return ends_ref[e] > (ends_ref[e - 1] if e else 0)

        for e in range(N_EXPERTS):
            @pl.when(nonempty(e))
            def _(e=e):
                tail_copy(e).start()
        for e in range(N_EXPERTS):
            @pl.when(nonempty(e))
            def _(e=e):
                tail_copy(e).wait()

        def unused_copy(tile):
            return pltpu.make_async_copy(zero_ref, _slab(xs_ref, tile * TM_G, TM_G, XW), zsem)

        n_used = lax.div(ends_ref[N_EXPERTS - 1], TM_G)
        n_all = xs_ref.shape[0] // (TM_G * XW)

        def start_unused(tile, carry):
            unused_copy(tile).start()
            return carry

        def wait_unused(tile, carry):
            unused_copy(tile).wait()
            return carry

        lax.fori_loop(n_used, n_all, start_unused, 0)
        lax.fori_loop(n_used, n_all, wait_unused, 0)

    def issue(t, carry):
        for k in range(TOP_K):
            pltpu.make_async_copy(_slab(h_ref, t, 1, XW), _slab(xs_ref, dest_ref[k, t], 1, XW), sem).start()
        return carry

    lax.fori_loop(0, TT_DISP, issue, 0)
    all_rows = _slab(xs_ref, 0, TOP_K * TT_DISP, XW)
    pltpu.make_async_copy(all_rows, all_rows, sem).wait()


def _dispatch(ends, dest, h2p, n_rows):
    T = h2p.shape[0] // XW
    grid_spec = pltpu.PrefetchScalarGridSpec(
        num_scalar_prefetch=1,
        grid=(T // TT_DISP,),
        in_specs=[
            pl.BlockSpec((TOP_K, TT_DISP), lambda i, ends: (0, i), memory_space=pltpu.SMEM),
            pl.BlockSpec((TT_DISP * XW, LANES), lambda i, ends: (i, 0)),
        ],
        out_specs=pl.BlockSpec(memory_space=pl.ANY),
        scratch_shapes=[pltpu.VMEM((TM_G * XW, LANES), U32), pltpu.SemaphoreType.DMA,
                        pltpu.SemaphoreType.DMA],
    )
    return pl.pallas_call(
        _dispatch_kernel,
        grid_spec=grid_spec,
        out_shape=jax.ShapeDtypeStruct((n_rows * XW, LANES), U32),
        compiler_params=_cparams(1),
        name="dispatch",
    )(ends, dest, h2p)


def _expert_changed(te_ref, i):
    return (i == 0) | (te_ref[i] != te_ref[jnp.maximum(i - 1, 0)])


WEIGHT_DMA_PRIORITY = 1


def _stream_expert_weights(te_ref, tv_ref, nx_ref, copies, wst_ref, wbf_ref, slot_ref):
    j = pl.program_id(0)
    i = pl.program_id(1)

    @pl.when((j == 0) & (i == 0))
    def _():
        slot_ref[0] = 0
        for c in copies(te_ref[0], 0, 0):
            c.start(priority=WEIGHT_DMA_PRIORITY)

    @pl.when((tv_ref[i] > 0) & _expert_changed(te_ref, i))
    def _():
        slot = slot_ref[0]
        e_next = nx_ref[i]

        @pl.when(e_next >= 0)
        def _():
            for c in copies(e_next, j, 1 - slot):
                c.start(priority=WEIGHT_DMA_PRIORITY)

        @pl.when((e_next < 0) & (j + 1 < pl.num_programs(0)))
        def _():
            for c in copies(te_ref[0], j + 1, 1 - slot):
                c.start(priority=WEIGHT_DMA_PRIORITY)

        for c in copies(te_ref[i], j, slot):
            c.wait()
        wbf_ref[...] = wst_ref[slot].astype(BF16)
        slot_ref[0] = 1 - slot


def _gemm1_kernel(te_ref, tv_ref, nx_ref, x_ref, bg_ref, bu_ref, w_hbm, act_ref,
                  wst_ref, wbf_ref, slot_ref, sem):
    i = pl.program_id(1)
    valid = tv_ref[i] > 0

    def copies(e, j, slot):
        col = pl.multiple_of(j * TN_G1, TN_G1)
        return (pltpu.make_async_copy(w_hbm.at[e, :, pl.ds(col, TN_G1)],
                                      wst_ref.at[slot, :, pl.ds(0, TN_G1)], sem.at[slot]),
                pltpu.make_async_copy(w_hbm.at[e, :, pl.ds(D_FF + col, TN_G1)],
                                      wst_ref.at[slot, :, pl.ds(TN_G1, TN_G1)], sem.at[slot]))

    _stream_expert_weights(te_ref, tv_ref, nx_ref, copies, wst_ref, wbf_ref, slot_ref)

    @pl.when(valid)
    def _():
        half = D_MODEL // 2
        xp = jnp.concatenate([x_ref[pl.ds(c, TM_G, stride=XW), :] for c in range(XW)], axis=1)
        lo = pltpu.unpack_elementwise(xp, index=0, packed_dtype=BF16, unpacked_dtype=F32).astype(BF16)
        hi = pltpu.unpack_elementwise(xp, index=1, packed_dtype=BF16, unpacked_dtype=F32).astype(BF16)
        gu = _dot(lo, wbf_ref[0:half, :]) + _dot(hi, wbf_ref[half:D_MODEL, :])
        gate = jnp.minimum(gu[:, 0:TN_G1] + bg_ref[0], SWIGLU_LIMIT)
        up = jnp.clip(gu[:, TN_G1:2 * TN_G1] + bu_ref[0], -SWIGLU_LIMIT, SWIGLU_LIMIT)
        act = (up + 1.0) * gate * jax.nn.sigmoid(SWIGLU_ALPHA * gate)
        act_ref[...] = act.astype(BF16)

    @pl.when(jnp.logical_not(valid))
    def _():
        act_ref[...] = jnp.zeros_like(act_ref)


def _gemm1(te, tv, nx, xs, w_gu, b_gu3, n_tiles):
    n_rows = xs.shape[0] // XW
    nJ = D_FF // TN_G1
    grid_spec = pltpu.PrefetchScalarGridSpec(
        num_scalar_prefetch=3,
        grid=(nJ, n_tiles),
        in_specs=[
            pl.BlockSpec((TM_G * XW, LANES), lambda j, i, te, tv, nx: (i, 0)),
            pl.BlockSpec((1, 1, TN_G1), lambda j, i, te, tv, nx: (te[i], 0, j)),
            pl.BlockSpec((1, 1, TN_G1), lambda j, i, te, tv, nx: (te[i], 0, nJ + j)),
            pl.BlockSpec(memory_space=pl.ANY),
        ],
        out_specs=pl.BlockSpec((TM_G, TN_G1), lambda j, i, te, tv, nx: (i, j)),
        scratch_shapes=[
            pltpu.VMEM((2, D_MODEL, 2 * TN_G1), F32),
            pltpu.VMEM((D_MODEL, 2 * TN_G1), BF16),
            pltpu.SMEM((1,), I32),
            pltpu.SemaphoreType.DMA((2,)),
        ],
    )
    return pl.pallas_call(
        _gemm1_kernel,
        grid_spec=grid_spec,
        out_shape=jax.ShapeDtypeStruct((n_rows, D_FF), BF16),
        compiler_params=_cparams(2),
        name="gemm1",
    )(te, tv, nx, xs, b_gu3, b_gu3, w_gu)


def _gemm2_kernel(te_ref, tv_ref, nx_ref, a_ref, b_ref, w_hbm, y_ref, wst_ref, wbf_ref, slot_ref, sem):
    i = pl.program_id(1)
    valid = tv_ref[i] > 0

    def copies(e, j, slot):
        col = pl.multiple_of(j * TN_G2, TN_G2)
        return (pltpu.make_async_copy(w_hbm.at[e, :, pl.ds(col, TN_G2)], wst_ref.at[slot], sem.at[slot]),)

    _stream_expert_weights(te_ref, tv_ref, nx_ref, copies, wst_ref, wbf_ref, slot_ref)

    @pl.when(valid)
    def _():
        y = _dot(a_ref[...], wbf_ref[...]) + b_ref[0]
        for c in range(YW):
            y_ref[pl.ds(c, TM_G, stride=YW), :] = y[:, c * LANES:(c + 1) * LANES]

    @pl.when(jnp.logical_not(valid))
    def _():
        y_ref[...] = jnp.zeros_like(y_ref)


def _gemm2(te, tv, nx, act, w_d, b_d3, n_tiles):
    n_rows = act.shape[0]
    nJ = D_MODEL // TN_G2
    grid_spec = pltpu.PrefetchScalarGridSpec(
        num_scalar_prefetch=3,
        grid=(nJ, n_tiles),
        in_specs=[
            pl.BlockSpec((TM_G, D_FF), lambda j, i, te, tv, nx: (i, 0)),
            pl.BlockSpec((1, 1, TN_G2), lambda j, i, te, tv, nx: (te[i], 0, j)),
            pl.BlockSpec(memory_space=pl.ANY),
        ],
        out_specs=pl.BlockSpec((TM_G * YW, LANES), lambda j, i, te, tv, nx: (i, 0)),
        scratch_shapes=[
            pltpu.VMEM((2, D_FF, TN_G2), F32),
            pltpu.VMEM((D_FF, TN_G2), BF16),
            pltpu.SMEM((1,), I32),
            pltpu.SemaphoreType.DMA((2,)),
        ],
    )
    return pl.pallas_call(
        _gemm2_kernel,
        grid_spec=grid_spec,
        out_shape=jax.ShapeDtypeStruct((n_rows * YW, LANES), F32),
        compiler_params=_cparams(2),
        name="gemm2",
    )(te, tv, nx, act, b_d3, w_d)


def _combine_kernel(dest_ref, dnext_ref, wt_ref, x1_ref, y_ref, o_ref, buf_ref, acc_ref, sem):
    i = pl.program_id(0)
    n = pl.num_programs(0)

    def gather(d_ref, slot):
        def issue(t, carry):
            for k in range(TOP_K):
                pltpu.make_async_copy(_slab(y_ref, d_ref[k, t], 1, YW),
                                      _slab(buf_ref.at[slot, k], t, 1, YW), sem.at[slot]).start()
            return carry
        lax.fori_loop(0, TT_COMB, issue, 0)

    def finish(slot):
        pltpu.make_async_copy(buf_ref.at[slot], buf_ref.at[slot], sem.at[slot]).wait()
        acc = None
        for k in range(TOP_K):
            w = jnp.broadcast_to(wt_ref[:, k:k + 1], (TT_COMB, LANES))
            w = jnp.broadcast_to(w[:, None, :], (TT_COMB, YW, LANES)).reshape(TT_COMB * YW, LANES)
            term = buf_ref[slot, k] * w
            acc = term if acc is None else acc + term
        acc_ref[...] = acc
        for c in range(YW):
            cs = slice(c * LANES, (c + 1) * LANES)
            o_ref[:, cs] = x1_ref[:, cs] + acc_ref[pl.ds(c, TT_COMB, stride=YW), :]

    @pl.when(i == 0)
    def _():
        gather(dest_ref, 0)

    for slot in range(2):
        @pl.when((i & 1) == slot)
        def _(slot=slot):
            @pl.when(i + 1 < n)
            def _():
                gather(dnext_ref, 1 - slot)
            finish(slot)


def _combine(dest, wt_tok, x1, y):
    T = x1.shape[0]
    nT = T // TT_COMB
    return pl.pallas_call(
        _combine_kernel,
        grid=(nT,),
        in_specs=[
            pl.BlockSpec((TOP_K, TT_COMB), lambda i: (0, i), memory_space=pltpu.SMEM),
            pl.BlockSpec((TOP_K, TT_COMB), lambda i: (0, jnp.minimum(i + 1, nT - 1)),
                         memory_space=pltpu.SMEM),
            pl.BlockSpec((TT_COMB, TOP_K), lambda i: (i, 0)),
            pl.BlockSpec((TT_COMB, D_MODEL), lambda i: (i, 0)),
            pl.BlockSpec(memory_space=pl.ANY),
        ],
        out_specs=pl.BlockSpec((TT_COMB, D_MODEL), lambda i: (i, 0)),
        out_shape=jax.ShapeDtypeStruct((T, D_MODEL), F32),
        scratch_shapes=[pltpu.VMEM((2, TOP_K, TT_COMB * YW, LANES), F32),
                        pltpu.VMEM((TT_COMB * YW, LANES), F32), pltpu.SemaphoreType.DMA((2,))],
        compiler_params=_cparams(1),
        name="combine",
    )(dest, dest, wt_tok, x1, y)


def _rope_tables(pos):
    half = ROT_DIM // 2
    inv_freq = ROPE_THETA ** (-jnp.arange(0, ROT_DIM, 2, dtype=F32) / ROT_DIM)
    ang = pos[:, None] * inv_freq[None, :]
    cos, sin = jnp.cos(ang), jnp.sin(ang)
    n = pos.shape[0]
    ones = jnp.ones((n, LANES - ROT_DIM), F32)
    zeros = jnp.zeros((n, LANES - ROT_DIM), F32)
    zh = jnp.zeros((n, half), F32)
    return (jnp.concatenate([cos, cos, ones], axis=1),
            jnp.concatenate([-sin, zh, zeros], axis=1),
            jnp.concatenate([zh, sin, zeros], axis=1))


def _permute_w_in(w_in):
    kv_end = NSA_WIDTH + N_KV_COLS * HEAD_DIM
    n_gate = NSA_HEADS * N_BRANCH
    per_g = HEADS_PER_GROUP * N_BRANCH
    gate = w_in[:, kv_end:kv_end + n_gate]
    pieces = [w_in[:, :kv_end], w_in[:, kv_end + n_gate:]]
    zpad = jnp.zeros((D_MODEL, GATE_PAD - per_g), w_in.dtype)
    for g in range(NSA_KV_HEADS):
        pieces += [gate[:, g * per_g:(g + 1) * per_g], zpad]
    pieces.append(jnp.zeros((D_MODEL, LANES - NSA_KV_HEADS * GATE_PAD), w_in.dtype))
    return jnp.concatenate(pieces, axis=1).astype(BF16)


def _layer(x, norm1_g, w_in, q_norm_g, k_norm_cmp_g, k_norm_slc_g, k_norm_win_g,
           cmp_k_pos, cmp_k_w1, cmp_k_b1, cmp_k_w2, cmp_k_b2,
           cmp_v_pos, cmp_v_w1, cmp_v_b1, cmp_v_w2, cmp_v_b2,
           pool_w, pool_scale, w_out, norm2_g,
           w_router, b_router, w_gate_up, b_gate_up, w_down, b_down):
    B, S, _ = x.shape
    T = B * S
    assert S % TQ == 0 and TQ == TK == TM_IN and WINDOW == 2 * TK and S // SLC_BLK <= N_BLK_PAD
    assert (S // CMP_STRIDE) % LANES == 0 and T % TT_ROUTE == 0
    NC = S // CMP_STRIDE
    x2 = x.reshape(T, D_MODEL)
    row = lambda v: v.reshape(1, -1)

    cos, slo, shi = _rope_tables(jnp.arange(S, dtype=F32))
    qn, kvc, ks, vst, kw, vwt, gt, pool_in = _inproj(
        x2, row(norm1_g), _permute_w_in(w_in), cos, slo, shi,
        row(q_norm_g), row(k_norm_slc_g), row(k_norm_win_g), B, S)

    c_end = (jnp.arange(NC, dtype=I32) * CMP_STRIDE + (CMP_BLK - 1)).astype(F32)
    ccos, cslo, cshi = _rope_tables(c_end)
    kvc4 = kvc.reshape(4, B, NC, CMP_STRIDE * HEAD_DIM)
    cw = CMP_STRIDE * HEAD_DIM
    kc = _compress(kvc4, 0, cmp_k_pos.reshape(2, cw), cmp_k_w1.reshape(2, cw, CMP_HIDDEN).astype(BF16),
                   row(cmp_k_b1), cmp_k_w2.astype(BF16), row(cmp_k_b2), B, S,
                   rope=(row(k_norm_cmp_g), ccos, cslo, cshi))
    vct = _compress(kvc4, 1, cmp_v_pos.reshape(2, cw), cmp_v_w1.reshape(2, cw, CMP_HIDDEN).astype(BF16),
                    row(cmp_v_b1), cmp_v_w2.astype(BF16), row(cmp_v_b2), B, S)

    nsa_out = _nsa(qn, kc, vct, ks, vst, kw, vwt, gt, B, S)

    wr_pad = jnp.pad(w_router, ((0, 0), (0, LANES - N_EXPERTS)))
    wr_hi = wr_pad.astype(BF16)
    wr_lo = (wr_pad - wr_hi.astype(F32)).astype(BF16)
    br_pad = jnp.concatenate([b_router.astype(F32), jnp.full((LANES - N_EXPERTS,), MASK_VALUE, F32)])
    x1, h2p, logits = _mix(nsa_out, pool_in, x2, w_out.astype(BF16), pool_w.astype(BF16),
                           row(pool_scale), row(norm2_g), jnp.concatenate([wr_hi, wr_lo], axis=1),
                           row(br_pad), S)

    n_tiles = T * TOP_K // TM_G + N_EXPERTS
    dest, wts, meta = _route(logits, n_tiles)
    te, tv, nx = meta[0, :n_tiles], meta[1, :n_tiles], meta[2, :n_tiles]
    xs = _dispatch(meta[3, :N_EXPERTS], dest, h2p, n_tiles * TM_G)
    act = _gemm1(te, tv, nx, xs, w_gate_up, b_gate_up.reshape(N_EXPERTS, 1, 2 * D_FF), n_tiles)
    y = _gemm2(te, tv, nx, act, w_down, b_down.reshape(N_EXPERTS, 1, D_MODEL), n_tiles)
    out = _combine(dest, wts.T, x1, y)
    return out.reshape(B, S, D_MODEL)


def kernel(x, norm1_g, w_in, q_norm_g, k_norm_cmp_g, k_norm_slc_g, k_norm_win_g, cmp_k_pos, cmp_k_w1, cmp_k_b1, cmp_k_w2, cmp_k_b2, cmp_v_pos, cmp_v_w1, cmp_v_b1, cmp_v_w2, cmp_v_b2, pool_w, pool_scale, w_out, norm2_g, w_router, b_router, w_gate_up, b_gate_up, w_down, b_down):
    params = (norm1_g, w_in, q_norm_g, k_norm_cmp_g, k_norm_slc_g, k_norm_win_g,
              cmp_k_pos, cmp_k_w1, cmp_k_b1, cmp_k_w2, cmp_k_b2,
              cmp_v_pos, cmp_v_w1, cmp_v_b1, cmp_v_w2, cmp_v_b2,
              pool_w, pool_scale, w_out, norm2_g,
              w_router, b_router, w_gate_up, b_gate_up, w_down, b_down)
    depth = norm1_g.shape[0]
    for l in range(depth):
        x = _layer(x, *[p.reshape(p.shape[1:]) if depth == 1 else p[l] for p in params])
    return x
```

```python
import functools

import jax
import jax.numpy as jnp
from jax import lax
from jax.experimental import pallas as pl
from jax.experimental.pallas import tpu as pltpu

F32 = jnp.float32
BF16 = jnp.bfloat16
I32 = jnp.int32
U32 = jnp.uint32

D_MODEL = 2048
HEAD_DIM = 128
NSA_HEADS = 8
NSA_KV_HEADS = 2
HEADS_PER_GROUP = NSA_HEADS // NSA_KV_HEADS
NSA_WIDTH = NSA_HEADS * HEAD_DIM
N_BRANCH = 3
CMP_BLK = 32
CMP_STRIDE = 16
CMP_HIDDEN = 256
SLC_BLK = 64
SLC_SHIFT = SLC_BLK.bit_length() - 1
N_SEL = 16
WINDOW = 512
ROPE_THETA = 500000.0
ROT_DIM = HEAD_DIM // 4
POOL_WIDTH = 1024
POOL_WINDOWS = (2, 4, 8, 16)
POOL_GC = 256
N_EXPERTS = 32
TOP_K = 4
D_FF = 2048
SWIGLU_ALPHA = 1.702
SWIGLU_LIMIT = 7.0
RMS_EPS = 1e-5
QK_EPS = 1e-6
MASK_VALUE = -1e30
FORCE_VALUE = 1e30
LOG2_E = 1.4426950408889634

LANES = 128
N_KV_COLS = 6 * NSA_KV_HEADS
GATE_PAD = 16
N_BLK_PAD = 128
POOL_HALO = 16

TM_IN = 256
TQ = 256
TK = 256
SEL_UNROLL = 4
TM_MIX = 256
MIX_SUB = 256
TT_ROUTE = 1024
TM_G = 256
TN_G1 = 1024
TN_G2 = 2048
XW = D_MODEL // 2 // LANES
TT_DISP = 256
TT_COMB = 256
VMEM_LIMIT = 56 * 1024 * 1024


def _cparams(n_axes, vmem=VMEM_LIMIT):
    return pltpu.CompilerParams(
        dimension_semantics=("arbitrary",) * n_axes, vmem_limit_bytes=vmem)


def _dot(a, b):
    return jnp.dot(a, b, preferred_element_type=F32)


def _slab(ref, token, n_tokens, width):
    first = pl.multiple_of(token * width, width)
    return ref.at[pl.ds(first, n_tokens * width), :]


def _dot_nt(a, b):
    return lax.dot_general(a, b, (((1,), (1,)), ((), ())), preferred_element_type=F32)


def _inproj_kernel(x_ref, g_ref, w_ref, cos_ref, slo_ref, shi_ref, qg_ref, ksg_ref, kwg_ref,
                   qn_ref, kvc_ref, ks_ref, vst_ref, kw_ref, vwt_ref, gt_ref, pool_ref, *, nS):
    x = x_ref[...]
    y = x * lax.rsqrt(jnp.mean(x * x, axis=-1, keepdims=True) + RMS_EPS)
    h = (y * g_ref[...]).astype(BF16)
    cos, slo, shi = cos_ref[...], slo_ref[...], shi_ref[...]
    G = NSA_KV_HEADS
    head = lambda z, c: z[:, c * HEAD_DIM:(c + 1) * HEAD_DIM]

    scale = HEAD_DIM ** -0.5 * LOG2_E
    zq = _dot(h, w_ref[:, 0:NSA_WIDTH])
    for c in range(NSA_HEADS):
        qn_ref[:, c * HEAD_DIM:(c + 1) * HEAD_DIM] = (
            _norm_rope(head(zq, c), qg_ref[...], cos, slo, shi) * scale).astype(BF16)

    pair = lambda p: _dot(h, w_ref[:, NSA_WIDTH + p * G * HEAD_DIM:NSA_WIDTH + (p + 1) * G * HEAD_DIM])
    zkc, zvc = pair(0), pair(1)
    for g in range(G):
        kvc_ref[g] = head(zkc, g)
        kvc_ref[G + g] = head(zvc, g)
    zks, zvs, zkw, zvw = pair(2), pair(3), pair(4), pair(5)
    row = lax.rem(pl.program_id(0), nS) * TM_IN + lax.broadcasted_iota(I32, (TM_IN, N_BLK_PAD), 0)
    lane = lax.broadcasted_iota(I32, (TM_IN, N_BLK_PAD), 1)
    onehot = jnp.where((row >> SLC_SHIFT) == lane, 1.0, 0.0).astype(BF16)
    for g in range(G):
        ks_ref[g, :, 0:HEAD_DIM] = _norm_rope(head(zks, g), ksg_ref[...], cos, slo, shi).astype(BF16)
        ks_ref[g, :, HEAD_DIM:HEAD_DIM + N_BLK_PAD] = onehot
        vst_ref[g, 0] = head(zvs, g).T.astype(BF16)
        kw_ref[g] = _norm_rope(head(zkw, g), kwg_ref[...], cos, slo, shi).astype(BF16)
        vwt_ref[g, 0] = head(zvw, g).T.astype(BF16)

    base = NSA_WIDTH + N_KV_COLS * HEAD_DIM
    pool_ref[...] = _dot(h, w_ref[:, base:base + POOL_WIDTH])
    sig_t = jax.nn.sigmoid(_dot(h, w_ref[:, base + POOL_WIDTH:base + POOL_WIDTH + LANES])).T
    for g in range(G):
        gt_ref[g] = sig_t[g * GATE_PAD:(g + 1) * GATE_PAD]


def _inproj(x2, g1, w_perm, cos, slo, shi, qg, ksg, kwg, B, S):
    T = x2.shape[0]
    n_cols = w_perm.shape[1]
    nS = S // TM_IN
    G = NSA_KV_HEADS
    full = lambda shape: pl.BlockSpec(shape, lambda i: (0,) * len(shape))
    tab = pl.BlockSpec((TM_IN, LANES), lambda i: (lax.rem(i, nS), 0))
    bi = lambda i: (i // nS, lax.rem(i, nS))
    return pl.pallas_call(
        functools.partial(_inproj_kernel, nS=nS),
        grid=(T // TM_IN,),
        in_specs=[
            pl.BlockSpec((TM_IN, D_MODEL), lambda i: (i, 0)),
            full((1, D_MODEL)),
            full((D_MODEL, n_cols)),
            tab, tab, tab,
            full((1, HEAD_DIM)), full((1, HEAD_DIM)), full((1, HEAD_DIM)),
        ],
        out_specs=[
            pl.BlockSpec((TM_IN, NSA_WIDTH), lambda i: (i, 0)),
            pl.BlockSpec((2 * G, TM_IN, LANES), lambda i: (0, i, 0)),
            pl.BlockSpec((G, TM_IN, 2 * HEAD_DIM), lambda i: (*bi(i), 0)),
            pl.BlockSpec((G, 1, HEAD_DIM, TM_IN), lambda i: (*bi(i), 0, 0)),
            pl.BlockSpec((G, TM_IN, HEAD_DIM), lambda i: (*bi(i), 0)),
            pl.BlockSpec((G, 1, HEAD_DIM, TM_IN), lambda i: (*bi(i), 0, 0)),
            pl.BlockSpec((G, GATE_PAD, TM_IN), lambda i: (i // nS, 0, lax.rem(i, nS))),
            pl.BlockSpec((TM_IN, POOL_WIDTH), lambda i: (i, 0)),
        ],
        out_shape=[
            jax.ShapeDtypeStruct((T, NSA_WIDTH), BF16),
            jax.ShapeDtypeStruct((2 * G, T, LANES), F32),
            jax.ShapeDtypeStruct((B * G, S, 2 * HEAD_DIM), BF16),
            jax.ShapeDtypeStruct((B * G, nS, HEAD_DIM, TM_IN), BF16),
            jax.ShapeDtypeStruct((B * G, S, HEAD_DIM), BF16),
            jax.ShapeDtypeStruct((B * G, nS, HEAD_DIM, TM_IN), BF16),
            jax.ShapeDtypeStruct((B * G, GATE_PAD, S), F32),
            jax.ShapeDtypeStruct((T, POOL_WIDTH), F32),
        ],
        compiler_params=_cparams(1),
        name="inproj",
    )(x2, g1, w_perm, cos, slo, shi, qg, ksg, kwg)


def _norm_rope(x, g, cos, sin_lo, sin_hi):
    half = ROT_DIM // 2
    y = x * lax.rsqrt(jnp.mean(x * x, axis=-1, keepdims=True) + QK_EPS) * g
    return (y * cos + pltpu.roll(y, LANES - half, 1) * sin_lo
            + pltpu.roll(y, half, 1) * sin_hi)


def _compress_body(x_ref, pos_ref, w1_ref, b1_ref, w2_ref, b2_ref):
    nc = x_ref.shape[1] // CMP_STRIDE
    pair_w = 2 * HEAD_DIM
    first = jnp.zeros((nc, CMP_HIDDEN), F32)
    second = jnp.zeros((nc, CMP_HIDDEN), F32)
    for p in range(CMP_STRIDE // 2):
        x = jnp.concatenate([x_ref[0, pl.ds(2 * p + u, nc, stride=CMP_STRIDE), :] for u in range(2)], axis=1)
        cols = slice(p * pair_w, (p + 1) * pair_w)
        first = first + _dot((x + pos_ref[0:1, cols]).astype(BF16), w1_ref[0, cols, :])
        second = second + _dot((x + pos_ref[1:2, cols]).astype(BF16), w1_ref[1, cols, :])
    nxt = pltpu.roll(second, nc - 1, 0)
    hid = jax.nn.gelu(first + nxt + b1_ref[...])
    return _dot(hid.astype(BF16), w2_ref[...]) + b2_ref[...]


def _compress_k_kernel(x_ref, pos_ref, w1_ref, b1_ref, w2_ref, b2_ref,
                       g_ref, cos_ref, slo_ref, shi_ref, o_ref):
    out = _compress_body(x_ref, pos_ref, w1_ref, b1_ref, w2_ref, b2_ref)
    o_ref[0] = _norm_rope(out, g_ref[...], cos_ref[...], slo_ref[...], shi_ref[...]).astype(BF16)


def _compress_v_kernel(x_ref, pos_ref, w1_ref, b1_ref, w2_ref, b2_ref, o_ref):
    out = _compress_body(x_ref, pos_ref, w1_ref, b1_ref, w2_ref, b2_ref)
    o_ref[0] = out.T.astype(BF16)


def _compress(kvc, which, pos2, w1, b1, w2, b2, B, S, rope=None):
    NC = S // CMP_STRIDE
    G = NSA_KV_HEADS
    CW = CMP_STRIDE * HEAD_DIM
    common_specs = [
        pl.BlockSpec((1, S, HEAD_DIM), lambda b, g: (which * G + g, b, 0)),
        pl.BlockSpec((2, CW), lambda b, g: (0, 0)),
        pl.BlockSpec((2, CW, CMP_HIDDEN), lambda b, g: (0, 0, 0)),
        pl.BlockSpec((1, CMP_HIDDEN), lambda b, g: (0, 0)),
        pl.BlockSpec((CMP_HIDDEN, HEAD_DIM), lambda b, g: (0, 0)),
        pl.BlockSpec((1, HEAD_DIM), lambda b, g: (0, 0)),
    ]
    if rope is not None:
        gk, cos, slo, shi = rope
        tab = pl.BlockSpec((NC, LANES), lambda b, g: (0, 0))
        return pl.pallas_call(
            _compress_k_kernel,
            grid=(B, G),
            in_specs=common_specs + [pl.BlockSpec((1, HEAD_DIM), lambda b, g: (0, 0)), tab, tab, tab],
            out_specs=pl.BlockSpec((1, NC, HEAD_DIM), lambda b, g: (b * G + g, 0, 0)),
            out_shape=jax.ShapeDtypeStruct((B * G, NC, HEAD_DIM), BF16),
            compiler_params=_cparams(2),
            name="compress_k",
        )(kvc, pos2, w1, b1, w2, b2, gk, cos, slo, shi)
    return pl.pallas_call(
        _compress_v_kernel,
        grid=(B, G),
        in_specs=common_specs,
        out_specs=pl.BlockSpec((1, HEAD_DIM, NC), lambda b, g: (b * G + g, 0, 0)),
        out_shape=jax.ShapeDtypeStruct((B * G, HEAD_DIM, NC), BF16),
        compiler_params=_cparams(2),
        name="compress_v",
    )(kvc, pos2, w1, b1, w2, b2)


BIAS_DIAG, BIAS_FAR, BIAS_ALL, BIAS_NONE = 0, 1, 2, 3


def _tile_bias_table():
    k = lax.broadcasted_iota(I32, (TK, TQ), 0)
    t = lax.broadcasted_iota(I32, (TK, TQ), 1)
    neg = jnp.full((TK, TQ), MASK_VALUE, F32)
    zero = jnp.zeros((TK, TQ), F32)
    return jnp.stack([jnp.where(k <= t, zero, neg), jnp.where(k > t, zero, neg), neg, zero])


def _add_tile_bias(s, b):
    return jnp.concatenate([s[:, r * TQ:(r + 1) * TQ] + b for r in range(HEADS_PER_GROUP)], axis=1)


def _nsa_kernel(q_ref, kc_ref, vct_ref, ks_ref, vst_ref, kw_ref, vwt_ref, gt_ref, bias_ref, o_ref,
                qaug_ref, s_ref, cm_ref, m_ref, l_ref, acc_ref, oc_ref, ow_ref, *, S):
    R = HEADS_PER_GROUP
    NQ = R * TQ
    NC = S // CMP_STRIDE
    qi = pl.program_id(2)
    qs = qi * TQ

    q = q_ref[...].astype(F32)
    q_t = jnp.concatenate([q[:, r * HEAD_DIM:(r + 1) * HEAD_DIM].T for r in range(R)], axis=1).astype(BF16)
    col = lax.broadcasted_iota(I32, (1, NQ), 1)
    t_row = qs + (col & (TQ - 1))

    s_c = _dot(kc_ref[0], q_t)
    c_end = lax.broadcasted_iota(I32, (NC, 1), 0) * CMP_STRIDE + (CMP_BLK - 1)
    s_c = jnp.where(c_end <= t_row, s_c, MASK_VALUE)
    e_c = jnp.exp2(s_c - jnp.max(s_c, axis=0, keepdims=True))
    l_c = jnp.sum(e_c, axis=0, keepdims=True)
    p_c = e_c * jnp.where(t_row >= CMP_BLK - 1, 1.0 / l_c, 0.0)
    oc_ref[...] = _dot(vct_ref[0], p_c.astype(BF16))

    imp = p_c[:, 0:TQ]
    for r in range(1, R):
        imp = imp + p_c[:, r * TQ:(r + 1) * TQ]
    jj = lax.broadcasted_iota(I32, (N_BLK_PAD, NC), 0) * (SLC_BLK // CMP_STRIDE)
    nn = lax.broadcasted_iota(I32, (N_BLK_PAD, NC), 1)
    per = SLC_BLK // CMP_STRIDE
    fold = (jnp.where((nn >= jj) & (nn < jj + per), 1.0, 0.0)
            + jnp.where((nn >= jj - 1) & (nn < jj + per - 1), 1.0, 0.0)).astype(BF16)
    imp_hi = imp.astype(BF16)
    imp_lo = (imp - imp_hi.astype(F32)).astype(BF16)
    blk_score = _dot(fold, imp_hi) + _dot(fold, imp_lo)

    jb = lax.broadcasted_iota(I32, (N_BLK_PAD, TQ), 0)
    cur = (qs + lax.broadcasted_iota(I32, (N_BLK_PAD, TQ), 1)) >> SLC_SHIFT
    forced = (jb == 0) | (jb == cur) | (jb == cur - 1)
    val = jnp.where(forced, FORCE_VALUE, jnp.where(jb <= cur, blk_score, MASK_VALUE))
    jbf = jb.astype(F32)
    bias = jnp.where(forced, 0.0, MASK_VALUE)
    val = jnp.where(forced, -jnp.inf, val)
    for _ in range(min(N_SEL, S // SLC_BLK) - 3):
        mx = jnp.max(val, axis=0, keepdims=True)
        first = jnp.min(jnp.where(val == mx, jbf, float(N_BLK_PAD)), axis=0, keepdims=True)
        pick = jbf == first
        bias = jnp.where(pick, 0.0, bias)
        val = jnp.where(pick, -jnp.inf, val)
    qaug_ref[0:HEAD_DIM, :] = q_t
    qaug_ref[HEAD_DIM:HEAD_DIM + N_BLK_PAD, :] = jnp.concatenate([bias.astype(BF16)] * R, axis=1)

    w_tiles = ((jnp.maximum(qi - 2, 0), jnp.where(qi >= 2, BIAS_FAR, BIAS_ALL)),
               (jnp.maximum(qi - 1, 0), jnp.where(qi >= 1, BIAS_NONE, BIAS_ALL)),
               (qi, BIAS_DIAG))
    s_w = []
    for kt, bi in w_tiles:
        kwt = kw_ref[0, pl.ds(pl.multiple_of(kt * TK, TK), TK), :]
        s_w.append(_add_tile_bias(_dot(kwt, q_t), bias_ref[bi]))
    m_w = jnp.max(s_w[0], axis=0, keepdims=True)
    for s in s_w[1:]:
        m_w = jnp.maximum(m_w, jnp.max(s, axis=0, keepdims=True))
    l_w = jnp.zeros((1, NQ), F32)
    o_w = jnp.zeros((HEAD_DIM, NQ), F32)
    for (kt, _), s in zip(w_tiles, s_w):
        e = jnp.exp2(s - m_w)
        l_w = l_w + jnp.sum(e, axis=0, keepdims=True)
        o_w = o_w + _dot(vwt_ref[0, kt], e.astype(BF16))
    ow_ref[...] = o_w * (1.0 / l_w)

    m_ref[...] = jnp.full((1, NQ), MASK_VALUE, F32)
    l_ref[...] = jnp.zeros((1, NQ), F32)
    acc_ref[...] = jnp.zeros((HEAD_DIM, NQ), F32)

    def produce(kj, slot, causal):
        k = ks_ref[0, pl.ds(pl.multiple_of(kj * TK, TK), TK), :]
        s = _dot(k, qaug_ref[...])
        if causal:
            s = _add_tile_bias(s, bias_ref[BIAS_DIAG])
        s_ref[slot] = s
        cm_ref[slot] = jnp.max(s, axis=0, keepdims=True)

    def consume(kj, slot):
        m_old = m_ref[...]
        m_new = jnp.maximum(m_old, cm_ref[slot])
        alpha = jnp.exp2(m_old - m_new)
        p = jnp.exp2(s_ref[slot] - m_new)
        l_ref[...] = alpha * l_ref[...] + jnp.sum(p, axis=0, keepdims=True)
        acc_ref[...] = alpha * acc_ref[...] + _dot(vst_ref[0, kj], p.astype(BF16))
        m_ref[...] = m_new

    def stage(kj, slot, causal_next):
        produce(kj + 1, 1 - slot, causal_next)
        consume(kj, slot)

    @pl.when(qi == 0)
    def _():
        produce(0, 0, True)

    @pl.when(qi > 0)
    def _():
        produce(0, 0, False)

    def stage_group(pp, carry):
        for u in range(SEL_UNROLL):
            stage(SEL_UNROLL * pp + u, u & 1, False)
        return carry

    n_plain = jnp.maximum(qi - 1, 0)
    n_grouped = n_plain // SEL_UNROLL * SEL_UNROLL
    lax.fori_loop(0, n_plain // SEL_UNROLL, stage_group, 0)
    for u in range(SEL_UNROLL - 1):
        @pl.when(n_plain - n_grouped > u)
        def _(u=u):
            stage(n_grouped + u, u & 1, False)

    @pl.when((qi >= 1) & ((n_plain & 1) == 0))
    def _():
        stage(qi - 1, 0, True)
        consume(qi, 1)

    @pl.when((qi >= 1) & ((n_plain & 1) == 1))
    def _():
        stage(qi - 1, 1, True)
        consume(qi, 0)

    @pl.when(qi == 0)
    def _():
        consume(0, 0)

    inv_l = 1.0 / l_ref[...]

    for r in range(R):
        sl = slice(r * TQ, (r + 1) * TQ)
        g0 = gt_ref[0, r * N_BRANCH + 0:r * N_BRANCH + 1, :]
        g1 = gt_ref[0, r * N_BRANCH + 1:r * N_BRANCH + 2, :]
        g2 = gt_ref[0, r * N_BRANCH + 2:r * N_BRANCH + 3, :]
        o = (g0 * oc_ref[:, sl] + (g1 * inv_l[:, sl]) * acc_ref[:, sl]
             + g2 * ow_ref[:, sl])
        o_ref[:, r * HEAD_DIM:(r + 1) * HEAD_DIM] = o.T.astype(BF16)


def _nsa(qn, kc, vct, ks, vst, kw, vwt, gt, B, S):
    G = NSA_KV_HEADS
    R = HEADS_PER_GROUP
    nQ = S // TQ
    nK = S // TK
    NC = S // CMP_STRIDE
    bg = lambda b, g, i: b * G + g
    return pl.pallas_call(
        functools.partial(_nsa_kernel, S=S),
        grid=(B, G, nQ),
        in_specs=[
            pl.BlockSpec((TQ, R * HEAD_DIM), lambda b, g, i: (b * nQ + i, g)),
            pl.BlockSpec((1, NC, HEAD_DIM), lambda b, g, i: (bg(b, g, i), 0, 0)),
            pl.BlockSpec((1, HEAD_DIM, NC), lambda b, g, i: (bg(b, g, i), 0, 0)),
            pl.BlockSpec((1, S, 2 * HEAD_DIM), lambda b, g, i: (bg(b, g, i), 0, 0)),
            pl.BlockSpec((1, nK, HEAD_DIM, TK), lambda b, g, i: (bg(b, g, i), 0, 0, 0)),
            pl.BlockSpec((1, S, HEAD_DIM), lambda b, g, i: (bg(b, g, i), 0, 0)),
            pl.BlockSpec((1, nK, HEAD_DIM, TK), lambda b, g, i: (bg(b, g, i), 0, 0, 0)),
            pl.BlockSpec((1, GATE_PAD, TQ), lambda b, g, i: (bg(b, g, i), 0, i)),
            pl.BlockSpec((4, TK, TQ), lambda b, g, i: (0, 0, 0)),
        ],
        out_specs=pl.BlockSpec((TQ, R * HEAD_DIM), lambda b, g, i: (b * nQ + i, g)),
        out_shape=jax.ShapeDtypeStruct((B * S, NSA_WIDTH), BF16),
        scratch_shapes=[
            pltpu.VMEM((2 * HEAD_DIM, R * TQ), BF16),
            pltpu.VMEM((2, TK, R * TQ), F32),
            pltpu.VMEM((2, 1, R * TQ), F32),
            pltpu.VMEM((1, R * TQ), F32),
            pltpu.VMEM((1, R * TQ), F32),
            pltpu.VMEM((HEAD_DIM, R * TQ), F32),
            pltpu.VMEM((HEAD_DIM, R * TQ), F32),
            pltpu.VMEM((HEAD_DIM, R * TQ), F32),
        ],
        compiler_params=_cparams(3),
        name="nsa",
    )(qn, kc, vct, ks, vst, kw, vwt, gt, _tile_bias_table())


def _mix_kernel(nsa_ref, pool_ref, halo_ref, x_ref, wo_ref, pw_ref, ps_ref, g2_ref,
                wr2_ref, br_ref, x1_ref, h2p_ref, logit_ref, ext_ref, *, S):
    nS = S // TM_MIX
    i = pl.program_id(0)
    t0 = lax.rem(i, nS) * TM_MIX
    ext_ref[0:POOL_HALO, :] = jnp.where(t0 == 0, 0.0, halo_ref[...])
    ext_ref[POOL_HALO:POOL_HALO + TM_MIX, :] = pool_ref[...]
    for sb in range(TM_MIX // MIX_SUB):
        r0 = sb * MIX_SUB
        rows = slice(r0, r0 + MIX_SUB)
        t = t0 + r0 + lax.broadcasted_iota(I32, (MIX_SUB, POOL_GC), 0)
        acc = _dot(nsa_ref[rows, :], wo_ref[0:NSA_WIDTH, :])
        for gi, w in enumerate(POOL_WINDOWS):
            cs = slice(gi * POOL_GC, (gi + 1) * POOL_GC)
            v = pool_ref[rows, cs]
            tot = v
            for k in range(1, w):
                tot = tot + ext_ref[POOL_HALO + r0 - k:POOL_HALO + r0 - k + MIX_SUB, cs]
            cnt = jnp.minimum(t + 1, w).astype(F32)
            d = tot / cnt - v
            y = _dot(d.astype(BF16), pw_ref[gi]) * ps_ref[:, cs]
            acc = acc + _dot(y.astype(BF16),
                             wo_ref[NSA_WIDTH + gi * POOL_GC:NSA_WIDTH + (gi + 1) * POOL_GC, :])
        x1 = x_ref[rows, :] + acc
        x1_ref[rows, :] = x1
        h2 = x1 * lax.rsqrt(jnp.mean(x1 * x1, axis=-1, keepdims=True) + RMS_EPS) * g2_ref[...]
        half = D_MODEL // 2
        packed = pltpu.pack_elementwise([h2[:, 0:half], h2[:, half:D_MODEL]], packed_dtype=BF16)
        for c in range(XW):
            h2p_ref[pl.ds(r0 * XW + c, MIX_SUB, stride=XW), :] = packed[:, c * LANES:(c + 1) * LANES]
        hi = h2.astype(BF16)
        lo = (h2 - hi.astype(F32)).astype(BF16)
        hl = _dot(hi, wr2_ref[...])
        logit_ref[rows, :] = (hl[:, 0:LANES] + hl[:, LANES:2 * LANES] + _dot(lo, wr2_ref[:, 0:LANES])
                              + br_ref[...])


def _mix(nsa_out, pool_in, x2, wo, pw, ps, g2, wr2, br, S):
    T = x2.shape[0]
    hb = TM_MIX // POOL_HALO
    full = lambda shape: pl.BlockSpec(shape, lambda i: (0,) * len(shape))
    return pl.pallas_call(
        functools.partial(_mix_kernel, S=S),
        grid=(T // TM_MIX,),
        in_specs=[
            pl.BlockSpec((TM_MIX, NSA_WIDTH), lambda i: (i, 0)),
            pl.BlockSpec((TM_MIX, POOL_WIDTH), lambda i: (i, 0)),
            pl.BlockSpec((POOL_HALO, POOL_WIDTH), lambda i: (jnp.maximum(i * hb - 1, 0), 0)),
            pl.BlockSpec((TM_MIX, D_MODEL), lambda i: (i, 0)),
            full((D_MODEL, D_MODEL)),
            full((len(POOL_WINDOWS), POOL_GC, POOL_GC)),
            full((1, POOL_WIDTH)),
            full((1, D_MODEL)),
            full((D_MODEL, 2 * LANES)),
            full((1, LANES)),
        ],
        out_specs=[
            pl.BlockSpec((TM_MIX, D_MODEL), lambda i: (i, 0)),
            pl.BlockSpec((TM_MIX * XW, LANES), lambda i: (i, 0)),
            pl.BlockSpec((TM_MIX, LANES), lambda i: (i, 0)),
        ],
        out_shape=[
            jax.ShapeDtypeStruct((T, D_MODEL), F32),
            jax.ShapeDtypeStruct((T * XW, LANES), U32),
            jax.ShapeDtypeStruct((T, LANES), F32),
        ],
        scratch_shapes=[pltpu.VMEM((POOL_HALO + TM_MIX, POOL_WIDTH), F32)],
        compiler_params=_cparams(1),
        name="mix",
    )(nsa_out, pool_in, pool_in, x2, wo, pw, ps, g2, wr2, br)


def _route_kernel(logit_ref, dest_ref, wt_ref, meta_ref, cnt_ref, run_ref, tri_ref, *, n_tiles):
    phase = pl.program_id(0)
    i = pl.program_id(1)
    TT = TT_ROUTE
    E = N_EXPERTS

    @pl.when((phase == 0) & (i == 0))
    def _():
        cnt_ref[...] = jnp.zeros_like(cnt_ref)
        run_ref[...] = jnp.zeros_like(run_ref)

    lt = logit_ref[...].T[0:E, :]
    ef = lax.broadcasted_iota(I32, (E, TT), 0).astype(F32)
    work = lt
    ids, vals, hots = [], [], []
    for _ in range(TOP_K):
        mx = jnp.max(work, axis=0, keepdims=True)
        eid = jnp.min(jnp.where(work == mx, ef, float(E)), axis=0, keepdims=True)
        hot = ef == eid
        ids.append(eid)
        vals.append(mx)
        hots.append(hot)
        work = jnp.where(hot, -jnp.inf, work)
    member = jnp.where(hots[0] | hots[1] | hots[2] | hots[3], 1.0, 0.0)
    tile_cnt = jnp.sum(member, axis=1, keepdims=True)

    @pl.when(phase == 0)
    def _():
        cnt_ref[...] = cnt_ref[...] + tile_cnt

    @pl.when(phase == 1)
    def _():
        ex = [jnp.exp(v - vals[0]) for v in vals]
        den = ex[0] + ex[1] + ex[2] + ex[3]
        cnt = jnp.broadcast_to(cnt_ref[...], (E, LANES))
        padded = jnp.ceil(cnt * (1.0 / TM_G)) * TM_G
        erow = lax.broadcasted_iota(I32, (E, LANES), 0)
        start = jnp.zeros((E, LANES), F32)
        running = jnp.zeros((1, LANES), F32)
        for e in range(E):
            start = jnp.where(erow == e, running, start)
            running = running + padded[e:e + 1, :]
        end = start + padded
        @pl.when(i == 0)
        def _():
            tri_ref[...] = jnp.where(lax.broadcasted_iota(I32, (TT, TT), 0)
                                     < lax.broadcasted_iota(I32, (TT, TT), 1), 1.0, 0.0).astype(BF16)

        before = _dot(member.astype(BF16), tri_ref[...]) + run_ref[...]
        slot = before + start[:, 0:1]
        for k in range(TOP_K):
            d = jnp.sum(jnp.where(hots[k], slot, 0.0), axis=0, keepdims=True)
            dest_ref[k:k + 1, :] = d.astype(I32)
            wt_ref[k:k + 1, :] = ex[k] / den
        run_ref[...] = run_ref[...] + tile_cnt

        @pl.when(i == 0)
        def _():
            lanes = meta_ref.shape[1]
            endw = jnp.broadcast_to(end[:, 0:1], (E, lanes))
            tile_row = (lax.broadcasted_iota(I32, (E, lanes), 1) * TM_G).astype(F32)
            owner = jnp.minimum(jnp.sum(jnp.where(endw <= tile_row, 1.0, 0.0), axis=0, keepdims=True),
                                float(E - 1))
            nxt = jnp.full((E, LANES), -1.0, F32)
            later = jnp.full((1, LANES), -1.0, F32)
            for e in reversed(range(E)):
                nxt = jnp.where(erow == e, later, nxt)
                later = jnp.where(cnt[e:e + 1, :] > 0.0, float(e), later)
            erow_w = lax.broadcasted_iota(I32, (E, lanes), 0).astype(F32)
            nxt_tile = jnp.sum(jnp.where(erow_w == owner, jnp.broadcast_to(nxt[:, 0:1], (E, lanes)), 0.0),
                               axis=0, keepdims=True)
            meta_ref[0:1, :] = owner.astype(I32)
            meta_ref[1:2, :] = jnp.where(tile_row[0:1, :] < running[:, 0:1], 1, 0).astype(I32)
            meta_ref[2:3, :] = nxt_tile.astype(I32)
            lane_w = lax.broadcasted_iota(I32, (E, lanes), 1).astype(F32)
            ends = jnp.sum(jnp.where(erow_w == lane_w, endw, 0.0), axis=0, keepdims=True)
            ends = jnp.where(lane_w[0:1, :] == float(E), running[:, 0:1] * (1.0 / TM_G), ends)
            meta_ref[3:4, :] = ends.astype(I32)
            meta_ref[4:8, :] = jnp.zeros((4, lanes), I32)


def _route(logits, n_tiles):
    T = logits.shape[0]
    nT = T // TT_ROUTE
    lanes = -(-n_tiles // LANES) * LANES
    return pl.pallas_call(
        functools.partial(_route_kernel, n_tiles=n_tiles),
        grid=(2, nT),
        in_specs=[pl.BlockSpec((TT_ROUTE, LANES), lambda p, i: (i, 0))],
        out_specs=[
            pl.BlockSpec((TOP_K, TT_ROUTE), lambda p, i: (0, i * p)),
            pl.BlockSpec((TOP_K, TT_ROUTE), lambda p, i: (0, i * p)),
            pl.BlockSpec((8, lanes), lambda p, i: (0, 0)),
        ],
        out_shape=[
            jax.ShapeDtypeStruct((TOP_K, T), I32),
            jax.ShapeDtypeStruct((TOP_K, T), F32),
            jax.ShapeDtypeStruct((8, lanes), I32),
        ],
        scratch_shapes=[pltpu.VMEM((N_EXPERTS, 1), F32), pltpu.VMEM((N_EXPERTS, 1), F32),
                        pltpu.VMEM((TT_ROUTE, TT_ROUTE), BF16)],
        compiler_params=_cparams(2),
        name="route",
    )(logits)


def _dispatch_kernel(ends_ref, dest_ref, h_ref, xs_ref, zero_ref, sem, zsem):
    @pl.when(pl.program_id(0) == 0)
    def _():
        zero_ref[...] = jnp.zeros_like(zero_ref)

        def tail_copy(e):
            return pltpu.make_async_copy(zero_ref, _slab(xs_ref, ends_ref[e] - TM_G, TM_G, XW), zsem)

        def nonempty(e):
            return ends_ref[e] > (ends_ref[e - 1] if e else 0)

        for e in range(N_EXPERTS):
            @pl.when(nonempty(e))
            def _(e=e):
                tail_copy(e).start()
        for e in range(N_EXPERTS):
            @pl.when(nonempty(e))
            def _(e=e):
                tail_copy(e).wait()

        def unused_copy(tile):
            return pltpu.make_async_copy(zero_ref, _slab(xs_ref, tile * TM_G, TM_G, XW), zsem)

        n_used = ends_ref[N_EXPERTS]
        n_all = xs_ref.shape[0] // (TM_G * XW)

        def start_unused(tile, carry):
            unused_copy(tile).start()
            return carry

        def wait_unused(tile, carry):
            unused_copy(tile).wait()
            return carry

        lax.fori_loop(n_used, n_all, start_unused, 0)
        lax.fori_loop(n_used, n_all, wait_unused, 0)

    def issue(t, carry):
        for k in range(TOP_K):
            pltpu.make_async_copy(_slab(h_ref, t, 1, XW), _slab(xs_ref, dest_ref[k, t], 1, XW),
                                  sem).start(priority=k % 2)
        return carry

    lax.fori_loop(0, TT_DISP, issue, 0)
    all_rows = _slab(xs_ref, 0, TOP_K * TT_DISP, XW)
    pltpu.make_async_copy(all_rows, all_rows, sem).wait()


def _dispatch(ends, dest, h2p, n_rows):
    T = h2p.shape[0] // XW
    grid_spec = pltpu.PrefetchScalarGridSpec(
        num_scalar_prefetch=1,
        grid=(T // TT_DISP,),
        in_specs=[
            pl.BlockSpec((TOP_K, TT_DISP), lambda i, ends: (0, i), memory_space=pltpu.SMEM),
            pl.BlockSpec((TT_DISP * XW, LANES), lambda i, ends: (i, 0)),
        ],
        out_specs=pl.BlockSpec(memory_space=pl.ANY),
        scratch_shapes=[pltpu.VMEM((TM_G * XW, LANES), U32), pltpu.SemaphoreType.DMA,
                        pltpu.SemaphoreType.DMA],
    )
    return pl.pallas_call(
        _dispatch_kernel,
        grid_spec=grid_spec,
        out_shape=jax.ShapeDtypeStruct((n_rows * XW, LANES), U32),
        compiler_params=_cparams(1),
        name="dispatch",
    )(ends, dest, h2p)


def _expert_changed(te_ref, i):
    return (i == 0) | (te_ref[i] != te_ref[jnp.maximum(i - 1, 0)])


WEIGHT_DMA_PRIORITY = 1


def _stream_expert_weights(te_ref, tv_ref, nx_ref, copies, wst_ref, wbf_ref, slot_ref):
    j = pl.program_id(0)
    i = pl.program_id(1)

    @pl.when((j == 0) & (i == 0))
    def _():
        slot_ref[0] = 0
        for c in copies(te_ref[0], 0, 0):
            c.start(priority=WEIGHT_DMA_PRIORITY)

    @pl.when((tv_ref[i] > 0) & _expert_changed(te_ref, i))
    def _():
        slot = slot_ref[0]
        e_next = nx_ref[i]

        @pl.when(e_next >= 0)
        def _():
            for c in copies(e_next, j, 1 - slot):
                c.start(priority=WEIGHT_DMA_PRIORITY)

        @pl.when((e_next < 0) & (j + 1 < pl.num_programs(0)))
        def _():
            for c in copies(te_ref[0], j + 1, 1 - slot):
                c.start(priority=WEIGHT_DMA_PRIORITY)

        for c in copies(te_ref[i], j, slot):
            c.wait()
        wbf_ref[...] = wst_ref[slot].astype(BF16)
        slot_ref[0] = 1 - slot


def _gemm1_kernel(te_ref, tv_ref, nx_ref, x_ref, bg_ref, bu_ref, w_hbm, act_ref,
                  wst_ref, wbf_ref, slot_ref, sem):
    i = pl.program_id(1)
    valid = tv_ref[i] > 0

    def copies(e, j, slot):
        col = pl.multiple_of(j * TN_G1, TN_G1)
        return (pltpu.make_async_copy(w_hbm.at[e, :, pl.ds(col, TN_G1)],
                                      wst_ref.at[slot, :, pl.ds(0, TN_G1)], sem.at[slot]),
                pltpu.make_async_copy(w_hbm.at[e, :, pl.ds(D_FF + col, TN_G1)],
                                      wst_ref.at[slot, :, pl.ds(TN_G1, TN_G1)], sem.at[slot]))

    _stream_expert_weights(te_ref, tv_ref, nx_ref, copies, wst_ref, wbf_ref, slot_ref)

    @pl.when(valid)
    def _():
        half = D_MODEL // 2
        xp = jnp.concatenate([x_ref[pl.ds(c, TM_G, stride=XW), :] for c in range(XW)], axis=1)
        lo = pltpu.unpack_elementwise(xp, index=0, packed_dtype=BF16, unpacked_dtype=F32).astype(BF16)
        hi = pltpu.unpack_elementwise(xp, index=1, packed_dtype=BF16, unpacked_dtype=F32).astype(BF16)
        gu = _dot(lo, wbf_ref[0:half, :]) + _dot(hi, wbf_ref[half:D_MODEL, :])
        gate = jnp.minimum(gu[:, 0:TN_G1] + bg_ref[0], SWIGLU_LIMIT)
        up = jnp.clip(gu[:, TN_G1:2 * TN_G1] + bu_ref[0], -SWIGLU_LIMIT, SWIGLU_LIMIT)
        act = (up + 1.0) * gate * jax.nn.sigmoid(SWIGLU_ALPHA * gate)
        act_ref[...] = act.astype(BF16)

    @pl.when(jnp.logical_not(valid))
    def _():
        act_ref[...] = jnp.zeros_like(act_ref)


def _gemm1(te, tv, nx, xs, w_gu, b_gu3, n_tiles):
    n_rows = xs.shape[0] // XW
    nJ = D_FF // TN_G1
    grid_spec = pltpu.PrefetchScalarGridSpec(
        num_scalar_prefetch=3,
        grid=(nJ, n_tiles),
        in_specs=[
            pl.BlockSpec((TM_G * XW, LANES), lambda j, i, te, tv, nx: (i, 0)),
            pl.BlockSpec((1, 1, TN_G1), lambda j, i, te, tv, nx: (te[i], 0, j)),
            pl.BlockSpec((1, 1, TN_G1), lambda j, i, te, tv, nx: (te[i], 0, nJ + j)),
            pl.BlockSpec(memory_space=pl.ANY),
        ],
        out_specs=pl.BlockSpec((TM_G, TN_G1), lambda j, i, te, tv, nx: (i, j)),
        scratch_shapes=[
            pltpu.VMEM((2, D_MODEL, 2 * TN_G1), F32),
            pltpu.VMEM((D_MODEL, 2 * TN_G1), BF16),
            pltpu.SMEM((1,), I32),
            pltpu.SemaphoreType.DMA((2,)),
        ],
    )
    return pl.pallas_call(
        _gemm1_kernel,
        grid_spec=grid_spec,
        out_shape=jax.ShapeDtypeStruct((n_rows, D_FF), BF16),
        compiler_params=_cparams(2),
        name="gemm1",
    )(te, tv, nx, xs, b_gu3, b_gu3, w_gu)


def _gemm2_kernel(te_ref, tv_ref, nx_ref, a_ref, b_ref, w_hbm, y_ref, wst_ref, wbf_ref, slot_ref, sem):
    i = pl.program_id(1)
    valid = tv_ref[i] > 0

    def copies(e, j, slot):
        col = pl.multiple_of(j * TN_G2, TN_G2)
        return (pltpu.make_async_copy(w_hbm.at[e, :, pl.ds(col, TN_G2)], wst_ref.at[slot], sem.at[slot]),)

    _stream_expert_weights(te_ref, tv_ref, nx_ref, copies, wst_ref, wbf_ref, slot_ref)

    @pl.when(valid)
    def _():
        y = _dot(a_ref[...], wbf_ref[...]) + b_ref[0]
        half = D_MODEL // 2
        packed = pltpu.pack_elementwise([y[:, 0:half], y[:, half:D_MODEL]], packed_dtype=BF16)
        for c in range(XW):
            y_ref[pl.ds(c, TM_G, stride=XW), :] = packed[:, c * LANES:(c + 1) * LANES]

    @pl.when(jnp.logical_not(valid))
    def _():
        zero = jnp.zeros(y_ref.shape, F32)
        y_ref[...] = pltpu.pack_elementwise([zero, zero], packed_dtype=BF16)


def _gemm2(te, tv, nx, act, w_d, b_d3, n_tiles):
    n_rows = act.shape[0]
    nJ = D_MODEL // TN_G2
    grid_spec = pltpu.PrefetchScalarGridSpec(
        num_scalar_prefetch=3,
        grid=(nJ, n_tiles),
        in_specs=[
            pl.BlockSpec((TM_G, D_FF), lambda j, i, te, tv, nx: (i, 0)),
            pl.BlockSpec((1, 1, TN_G2), lambda j, i, te, tv, nx: (te[i], 0, j)),
            pl.BlockSpec(memory_space=pl.ANY),
        ],
        out_specs=pl.BlockSpec((TM_G * XW, LANES), lambda j, i, te, tv, nx: (i, 0)),
        scratch_shapes=[
            pltpu.VMEM((2, D_FF, TN_G2), F32),
            pltpu.VMEM((D_FF, TN_G2), BF16),
            pltpu.SMEM((1,), I32),
            pltpu.SemaphoreType.DMA((2,)),
        ],
    )
    return pl.pallas_call(
        _gemm2_kernel,
        grid_spec=grid_spec,
        out_shape=jax.ShapeDtypeStruct((n_rows * XW, LANES), U32),
        compiler_params=_cparams(2),
        name="gemm2",
    )(te, tv, nx, act, b_d3, w_d)


def _combine_kernel(dest_ref, dnext_ref, wt_ref, x1_ref, y_ref, o_ref, buf_ref, acc_ref, sem):
    i = pl.program_id(0)
    n = pl.num_programs(0)

    def gather(d_ref, slot):
        def issue(t, carry):
            for k in range(TOP_K):
                pltpu.make_async_copy(_slab(y_ref, d_ref[k, t], 1, XW), _slab(buf_ref.at[slot, k], t, 1, XW),
                                      sem.at[slot]).start(priority=k % 2)
            return carry
        lax.fori_loop(0, TT_COMB, issue, 0)

    def finish(slot):
        pltpu.make_async_copy(buf_ref.at[slot], buf_ref.at[slot], sem.at[slot]).wait()
        acc = [None, None]
        for k in range(TOP_K):
            w = jnp.broadcast_to(wt_ref[:, k:k + 1], (TT_COMB, LANES))
            w = jnp.broadcast_to(w[:, None, :], (TT_COMB, XW, LANES)).reshape(TT_COMB * XW, LANES)
            for half in range(2):
                term = pltpu.unpack_elementwise(buf_ref[slot, k], index=half, packed_dtype=BF16,
                                                unpacked_dtype=F32) * w
                acc[half] = term if acc[half] is None else acc[half] + term
        for half in range(2):
            acc_ref[half] = acc[half]
        for half in range(2):
            for c in range(XW):
                f0 = half * (D_MODEL // 2) + c * LANES
                o_ref[:, f0:f0 + LANES] = (x1_ref[:, f0:f0 + LANES]
                                           + acc_ref[half, pl.ds(c, TT_COMB, stride=XW), :])

    @pl.when(i == 0)
    def _():
        gather(dest_ref, 0)

    for slot in range(2):
        @pl.when((i & 1) == slot)
        def _(slot=slot):
            @pl.when(i + 1 < n)
            def _():
                gather(dnext_ref, 1 - slot)
            finish(slot)


def _combine(dest, wt_tok, x1, y):
    T = x1.shape[0]
    nT = T // TT_COMB
    return pl.pallas_call(
        _combine_kernel,
        grid=(nT,),
        in_specs=[
            pl.BlockSpec((TOP_K, TT_COMB), lambda i: (0, i), memory_space=pltpu.SMEM),
            pl.BlockSpec((TOP_K, TT_COMB), lambda i: (0, jnp.minimum(i + 1, nT - 1)),
                         memory_space=pltpu.SMEM),
            pl.BlockSpec((TT_COMB, TOP_K), lambda i: (i, 0)),
            pl.BlockSpec((TT_COMB, D_MODEL), lambda i: (i, 0)),
            pl.BlockSpec(memory_space=pl.ANY),
        ],
        out_specs=pl.BlockSpec((TT_COMB, D_MODEL), lambda i: (i, 0)),
        out_shape=jax.ShapeDtypeStruct((T, D_MODEL), F32),
        scratch_shapes=[pltpu.VMEM((2, TOP_K, TT_COMB * XW, LANES), U32),
                        pltpu.VMEM((2, TT_COMB * XW, LANES), F32), pltpu.SemaphoreType.DMA((2,))],
        compiler_params=_cparams(1),
        name="combine",
    )(dest, dest, wt_tok, x1, y)


def _rope_tables(pos):
    half = ROT_DIM // 2
    inv_freq = ROPE_THETA ** (-jnp.arange(0, ROT_DIM, 2, dtype=F32) / ROT_DIM)
    ang = pos[:, None] * inv_freq[None, :]
    cos, sin = jnp.cos(ang), jnp.sin(ang)
    n = pos.shape[0]
    ones = jnp.ones((n, LANES - ROT_DIM), F32)
    zeros = jnp.zeros((n, LANES - ROT_DIM), F32)
    zh = jnp.zeros((n, half), F32)
    return (jnp.concatenate([cos, cos, ones], axis=1),
            jnp.concatenate([-sin, zh, zeros], axis=1),
            jnp.concatenate([zh, sin, zeros], axis=1))


def _permute_w_in(w_in):
    kv_end = NSA_WIDTH + N_KV_COLS * HEAD_DIM
    n_gate = NSA_HEADS * N_BRANCH
    per_g = HEADS_PER_GROUP * N_BRANCH
    gate = w_in[:, kv_end:kv_end + n_gate]
    pieces = [w_in[:, :kv_end], w_in[:, kv_end + n_gate:]]
    zpad = jnp.zeros((D_MODEL, GATE_PAD - per_g), w_in.dtype)
    for g in range(NSA_KV_HEADS):
        pieces += [gate[:, g * per_g:(g + 1) * per_g], zpad]
    pieces.append(jnp.zeros((D_MODEL, LANES - NSA_KV_HEADS * GATE_PAD), w_in.dtype))
    return jnp.concatenate(pieces, axis=1).astype(BF16)


def _layer(x, norm1_g, w_in, q_norm_g, k_norm_cmp_g, k_norm_slc_g, k_norm_win_g,
           cmp_k_pos, cmp_k_w1, cmp_k_b1, cmp_k_w2, cmp_k_b2,
           cmp_v_pos, cmp_v_w1, cmp_v_b1, cmp_v_w2, cmp_v_b2,
           pool_w, pool_scale, w_out, norm2_g,
           w_router, b_router, w_gate_up, b_gate_up, w_down, b_down):
    B, S, _ = x.shape
    T = B * S
    assert S % TQ == 0 and TQ == TK == TM_IN and WINDOW == 2 * TK and S // SLC_BLK <= N_BLK_PAD
    assert (S // CMP_STRIDE) % LANES == 0 and T % TT_ROUTE == 0
    NC = S // CMP_STRIDE
    x2 = x.reshape(T, D_MODEL)
    row = lambda v: v.reshape(1, -1)

    cos, slo, shi = _rope_tables(jnp.arange(S, dtype=F32))
    qn, kvc, ks, vst, kw, vwt, gt, pool_in = _inproj(
        x2, row(norm1_g), _permute_w_in(w_in), cos, slo, shi,
        row(q_norm_g), row(k_norm_slc_g), row(k_norm_win_g), B, S)

    c_end = (jnp.arange(NC, dtype=I32) * CMP_STRIDE + (CMP_BLK - 1)).astype(F32)
    ccos, cslo, cshi = _rope_tables(c_end)
    cw = CMP_STRIDE * HEAD_DIM
    kc = _compress(kvc, 0, cmp_k_pos.reshape(2, cw), cmp_k_w1.reshape(2, cw, CMP_HIDDEN).astype(BF16),
                   row(cmp_k_b1), cmp_k_w2.astype(BF16), row(cmp_k_b2), B, S,
                   rope=(row(k_norm_cmp_g), ccos, cslo, cshi))
    vct = _compress(kvc, 1, cmp_v_pos.reshape(2, cw), cmp_v_w1.reshape(2, cw, CMP_HIDDEN).astype(BF16),
                    row(cmp_v_b1), cmp_v_w2.astype(BF16), row(cmp_v_b2), B, S)

    nsa_out = _nsa(qn, kc, vct, ks, vst, kw, vwt, gt, B, S)

    wr_pad = jnp.pad(w_router, ((0, 0), (0, LANES - N_EXPERTS)))
    wr_hi = wr_pad.astype(BF16)
    wr_lo = (wr_pad - wr_hi.astype(F32)).astype(BF16)
    br_pad = jnp.concatenate([b_router.astype(F32), jnp.full((LANES - N_EXPERTS,), MASK_VALUE, F32)])
    x1, h2p, logits = _mix(nsa_out, pool_in, x2, w_out.astype(BF16), pool_w.astype(BF16),
                           row(pool_scale), row(norm2_g), jnp.concatenate([wr_hi, wr_lo], axis=1),
                           row(br_pad), S)

    n_tiles = T * TOP_K // TM_G + N_EXPERTS
    dest, wts, meta = _route(logits, n_tiles)
    te, tv, nx = meta[0, :n_tiles], meta[1, :n_tiles], meta[2, :n_tiles]
    xs = _dispatch(meta[3, :N_EXPERTS + 1], dest, h2p, n_tiles * TM_G)
    act = _gemm1(te, tv, nx, xs, w_gate_up, b_gate_up.reshape(N_EXPERTS, 1, 2 * D_FF), n_tiles)
    y = _gemm2(te, tv, nx, act, w_down, b_down.reshape(N_EXPERTS, 1, D_MODEL), n_tiles)
    out = _combine(dest, wts.T, x1, y)
    return out.reshape(B, S, D_MODEL)


def kernel(x, norm1_g, w_in, q_norm_g, k_norm_cmp_g, k_norm_slc_g, k_norm_win_g, cmp_k_pos, cmp_k_w1, cmp_k_b1, cmp_k_w2, cmp_k_b2, cmp_v_pos, cmp_v_w1, cmp_v_b1, cmp_v_w2, cmp_v_b2, pool_w, pool_scale, w_out, norm2_g, w_router, b_router, w_gate_up, b_gate_up, w_down, b_down):
    params = (norm1_g, w_in, q_norm_g, k_norm_cmp_g, k_norm_slc_g, k_norm_win_g,
              cmp_k_pos, cmp_k_w1, cmp_k_b1, cmp_k_w2, cmp_k_b2,
              cmp_v_pos, cmp_v_w1, cmp_v_b1, cmp_v_w2, cmp_v_b2,
              pool_w, pool_scale, w_out, norm2_g,
              w_router, b_router, w_gate_up, b_gate_up, w_down, b_down)
    depth = norm1_g.shape[0]
    for l in range(depth):
        x = _layer(x, *[p.reshape(p.shape[1:]) if depth == 1 else p[l] for p in params])
    return x
```

```python
import functools

import jax
import jax.numpy as jnp
from jax import lax
from jax.experimental import pallas as pl
from jax.experimental.pallas import tpu as pltpu

F32 = jnp.float32
BF16 = jnp.bfloat16
I32 = jnp.int32
U32 = jnp.uint32

D_MODEL = 2048
HEAD_DIM = 128
NSA_HEADS = 8
NSA_KV_HEADS = 2
HEADS_PER_GROUP = NSA_HEADS // NSA_KV_HEADS
NSA_WIDTH = NSA_HEADS * HEAD_DIM
N_BRANCH = 3
CMP_BLK = 32
CMP_STRIDE = 16
CMP_HIDDEN = 256
SLC_BLK = 64
SLC_SHIFT = SLC_BLK.bit_length() - 1
N_SEL = 16
WINDOW = 512
ROPE_THETA = 500000.0
ROT_DIM = HEAD_DIM // 4
POOL_WIDTH = 1024
POOL_WINDOWS = (2, 4, 8, 16)
POOL_GC = 256
N_EXPERTS = 32
TOP_K = 4
D_FF = 2048
SWIGLU_ALPHA = 1.702
SWIGLU_LIMIT = 7.0
RMS_EPS = 1e-5
QK_EPS = 1e-6
MASK_VALUE = -1e30
FORCE_VALUE = 1e30
LOG2_E = 1.4426950408889634

LANES = 128
N_KV_COLS = 6 * NSA_KV_HEADS
GATE_PAD = 16
N_BLK_PAD = 128
POOL_HALO = 16

TM_IN = 256
TQ = 256
TK = 256
SEL_UNROLL = 4
TM_MIX = 256
MIX_SUB = 256
TT_ROUTE = 1024
TM_G = 256
TN_G1 = 1024
TN_G2 = 2048
XW = D_MODEL // 2 // LANES
TT_DISP = 256
TT_COMB = 256
VMEM_LIMIT = 56 * 1024 * 1024


def _cparams(n_axes, vmem=VMEM_LIMIT):
    return pltpu.CompilerParams(
        dimension_semantics=("arbitrary",) * n_axes, vmem_limit_bytes=vmem)


def _dot(a, b):
    return jnp.dot(a, b, preferred_element_type=F32)


def _slab(ref, token, n_tokens, width):
    first = pl.multiple_of(token * width, width)
    return ref.at[pl.ds(first, n_tokens * width), :]


def _dot_nt(a, b):
    return lax.dot_general(a, b, (((1,), (1,)), ((), ())), preferred_element_type=F32)


def _inproj_kernel(x_ref, g_ref, w_ref, cos_ref, slo_ref, shi_ref, qg_ref, ksg_ref, kwg_ref,
                   qn_ref, kvc_ref, ks_ref, vst_ref, kw_ref, vwt_ref, gt_ref, pool_ref, *, nS):
    x = x_ref[...]
    y = x * lax.rsqrt(jnp.mean(x * x, axis=-1, keepdims=True) + RMS_EPS)
    h = (y * g_ref[...]).astype(BF16)
    cos, slo, shi = cos_ref[...], slo_ref[...], shi_ref[...]
    G = NSA_KV_HEADS
    head = lambda z, c: z[:, c * HEAD_DIM:(c + 1) * HEAD_DIM]

    scale = HEAD_DIM ** -0.5 * LOG2_E
    zq = _dot(h, w_ref[:, 0:NSA_WIDTH])
    for c in range(NSA_HEADS):
        qn_ref[:, c * HEAD_DIM:(c + 1) * HEAD_DIM] = (
            _norm_rope(head(zq, c), qg_ref[...], cos, slo, shi) * scale).astype(BF16)

    pair = lambda p: _dot(h, w_ref[:, NSA_WIDTH + p * G * HEAD_DIM:NSA_WIDTH + (p + 1) * G * HEAD_DIM])
    zkc, zvc = pair(0), pair(1)
    for g in range(G):
        kvc_ref[g] = head(zkc, g)
        kvc_ref[G + g] = head(zvc, g)
    zks, zvs, zkw, zvw = pair(2), pair(3), pair(4), pair(5)
    row = lax.rem(pl.program_id(0), nS) * TM_IN + lax.broadcasted_iota(I32, (TM_IN, N_BLK_PAD), 0)
    lane = lax.broadcasted_iota(I32, (TM_IN, N_BLK_PAD), 1)
    onehot = jnp.where((row >> SLC_SHIFT) == lane, 1.0, 0.0).astype(BF16)
    for g in range(G):
        ks_ref[g, :, 0:HEAD_DIM] = _norm_rope(head(zks, g), ksg_ref[...], cos, slo, shi).astype(BF16)
        ks_ref[g, :, HEAD_DIM:HEAD_DIM + N_BLK_PAD] = onehot
        vst_ref[g, 0] = head(zvs, g).T.astype(BF16)
        kw_ref[g] = _norm_rope(head(zkw, g), kwg_ref[...], cos, slo, shi).astype(BF16)
        vwt_ref[g, 0] = head(zvw, g).T.astype(BF16)

    base = NSA_WIDTH + N_KV_COLS * HEAD_DIM
    pool_ref[...] = _dot(h, w_ref[:, base:base + POOL_WIDTH])
    sig_t = jax.nn.sigmoid(_dot(h, w_ref[:, base + POOL_WIDTH:base + POOL_WIDTH + LANES])).T
    for g in range(G):
        gt_ref[g] = sig_t[g * GATE_PAD:(g + 1) * GATE_PAD]


def _inproj(x2, g1, w_perm, cos, slo, shi, qg, ksg, kwg, B, S):
    T = x2.shape[0]
    n_cols = w_perm.shape[1]
    nS = S // TM_IN
    G = NSA_KV_HEADS
    full = lambda shape: pl.BlockSpec(shape, lambda i: (0,) * len(shape))
    tab = pl.BlockSpec((TM_IN, LANES), lambda i: (lax.rem(i, nS), 0))
    bi = lambda i: (i // nS, lax.rem(i, nS))
    return pl.pallas_call(
        functools.partial(_inproj_kernel, nS=nS),
        grid=(T // TM_IN,),
        in_specs=[
            pl.BlockSpec((TM_IN, D_MODEL), lambda i: (i, 0)),
            full((1, D_MODEL)),
            full((D_MODEL, n_cols)),
            tab, tab, tab,
            full((1, HEAD_DIM)), full((1, HEAD_DIM)), full((1, HEAD_DIM)),
        ],
        out_specs=[
            pl.BlockSpec((TM_IN, NSA_WIDTH), lambda i: (i, 0)),
            pl.BlockSpec((2 * G, TM_IN, LANES), lambda i: (0, i, 0)),
            pl.BlockSpec((G, TM_IN, 2 * HEAD_DIM), lambda i: (*bi(i), 0)),
            pl.BlockSpec((G, 1, HEAD_DIM, TM_IN), lambda i: (*bi(i), 0, 0)),
            pl.BlockSpec((G, TM_IN, HEAD_DIM), lambda i: (*bi(i), 0)),
            pl.BlockSpec((G, 1, HEAD_DIM, TM_IN), lambda i: (*bi(i), 0, 0)),
            pl.BlockSpec((G, GATE_PAD, TM_IN), lambda i: (i // nS, 0, lax.rem(i, nS))),
            pl.BlockSpec((TM_IN, POOL_WIDTH), lambda i: (i, 0)),
        ],
        out_shape=[
            jax.ShapeDtypeStruct((T, NSA_WIDTH), BF16),
            jax.ShapeDtypeStruct((2 * G, T, LANES), F32),
            jax.ShapeDtypeStruct((B * G, S, 2 * HEAD_DIM), BF16),
            jax.ShapeDtypeStruct((B * G, nS, HEAD_DIM, TM_IN), BF16),
            jax.ShapeDtypeStruct((B * G, S, HEAD_DIM), BF16),
            jax.ShapeDtypeStruct((B * G, nS, HEAD_DIM, TM_IN), BF16),
            jax.ShapeDtypeStruct((B * G, GATE_PAD, S), F32),
            jax.ShapeDtypeStruct((T, POOL_WIDTH), F32),
        ],
        compiler_params=_cparams(1),
        name="inproj",
    )(x2, g1, w_perm, cos, slo, shi, qg, ksg, kwg)


def _norm_rope(x, g, cos, sin_lo, sin_hi):
    half = ROT_DIM // 2
    y = x * lax.rsqrt(jnp.mean(x * x, axis=-1, keepdims=True) + QK_EPS) * g
    return (y * cos + pltpu.roll(y, LANES - half, 1) * sin_lo
            + pltpu.roll(y, half, 1) * sin_hi)


def _compress_body(x_ref, pos_ref, w1_ref, b1_ref, w2_ref, b2_ref):
    nc = x_ref.shape[1] // CMP_STRIDE
    pair_w = 2 * HEAD_DIM
    first = jnp.zeros((nc, CMP_HIDDEN), F32)
    second = jnp.zeros((nc, CMP_HIDDEN), F32)
    for p in range(CMP_STRIDE // 2):
        x = jnp.concatenate([x_ref[0, pl.ds(2 * p + u, nc, stride=CMP_STRIDE), :] for u in range(2)], axis=1)
        cols = slice(p * pair_w, (p + 1) * pair_w)
        first = first + _dot((x + pos_ref[0:1, cols]).astype(BF16), w1_ref[0, cols, :])
        second = second + _dot((x + pos_ref[1:2, cols]).astype(BF16), w1_ref[1, cols, :])
    nxt = pltpu.roll(second, nc - 1, 0)
    hid = jax.nn.gelu(first + nxt + b1_ref[...])
    return _dot(hid.astype(BF16), w2_ref[...]) + b2_ref[...]


def _compress_k_kernel(x_ref, pos_ref, w1_ref, b1_ref, w2_ref, b2_ref,
                       g_ref, cos_ref, slo_ref, shi_ref, o_ref):
    out = _compress_body(x_ref, pos_ref, w1_ref, b1_ref, w2_ref, b2_ref)
    o_ref[0] = _norm_rope(out, g_ref[...], cos_ref[...], slo_ref[...], shi_ref[...]).astype(BF16)


def _compress_v_kernel(x_ref, pos_ref, w1_ref, b1_ref, w2_ref, b2_ref, o_ref):
    out = _compress_body(x_ref, pos_ref, w1_ref, b1_ref, w2_ref, b2_ref)
    o_ref[0] = out.T.astype(BF16)


def _compress(kvc, which, pos2, w1, b1, w2, b2, B, S, rope=None):
    NC = S // CMP_STRIDE
    G = NSA_KV_HEADS
    CW = CMP_STRIDE * HEAD_DIM
    common_specs = [
        pl.BlockSpec((1, S, HEAD_DIM), lambda b, g: (which * G + g, b, 0)),
        pl.BlockSpec((2, CW), lambda b, g: (0, 0)),
        pl.BlockSpec((2, CW, CMP_HIDDEN), lambda b, g: (0, 0, 0)),
        pl.BlockSpec((1, CMP_HIDDEN), lambda b, g: (0, 0)),
        pl.BlockSpec((CMP_HIDDEN, HEAD_DIM), lambda b, g: (0, 0)),
        pl.BlockSpec((1, HEAD_DIM), lambda b, g: (0, 0)),
    ]
    if rope is not None:
        gk, cos, slo, shi = rope
        tab = pl.BlockSpec((NC, LANES), lambda b, g: (0, 0))
        return pl.pallas_call(
            _compress_k_kernel,
            grid=(B, G),
            in_specs=common_specs + [pl.BlockSpec((1, HEAD_DIM), lambda b, g: (0, 0)), tab, tab, tab],
            out_specs=pl.BlockSpec((1, NC, HEAD_DIM), lambda b, g: (b * G + g, 0, 0)),
            out_shape=jax.ShapeDtypeStruct((B * G, NC, HEAD_DIM), BF16),
            compiler_params=_cparams(2),
            name="compress_k",
        )(kvc, pos2, w1, b1, w2, b2, gk, cos, slo, shi)
    return pl.pallas_call(
        _compress_v_kernel,
        grid=(B, G),
        in_specs=common_specs,
        out_specs=pl.BlockSpec((1, HEAD_DIM, NC), lambda b, g: (b * G + g, 0, 0)),
        out_shape=jax.ShapeDtypeStruct((B * G, HEAD_DIM, NC), BF16),
        compiler_params=_cparams(2),
        name="compress_v",
    )(kvc, pos2, w1, b1, w2, b2)


BIAS_DIAG, BIAS_FAR, BIAS_ALL, BIAS_NONE = 0, 1, 2, 3


def _tile_bias_table():
    k = lax.broadcasted_iota(I32, (TK, TQ), 0)
    t = lax.broadcasted_iota(I32, (TK, TQ), 1)
    neg = jnp.full((TK, TQ), MASK_VALUE, F32)
    zero = jnp.zeros((TK, TQ), F32)
    return jnp.stack([jnp.where(k <= t, zero, neg), jnp.where(k > t, zero, neg), neg, zero])


def _add_tile_bias(s, b):
    return jnp.concatenate([s[:, r * TQ:(r + 1) * TQ] + b for r in range(HEADS_PER_GROUP)], axis=1)


def _nsa_kernel(q_ref, kc_ref, vct_ref, ks_ref, vst_ref, kw_ref, vwt_ref, gt_ref, bias_ref, o_ref,
                qaug_ref, s_ref, cm_ref, m_ref, l_ref, acc_ref, oc_ref, ow_ref, *, S):
    R = HEADS_PER_GROUP
    NQ = R * TQ
    NC = S // CMP_STRIDE
    qi = pl.program_id(2)
    qs = qi * TQ

    q = q_ref[...].astype(F32)
    q_t = jnp.concatenate([q[:, r * HEAD_DIM:(r + 1) * HEAD_DIM].T for r in range(R)], axis=1).astype(BF16)
    col = lax.broadcasted_iota(I32, (1, NQ), 1)
    t_row = qs + (col & (TQ - 1))

    s_c = _dot(kc_ref[0], q_t)
    c_end = lax.broadcasted_iota(I32, (NC, 1), 0) * CMP_STRIDE + (CMP_BLK - 1)
    s_c = jnp.where(c_end <= t_row, s_c, MASK_VALUE)
    e_c = jnp.exp2(s_c - jnp.max(s_c, axis=0, keepdims=True))
    l_c = jnp.sum(e_c, axis=0, keepdims=True)
    p_c = e_c * jnp.where(t_row >= CMP_BLK - 1, 1.0 / l_c, 0.0)
    oc_ref[...] = _dot(vct_ref[0], p_c.astype(BF16))

    imp = p_c[:, 0:TQ]
    for r in range(1, R):
        imp = imp + p_c[:, r * TQ:(r + 1) * TQ]
    jj = lax.broadcasted_iota(I32, (N_BLK_PAD, NC), 0) * (SLC_BLK // CMP_STRIDE)
    nn = lax.broadcasted_iota(I32, (N_BLK_PAD, NC), 1)
    per = SLC_BLK // CMP_STRIDE
    fold = (jnp.where((nn >= jj) & (nn < jj + per), 1.0, 0.0)
            + jnp.where((nn >= jj - 1) & (nn < jj + per - 1), 1.0, 0.0)).astype(BF16)
    imp_hi = imp.astype(BF16)
    imp_lo = (imp - imp_hi.astype(F32)).astype(BF16)
    blk_score = _dot(fold, imp_hi) + _dot(fold, imp_lo)

    jb = lax.broadcasted_iota(I32, (N_BLK_PAD, TQ), 0)
    cur = (qs + lax.broadcasted_iota(I32, (N_BLK_PAD, TQ), 1)) >> SLC_SHIFT
    forced = (jb == 0) | (jb == cur) | (jb == cur - 1)
    val = jnp.where(forced, FORCE_VALUE, jnp.where(jb <= cur, blk_score, MASK_VALUE))
    jbf = jb.astype(F32)
    bias = jnp.where(forced, 0.0, MASK_VALUE)
    val = jnp.where(forced, -jnp.inf, val)
    for _ in range(min(N_SEL, S // SLC_BLK) - 3):
        mx = jnp.max(val, axis=0, keepdims=True)
        first = jnp.min(jnp.where(val == mx, jbf, float(N_BLK_PAD)), axis=0, keepdims=True)
        pick = jbf == first
        bias = jnp.where(pick, 0.0, bias)
        val = jnp.where(pick, -jnp.inf, val)
    qaug_ref[0:HEAD_DIM, :] = q_t
    qaug_ref[HEAD_DIM:HEAD_DIM + N_BLK_PAD, :] = jnp.concatenate([bias.astype(BF16)] * R, axis=1)

    w_tiles = ((jnp.maximum(qi - 2, 0), jnp.where(qi >= 2, BIAS_FAR, BIAS_ALL)),
               (jnp.maximum(qi - 1, 0), jnp.where(qi >= 1, BIAS_NONE, BIAS_ALL)),
               (qi, BIAS_DIAG))
    s_w = []
    for kt, bi in w_tiles:
        kwt = kw_ref[0, pl.ds(pl.multiple_of(kt * TK, TK), TK), :]
        s_w.append(_add_tile_bias(_dot(kwt, q_t), bias_ref[bi]))
    m_w = jnp.max(s_w[0], axis=0, keepdims=True)
    for s in s_w[1:]:
        m_w = jnp.maximum(m_w, jnp.max(s, axis=0, keepdims=True))
    l_w = jnp.zeros((1, NQ), F32)
    o_w = jnp.zeros((HEAD_DIM, NQ), F32)
    for (kt, _), s in zip(w_tiles, s_w):
        e = jnp.exp2(s - m_w)
        l_w = l_w + jnp.sum(e, axis=0, keepdims=True)
        o_w = o_w + _dot(vwt_ref[0, kt], e.astype(BF16))
    ow_ref[...] = o_w * (1.0 / l_w)

    m_ref[...] = jnp.full((1, NQ), MASK_VALUE, F32)
    l_ref[...] = jnp.zeros((1, NQ), F32)
    acc_ref[...] = jnp.zeros((HEAD_DIM, NQ), F32)

    def produce(kj, slot, causal):
        k = ks_ref[0, pl.ds(pl.multiple_of(kj * TK, TK), TK), :]
        s = _dot(k, qaug_ref[...])
        if causal:
            s = _add_tile_bias(s, bias_ref[BIAS_DIAG])
        s_ref[slot] = s
        cm_ref[slot] = jnp.max(s, axis=0, keepdims=True)

    def consume(kj, slot):
        m_old = m_ref[...]
        m_new = jnp.maximum(m_old, cm_ref[slot])
        alpha = jnp.exp2(m_old - m_new)
        p = jnp.exp2(s_ref[slot] - m_new)
        l_ref[...] = alpha * l_ref[...] + jnp.sum(p, axis=0, keepdims=True)
        acc_ref[...] = alpha * acc_ref[...] + _dot(vst_ref[0, kj], p.astype(BF16))
        m_ref[...] = m_new

    def stage(kj, slot, causal_next):
        produce(kj + 1, 1 - slot, causal_next)
        consume(kj, slot)

    @pl.when(qi == 0)
    def _():
        produce(0, 0, True)

    @pl.when(qi > 0)
    def _():
        produce(0, 0, False)

    def stage_group(pp, carry):
        for u in range(SEL_UNROLL):
            stage(SEL_UNROLL * pp + u, u & 1, False)
        return carry

    n_plain = jnp.maximum(qi - 1, 0)
    n_grouped = n_plain // SEL_UNROLL * SEL_UNROLL
    lax.fori_loop(0, n_plain // SEL_UNROLL, stage_group, 0)
    for u in range(SEL_UNROLL - 1):
        @pl.when(n_plain - n_grouped > u)
        def _(u=u):
            stage(n_grouped + u, u & 1, False)

    @pl.when((qi >= 1) & ((n_plain & 1) == 0))
    def _():
        stage(qi - 1, 0, True)
        consume(qi, 1)

    @pl.when((qi >= 1) & ((n_plain & 1) == 1))
    def _():
        stage(qi - 1, 1, True)
        consume(qi, 0)

    @pl.when(qi == 0)
    def _():
        consume(0, 0)

    inv_l = 1.0 / l_ref[...]

    for r in range(R):
        sl = slice(r * TQ, (r + 1) * TQ)
        g0 = gt_ref[0, r * N_BRANCH + 0:r * N_BRANCH + 1, :]
        g1 = gt_ref[0, r * N_BRANCH + 1:r * N_BRANCH + 2, :]
        g2 = gt_ref[0, r * N_BRANCH + 2:r * N_BRANCH + 3, :]
        o = (g0 * oc_ref[:, sl] + (g1 * inv_l[:, sl]) * acc_ref[:, sl]
             + g2 * ow_ref[:, sl])
        o_ref[:, r * HEAD_DIM:(r + 1) * HEAD_DIM] = o.T.astype(BF16)


def _nsa(qn, kc, vct, ks, vst, kw, vwt, gt, B, S):
    G = NSA_KV_HEADS
    R = HEADS_PER_GROUP
    nQ = S // TQ
    nK = S // TK
    NC = S // CMP_STRIDE
    bg = lambda b, g, i: b * G + g
    return pl.pallas_call(
        functools.partial(_nsa_kernel, S=S),
        grid=(B, G, nQ),
        in_specs=[
            pl.BlockSpec((TQ, R * HEAD_DIM), lambda b, g, i: (b * nQ + i, g)),
            pl.BlockSpec((1, NC, HEAD_DIM), lambda b, g, i: (bg(b, g, i), 0, 0)),
            pl.BlockSpec((1, HEAD_DIM, NC), lambda b, g, i: (bg(b, g, i), 0, 0)),
            pl.BlockSpec((1, S, 2 * HEAD_DIM), lambda b, g, i: (bg(b, g, i), 0, 0)),
            pl.BlockSpec((1, nK, HEAD_DIM, TK), lambda b, g, i: (bg(b, g, i), 0, 0, 0)),
            pl.BlockSpec((1, S, HEAD_DIM), lambda b, g, i: (bg(b, g, i), 0, 0)),
            pl.BlockSpec((1, nK, HEAD_DIM, TK), lambda b, g, i: (bg(b, g, i), 0, 0, 0)),
            pl.BlockSpec((1, GATE_PAD, TQ), lambda b, g, i: (bg(b, g, i), 0, i)),
            pl.BlockSpec((4, TK, TQ), lambda b, g, i: (0, 0, 0)),
        ],
        out_specs=pl.BlockSpec((TQ, R * HEAD_DIM), lambda b, g, i: (b * nQ + i, g)),
        out_shape=jax.ShapeDtypeStruct((B * S, NSA_WIDTH), BF16),
        scratch_shapes=[
            pltpu.VMEM((2 * HEAD_DIM, R * TQ), BF16),
            pltpu.VMEM((2, TK, R * TQ), F32),
            pltpu.VMEM((2, 1, R * TQ), F32),
            pltpu.VMEM((1, R * TQ), F32),
            pltpu.VMEM((1, R * TQ), F32),
            pltpu.VMEM((HEAD_DIM, R * TQ), F32),
            pltpu.VMEM((HEAD_DIM, R * TQ), F32),
            pltpu.VMEM((HEAD_DIM, R * TQ), F32),
        ],
        compiler_params=_cparams(3),
        name="nsa",
    )(qn, kc, vct, ks, vst, kw, vwt, gt, _tile_bias_table())


def _mix_kernel(nsa_ref, pool_ref, halo_ref, x_ref, wo_ref, pw_ref, ps_ref, g2_ref,
                wr2_ref, br_ref, x1_ref, h2p_ref, logit_ref, ext_ref, *, S):
    nS = S // TM_MIX
    i = pl.program_id(0)
    t0 = lax.rem(i, nS) * TM_MIX
    ext_ref[0:POOL_HALO, :] = jnp.where(t0 == 0, 0.0, halo_ref[...])
    ext_ref[POOL_HALO:POOL_HALO + TM_MIX, :] = pool_ref[...]
    for sb in range(TM_MIX // MIX_SUB):
        r0 = sb * MIX_SUB
        rows = slice(r0, r0 + MIX_SUB)
        t = t0 + r0 + lax.broadcasted_iota(I32, (MIX_SUB, POOL_GC), 0)
        acc = _dot(nsa_ref[rows, :], wo_ref[0:NSA_WIDTH, :])
        for gi, w in enumerate(POOL_WINDOWS):
            cs = slice(gi * POOL_GC, (gi + 1) * POOL_GC)
            v = pool_ref[rows, cs]
            tot = v
            for k in range(1, w):
                tot = tot + ext_ref[POOL_HALO + r0 - k:POOL_HALO + r0 - k + MIX_SUB, cs]
            cnt = jnp.minimum(t + 1, w).astype(F32)
            d = tot / cnt - v
            y = _dot(d.astype(BF16), pw_ref[gi]) * ps_ref[:, cs]
            acc = acc + _dot(y.astype(BF16),
                             wo_ref[NSA_WIDTH + gi * POOL_GC:NSA_WIDTH + (gi + 1) * POOL_GC, :])
        x1 = x_ref[rows, :] + acc
        x1_ref[rows, :] = x1
        h2 = x1 * lax.rsqrt(jnp.mean(x1 * x1, axis=-1, keepdims=True) + RMS_EPS) * g2_ref[...]
        half = D_MODEL // 2
        packed = pltpu.pack_elementwise([h2[:, 0:half], h2[:, half:D_MODEL]], packed_dtype=BF16)
        for c in range(XW):
            h2p_ref[pl.ds(r0 * XW + c, MIX_SUB, stride=XW), :] = packed[:, c * LANES:(c + 1) * LANES]
        hi = h2.astype(BF16)
        lo = (h2 - hi.astype(F32)).astype(BF16)
        hl = _dot(hi, wr2_ref[...])
        logit_ref[rows, :] = (hl[:, 0:LANES] + hl[:, LANES:2 * LANES] + _dot(lo, wr2_ref[:, 0:LANES])
                              + br_ref[...])


def _mix(nsa_out, pool_in, x2, wo, pw, ps, g2, wr2, br, S):
    T = x2.shape[0]
    hb = TM_MIX // POOL_HALO
    full = lambda shape: pl.BlockSpec(shape, lambda i: (0,) * len(shape))
    return pl.pallas_call(
        functools.partial(_mix_kernel, S=S),
        grid=(T // TM_MIX,),
        in_specs=[
            pl.BlockSpec((TM_MIX, NSA_WIDTH), lambda i: (i, 0)),
            pl.BlockSpec((TM_MIX, POOL_WIDTH), lambda i: (i, 0)),
            pl.BlockSpec((POOL_HALO, POOL_WIDTH), lambda i: (jnp.maximum(i * hb - 1, 0), 0)),
            pl.BlockSpec((TM_MIX, D_MODEL), lambda i: (i, 0)),
            full((D_MODEL, D_MODEL)),
            full((len(POOL_WINDOWS), POOL_GC, POOL_GC)),
            full((1, POOL_WIDTH)),
            full((1, D_MODEL)),
            full((D_MODEL, 2 * LANES)),
            full((1, LANES)),
        ],
        out_specs=[
            pl.BlockSpec((TM_MIX, D_MODEL), lambda i: (i, 0)),
            pl.BlockSpec((TM_MIX * XW, LANES), lambda i: (i, 0)),
            pl.BlockSpec((TM_MIX, LANES), lambda i: (i, 0)),
        ],
        out_shape=[
            jax.ShapeDtypeStruct((T, D_MODEL), F32),
            jax.ShapeDtypeStruct((T * XW, LANES), U32),
            jax.ShapeDtypeStruct((T, LANES), F32),
        ],
        scratch_shapes=[pltpu.VMEM((POOL_HALO + TM_MIX, POOL_WIDTH), F32)],
        compiler_params=_cparams(1),
        name="mix",
    )(nsa_out, pool_in, pool_in, x2, wo, pw, ps, g2, wr2, br)


def _route_kernel(logit_ref, dest_ref, wt_ref, meta_ref, cnt_ref, run_ref, tri_ref, *, n_tiles):
    phase = pl.program_id(0)
    i = pl.program_id(1)
    TT = TT_ROUTE
    E = N_EXPERTS

    @pl.when((phase == 0) & (i == 0))
    def _():
        cnt_ref[...] = jnp.zeros_like(cnt_ref)
        run_ref[...] = jnp.zeros_like(run_ref)

    lt = logit_ref[...].T[0:E, :]
    ef = lax.broadcasted_iota(I32, (E, TT), 0).astype(F32)
    work = lt
    ids, vals, hots = [], [], []
    for _ in range(TOP_K):
        mx = jnp.max(work, axis=0, keepdims=True)
        eid = jnp.min(jnp.where(work == mx, ef, float(E)), axis=0, keepdims=True)
        hot = ef == eid
        ids.append(eid)
        vals.append(mx)
        hots.append(hot)
        work = jnp.where(hot, -jnp.inf, work)
    member = jnp.where(hots[0] | hots[1] | hots[2] | hots[3], 1.0, 0.0)
    tile_cnt = jnp.sum(member, axis=1, keepdims=True)

    @pl.when(phase == 0)
    def _():
        cnt_ref[...] = cnt_ref[...] + tile_cnt

    @pl.when(phase == 1)
    def _():
        ex = [jnp.exp(v - vals[0]) for v in vals]
        den = ex[0] + ex[1] + ex[2] + ex[3]
        cnt = jnp.broadcast_to(cnt_ref[...], (E, LANES))
        padded = jnp.ceil(cnt * (1.0 / TM_G)) * TM_G
        erow = lax.broadcasted_iota(I32, (E, LANES), 0)
        start = jnp.zeros((E, LANES), F32)
        running = jnp.zeros((1, LANES), F32)
        for e in range(E):
            start = jnp.where(erow == e, running, start)
            running = running + padded[e:e + 1, :]
        end = start + padded
        @pl.when(i == 0)
        def _():
            tri_ref[...] = jnp.where(lax.broadcasted_iota(I32, (TT, TT), 0)
                                     < lax.broadcasted_iota(I32, (TT, TT), 1), 1.0, 0.0).astype(BF16)

        before = _dot(member.astype(BF16), tri_ref[...]) + run_ref[...]
        slot = before + start[:, 0:1]
        for k in range(TOP_K):
            d = jnp.sum(jnp.where(hots[k], slot, 0.0), axis=0, keepdims=True)
            dest_ref[k:k + 1, :] = d.astype(I32)
            wt_ref[k:k + 1, :] = ex[k] / den
        run_ref[...] = run_ref[...] + tile_cnt

        @pl.when(i == 0)
        def _():
            lanes = meta_ref.shape[1]
            endw = jnp.broadcast_to(end[:, 0:1], (E, lanes))
            tile_row = (lax.broadcasted_iota(I32, (E, lanes), 1) * TM_G).astype(F32)
            owner = jnp.minimum(jnp.sum(jnp.where(endw <= tile_row, 1.0, 0.0), axis=0, keepdims=True),
                                float(E - 1))
            nxt = jnp.full((E, LANES), -1.0, F32)
            later = jnp.full((1, LANES), -1.0, F32)
            for e in reversed(range(E)):
                nxt = jnp.where(erow == e, later, nxt)
                later = jnp.where(cnt[e:e + 1, :] > 0.0, float(e), later)
            erow_w = lax.broadcasted_iota(I32, (E, lanes), 0).astype(F32)
            nxt_tile = jnp.sum(jnp.where(erow_w == owner, jnp.broadcast_to(nxt[:, 0:1], (E, lanes)), 0.0),
                               axis=0, keepdims=True)
            meta_ref[0:1, :] = owner.astype(I32)
            meta_ref[1:2, :] = jnp.where(tile_row[0:1, :] < running[:, 0:1], 1, 0).astype(I32)
            meta_ref[2:3, :] = nxt_tile.astype(I32)
            lane_w = lax.broadcasted_iota(I32, (E, lanes), 1).astype(F32)
            ends = jnp.sum(jnp.where(erow_w == lane_w, endw, 0.0), axis=0, keepdims=True)
            ends = jnp.where(lane_w[0:1, :] == float(E), running[:, 0:1] * (1.0 / TM_G), ends)
            meta_ref[3:4, :] = ends.astype(I32)
            meta_ref[4:8, :] = jnp.zeros((4, lanes), I32)


def _route(logits, n_tiles):
    T = logits.shape[0]
    nT = T // TT_ROUTE
    lanes = -(-n_tiles // LANES) * LANES
    return pl.pallas_call(
        functools.partial(_route_kernel, n_tiles=n_tiles),
        grid=(2, nT),
        in_specs=[pl.BlockSpec((TT_ROUTE, LANES), lambda p, i: (i, 0))],
        out_specs=[
            pl.BlockSpec((TOP_K, TT_ROUTE), lambda p, i: (0, i * p)),
            pl.BlockSpec((TOP_K, TT_ROUTE), lambda p, i: (0, i * p)),
            pl.BlockSpec((8, lanes), lambda p, i: (0, 0)),
        ],
        out_shape=[
            jax.ShapeDtypeStruct((TOP_K, T), I32),
            jax.ShapeDtypeStruct((TOP_K, T), F32),
            jax.ShapeDtypeStruct((8, lanes), I32),
        ],
        scratch_shapes=[pltpu.VMEM((N_EXPERTS, 1), F32), pltpu.VMEM((N_EXPERTS, 1), F32),
                        pltpu.VMEM((TT_ROUTE, TT_ROUTE), BF16)],
        compiler_params=_cparams(2),
        name="route",
    )(logits)


def _dispatch_kernel(ends_ref, dest_ref, h_ref, xs_ref, zero_ref, sem, zsem):
    @pl.when(pl.program_id(0) == 0)
    def _():
        zero_ref[...] = jnp.zeros_like(zero_ref)

        def tail_copy(e):
            return pltpu.make_async_copy(zero_ref, _slab(xs_ref, ends_ref[e] - TM_G, TM_G, XW), zsem)

        def nonempty(e):
            return ends_ref[e] > (ends_ref[e - 1] if e else 0)

        for e in range(N_EXPERTS):
            @pl.when(nonempty(e))
            def _(e=e):
                tail_copy(e).start()
        for e in range(N_EXPERTS):
            @pl.when(nonempty(e))
            def _(e=e):
                tail_copy(e).wait()

        def unused_copy(tile):
            return pltpu.make_async_copy(zero_ref, _slab(xs_ref, tile * TM_G, TM_G, XW), zsem)

        n_used = ends_ref[N_EXPERTS]
        n_all = xs_ref.shape[0] // (TM_G * XW)

        def start_unused(tile, carry):
            unused_copy(tile).start()
            return carry

        def wait_unused(tile, carry):
            unused_copy(tile).wait()
            return carry

        lax.fori_loop(n_used, n_all, start_unused, 0)
        lax.fori_loop(n_used, n_all, wait_unused, 0)

    def issue(t, carry):
        for k in range(TOP_K):
            pltpu.make_async_copy(_slab(h_ref, t, 1, XW), _slab(xs_ref, dest_ref[k, t], 1, XW),
                                  sem).start(priority=k % 2)
        return carry

    lax.fori_loop(0, TT_DISP, issue, 0)
    all_rows = _slab(xs_ref, 0, TOP_K * TT_DISP, XW)
    pltpu.make_async_copy(all_rows, all_rows, sem).wait()


def _dispatch(ends, dest, h2p, n_rows):
    T = h2p.shape[0] // XW
    grid_spec = pltpu.PrefetchScalarGridSpec(
        num_scalar_prefetch=1,
        grid=(T // TT_DISP,),
        in_specs=[
            pl.BlockSpec((TOP_K, TT_DISP), lambda i, ends: (0, i), memory_space=pltpu.SMEM),
            pl.BlockSpec((TT_DISP * XW, LANES), lambda i, ends: (i, 0)),
        ],
        out_specs=pl.BlockSpec(memory_space=pl.ANY),
        scratch_shapes=[pltpu.VMEM((TM_G * XW, LANES), U32), pltpu.SemaphoreType.DMA,
                        pltpu.SemaphoreType.DMA],
    )
    return pl.pallas_call(
        _dispatch_kernel,
        grid_spec=grid_spec,
        out_shape=jax.ShapeDtypeStruct((n_rows * XW, LANES), U32),
        compiler_params=_cparams(1),
        name="dispatch",
    )(ends, dest, h2p)


def _expert_changed(te_ref, i):
    return (i == 0) | (te_ref[i] != te_ref[jnp.maximum(i - 1, 0)])


WEIGHT_DMA_PRIORITY = 1


def _stream_expert_weights(te_ref, tv_ref, nx_ref, copies, wst_ref, wbf_ref, slot_ref):
    j = pl.program_id(0)
    i = pl.program_id(1)

    @pl.when((j == 0) & (i == 0))
    def _():
        slot_ref[0] = 0
        for c in copies(te_ref[0], 0, 0):
            c.start(priority=WEIGHT_DMA_PRIORITY)

    @pl.when((tv_ref[i] > 0) & _expert_changed(te_ref, i))
    def _():
        slot = slot_ref[0]
        e_next = nx_ref[i]

        @pl.when(e_next >= 0)
        def _():
            for c in copies(e_next, j, 1 - slot):
                c.start(priority=WEIGHT_DMA_PRIORITY)

        @pl.when((e_next < 0) & (j + 1 < pl.num_programs(0)))
        def _():
            for c in copies(te_ref[0], j + 1, 1 - slot):
                c.start(priority=WEIGHT_DMA_PRIORITY)

        for c in copies(te_ref[i], j, slot):
            c.wait()
        slot_ref[1] = slot
        slot_ref[0] = 1 - slot


W_CHUNK = 256


def _expert_matmul(x, wst_ref, wbf_ref, slot_ref, refresh):
    if not refresh:
        return _dot(x, wbf_ref[...])
    acc = None
    for k in range(x.shape[1] // W_CHUNK):
        rows = slice(k * W_CHUNK, (k + 1) * W_CHUNK)
        wbf_ref[rows, :] = wst_ref[slot_ref[1], rows, :].astype(BF16)
        part = _dot(x[:, rows], wbf_ref[rows, :])
        acc = part if acc is None else acc + part
    return acc


def _gemm1_kernel(te_ref, tv_ref, nx_ref, x_ref, bg_ref, bu_ref, w_hbm, act_ref,
                  wst_ref, wbf_ref, slot_ref, sem):
    i = pl.program_id(1)
    valid = tv_ref[i] > 0
    changed = _expert_changed(te_ref, i)

    def copies(e, j, slot):
        col = pl.multiple_of(j * TN_G1, TN_G1)
        return (pltpu.make_async_copy(w_hbm.at[e, :, pl.ds(col, TN_G1)],
                                      wst_ref.at[slot, :, pl.ds(0, TN_G1)], sem.at[slot]),
                pltpu.make_async_copy(w_hbm.at[e, :, pl.ds(D_FF + col, TN_G1)],
                                      wst_ref.at[slot, :, pl.ds(TN_G1, TN_G1)], sem.at[slot]))

    _stream_expert_weights(te_ref, tv_ref, nx_ref, copies, wst_ref, wbf_ref, slot_ref)

    def compute(refresh):
        xp = jnp.concatenate([x_ref[pl.ds(c, TM_G, stride=XW), :] for c in range(XW)], axis=1)
        lo = pltpu.unpack_elementwise(xp, index=0, packed_dtype=BF16, unpacked_dtype=F32).astype(BF16)
        hi = pltpu.unpack_elementwise(xp, index=1, packed_dtype=BF16, unpacked_dtype=F32).astype(BF16)
        gu = _expert_matmul(jnp.concatenate([lo, hi], axis=1), wst_ref, wbf_ref, slot_ref, refresh)
        gate = jnp.minimum(gu[:, 0:TN_G1] + bg_ref[0], SWIGLU_LIMIT)
        up = jnp.clip(gu[:, TN_G1:2 * TN_G1] + bu_ref[0], -SWIGLU_LIMIT, SWIGLU_LIMIT)
        act = (up + 1.0) * gate * jax.nn.sigmoid(SWIGLU_ALPHA * gate)
        act_ref[...] = act.astype(BF16)

    @pl.when(valid & changed)
    def _():
        compute(True)

    @pl.when(valid & jnp.logical_not(changed))
    def _():
        compute(False)

    @pl.when(jnp.logical_not(valid))
    def _():
        act_ref[...] = jnp.zeros_like(act_ref)


def _gemm1(te, tv, nx, xs, w_gu, b_gu3, n_tiles):
    n_rows = xs.shape[0] // XW
    nJ = D_FF // TN_G1
    grid_spec = pltpu.PrefetchScalarGridSpec(
        num_scalar_prefetch=3,
        grid=(nJ, n_tiles),
        in_specs=[
            pl.BlockSpec((TM_G * XW, LANES), lambda j, i, te, tv, nx: (i, 0)),
            pl.BlockSpec((1, 1, TN_G1), lambda j, i, te, tv, nx: (te[i], 0, j)),
            pl.BlockSpec((1, 1, TN_G1), lambda j, i, te, tv, nx: (te[i], 0, nJ + j)),
            pl.BlockSpec(memory_space=pl.ANY),
        ],
        out_specs=pl.BlockSpec((TM_G, TN_G1), lambda j, i, te, tv, nx: (i, j)),
        scratch_shapes=[
            pltpu.VMEM((2, D_MODEL, 2 * TN_G1), F32),
            pltpu.VMEM((D_MODEL, 2 * TN_G1), BF16),
            pltpu.SMEM((2,), I32),
            pltpu.SemaphoreType.DMA((2,)),
        ],
    )
    return pl.pallas_call(
        _gemm1_kernel,
        grid_spec=grid_spec,
        out_shape=jax.ShapeDtypeStruct((n_rows, D_FF), BF16),
        compiler_params=_cparams(2),
        name="gemm1",
    )(te, tv, nx, xs, b_gu3, b_gu3, w_gu)


def _gemm2_kernel(te_ref, tv_ref, nx_ref, a_ref, b_ref, w_hbm, y_ref, wst_ref, wbf_ref, slot_ref, sem):
    i = pl.program_id(1)
    valid = tv_ref[i] > 0

    def copies(e, j, slot):
        col = pl.multiple_of(j * TN_G2, TN_G2)
        return (pltpu.make_async_copy(w_hbm.at[e, :, pl.ds(col, TN_G2)], wst_ref.at[slot], sem.at[slot]),)

    _stream_expert_weights(te_ref, tv_ref, nx_ref, copies, wst_ref, wbf_ref, slot_ref)

    def compute(refresh):
        y = _expert_matmul(a_ref[...], wst_ref, wbf_ref, slot_ref, refresh) + b_ref[0]
        half = D_MODEL // 2
        packed = pltpu.pack_elementwise([y[:, 0:half], y[:, half:D_MODEL]], packed_dtype=BF16)
        for c in range(XW):
            y_ref[pl.ds(c, TM_G, stride=XW), :] = packed[:, c * LANES:(c + 1) * LANES]

    changed = _expert_changed(te_ref, i)

    @pl.when(valid & changed)
    def _():
        compute(True)

    @pl.when(valid & jnp.logical_not(changed))
    def _():
        compute(False)

    @pl.when(jnp.logical_not(valid))
    def _():
        zero = jnp.zeros(y_ref.shape, F32)
        y_ref[...] = pltpu.pack_elementwise([zero, zero], packed_dtype=BF16)


def _gemm2(te, tv, nx, act, w_d, b_d3, n_tiles):
    n_rows = act.shape[0]
    nJ = D_MODEL // TN_G2
    grid_spec = pltpu.PrefetchScalarGridSpec(
        num_scalar_prefetch=3,
        grid=(nJ, n_tiles),
        in_specs=[
            pl.BlockSpec((TM_G, D_FF), lambda j, i, te, tv, nx: (i, 0)),
            pl.BlockSpec((1, 1, TN_G2), lambda j, i, te, tv, nx: (te[i], 0, j)),
            pl.BlockSpec(memory_space=pl.ANY),
        ],
        out_specs=pl.BlockSpec((TM_G * XW, LANES), lambda j, i, te, tv, nx: (i, 0)),
        scratch_shapes=[
            pltpu.VMEM((2, D_FF, TN_G2), F32),
            pltpu.VMEM((D_FF, TN_G2), BF16),
            pltpu.SMEM((2,), I32),
            pltpu.SemaphoreType.DMA((2,)),
        ],
    )
    return pl.pallas_call(
        _gemm2_kernel,
        grid_spec=grid_spec,
        out_shape=jax.ShapeDtypeStruct((n_rows * XW, LANES), U32),
        compiler_params=_cparams(2),
        name="gemm2",
    )(te, tv, nx, act, b_d3, w_d)


def _combine_kernel(dest_ref, dnext_ref, wt_ref, x1_ref, y_ref, o_ref, buf_ref, acc_ref, sem):
    i = pl.program_id(0)
    n = pl.num_programs(0)

    def gather(d_ref, slot):
        def issue(t, carry):
            for k in range(TOP_K):
                pltpu.make_async_copy(_slab(y_ref, d_ref[k, t], 1, XW), _slab(buf_ref.at[slot, k], t, 1, XW),
                                      sem.at[slot]).start(priority=k % 2)
            return carry
        lax.fori_loop(0, TT_COMB, issue, 0)

    def finish(slot):
        pltpu.make_async_copy(buf_ref.at[slot], buf_ref.at[slot], sem.at[slot]).wait()
        acc = [None, None]
        for k in range(TOP_K):
            w = jnp.broadcast_to(wt_ref[:, k:k + 1], (TT_COMB, LANES))
            w = jnp.broadcast_to(w[:, None, :], (TT_COMB, XW, LANES)).reshape(TT_COMB * XW, LANES)
            for half in range(2):
                term = pltpu.unpack_elementwise(buf_ref[slot, k], index=half, packed_dtype=BF16,
                                                unpacked_dtype=F32) * w
                acc[half] = term if acc[half] is None else acc[half] + term
        for half in range(2):
            acc_ref[half] = acc[half]
        for half in range(2):
            for c in range(XW):
                f0 = half * (D_MODEL // 2) + c * LANES
                o_ref[:, f0:f0 + LANES] = (x1_ref[:, f0:f0 + LANES]
                                           + acc_ref[half, pl.ds(c, TT_COMB, stride=XW), :])

    @pl.when(i == 0)
    def _():
        gather(dest_ref, 0)

    for slot in range(2):
        @pl.when((i & 1) == slot)
        def _(slot=slot):
            @pl.when(i + 1 < n)
            def _():
                gather(dnext_ref, 1 - slot)
            finish(slot)


def _combine(dest, wt_tok, x1, y):
    T = x1.shape[0]
    nT = T // TT_COMB
    return pl.pallas_call(
        _combine_kernel,
        grid=(nT,),
        in_specs=[
            pl.BlockSpec((TOP_K, TT_COMB), lambda i: (0, i), memory_space=pltpu.SMEM),
            pl.BlockSpec((TOP_K, TT_COMB), lambda i: (0, jnp.minimum(i + 1, nT - 1)),
                         memory_space=pltpu.SMEM),
            pl.BlockSpec((TT_COMB, TOP_K), lambda i: (i, 0)),
            pl.BlockSpec((TT_COMB, D_MODEL), lambda i: (i, 0)),
            pl.BlockSpec(memory_space=pl.ANY),
        ],
        out_specs=pl.BlockSpec((TT_COMB, D_MODEL), lambda i: (i, 0)),
        out_shape=jax.ShapeDtypeStruct((T, D_MODEL), F32),
        scratch_shapes=[pltpu.VMEM((2, TOP_K, TT_COMB * XW, LANES), U32),
                        pltpu.VMEM((2, TT_COMB * XW, LANES), F32), pltpu.SemaphoreType.DMA((2,))],
        compiler_params=_cparams(1),
        name="combine",
    )(dest, dest, wt_tok, x1, y)


def _rope_tables(pos):
    half = ROT_DIM // 2
    inv_freq = ROPE_THETA ** (-jnp.arange(0, ROT_DIM, 2, dtype=F32) / ROT_DIM)
    ang = pos[:, None] * inv_freq[None, :]
    cos, sin = jnp.cos(ang), jnp.sin(ang)
    n = pos.shape[0]
    ones = jnp.ones((n, LANES - ROT_DIM), F32)
    zeros = jnp.zeros((n, LANES - ROT_DIM), F32)
    zh = jnp.zeros((n, half), F32)
    return (jnp.concatenate([cos, cos, ones], axis=1),
            jnp.concatenate([-sin, zh, zeros], axis=1),
            jnp.concatenate([zh, sin, zeros], axis=1))


def _permute_w_in(w_in):
    kv_end = NSA_WIDTH + N_KV_COLS * HEAD_DIM
    n_gate = NSA_HEADS * N_BRANCH
    per_g = HEADS_PER_GROUP * N_BRANCH
    gate = w_in[:, kv_end:kv_end + n_gate]
    pieces = [w_in[:, :kv_end], w_in[:, kv_end + n_gate:]]
    zpad = jnp.zeros((D_MODEL, GATE_PAD - per_g), w_in.dtype)
    for g in range(NSA_KV_HEADS):
        pieces += [gate[:, g * per_g:(g + 1) * per_g], zpad]
    pieces.append(jnp.zeros((D_MODEL, LANES - NSA_KV_HEADS * GATE_PAD), w_in.dtype))
    return jnp.concatenate(pieces, axis=1).astype(BF16)


def _layer(x, norm1_g, w_in, q_norm_g, k_norm_cmp_g, k_norm_slc_g, k_norm_win_g,
           cmp_k_pos, cmp_k_w1, cmp_k_b1, cmp_k_w2, cmp_k_b2,
           cmp_v_pos, cmp_v_w1, cmp_v_b1, cmp_v_w2, cmp_v_b2,
           pool_w, pool_scale, w_out, norm2_g,
           w_router, b_router, w_gate_up, b_gate_up, w_down, b_down):
    B, S, _ = x.shape
    T = B * S
    assert S % TQ == 0 and TQ == TK == TM_IN and WINDOW == 2 * TK and S // SLC_BLK <= N_BLK_PAD
    assert (S // CMP_STRIDE) % LANES == 0 and T % TT_ROUTE == 0
    NC = S // CMP_STRIDE
    x2 = x.reshape(T, D_MODEL)
    row = lambda v: v.reshape(1, -1)

    cos, slo, shi = _rope_tables(jnp.arange(S, dtype=F32))
    qn, kvc, ks, vst, kw, vwt, gt, pool_in = _inproj(
        x2, row(norm1_g), _permute_w_in(w_in), cos, slo, shi,
        row(q_norm_g), row(k_norm_slc_g), row(k_norm_win_g), B, S)

    c_end = (jnp.arange(NC, dtype=I32) * CMP_STRIDE + (CMP_BLK - 1)).astype(F32)
    ccos, cslo, cshi = _rope_tables(c_end)
    cw = CMP_STRIDE * HEAD_DIM
    kc = _compress(kvc, 0, cmp_k_pos.reshape(2, cw), cmp_k_w1.reshape(2, cw, CMP_HIDDEN).astype(BF16),
                   row(cmp_k_b1), cmp_k_w2.astype(BF16), row(cmp_k_b2), B, S,
                   rope=(row(k_norm_cmp_g), ccos, cslo, cshi))
    vct = _compress(kvc, 1, cmp_v_pos.reshape(2, cw), cmp_v_w1.reshape(2, cw, CMP_HIDDEN).astype(BF16),
                    row(cmp_v_b1), cmp_v_w2.astype(BF16), row(cmp_v_b2), B, S)

    nsa_out = _nsa(qn, kc, vct, ks, vst, kw, vwt, gt, B, S)

    wr_pad = jnp.pad(w_router, ((0, 0), (0, LANES - N_EXPERTS)))
    wr_hi = wr_pad.astype(BF16)
    wr_lo = (wr_pad - wr_hi.astype(F32)).astype(BF16)
    br_pad = jnp.concatenate([b_router.astype(F32), jnp.full((LANES - N_EXPERTS,), MASK_VALUE, F32)])
    x1, h2p, logits = _mix(nsa_out, pool_in, x2, w_out.astype(BF16), pool_w.astype(BF16),
                           row(pool_scale), row(norm2_g), jnp.concatenate([wr_hi, wr_lo], axis=1),
                           row(br_pad), S)

    n_tiles = T * TOP_K // TM_G + N_EXPERTS
    dest, wts, meta = _route(logits, n_tiles)
    te, tv, nx = meta[0, :n_tiles], meta[1, :n_tiles], meta[2, :n_tiles]
    xs = _dispatch(meta[3, :N_EXPERTS + 1], dest, h2p, n_tiles * TM_G)
    act = _gemm1(te, tv, nx, xs, w_gate_up, b_gate_up.reshape(N_EXPERTS, 1, 2 * D_FF), n_tiles)
    y = _gemm2(te, tv, nx, act, w_down, b_down.reshape(N_EXPERTS, 1, D_MODEL), n_tiles)
    out = _combine(dest, wts.T, x1, y)
    return out.reshape(B, S, D_MODEL)


def kernel(x, norm1_g, w_in, q_norm_g, k_norm_cmp_g, k_norm_slc_g, k_norm_win_g, cmp_k_pos, cmp_k_w1, cmp_k_b1, cmp_k_w2, cmp_k_b2, cmp_v_pos, cmp_v_w1, cmp_v_b1, cmp_v_w2, cmp_v_b2, pool_w, pool_scale, w_out, norm2_g, w_router, b_router, w_gate_up, b_gate_up, w_down, b_down):
    params = (norm1_g, w_in, q_norm_g, k_norm_cmp_g, k_norm_slc_g, k_norm_win_g,
              cmp_k_pos, cmp_k_w1, cmp_k_b1, cmp_k_w2, cmp_k_b2,
              cmp_v_pos, cmp_v_w1, cmp_v_b1, cmp_v_w2, cmp_v_b2,
              pool_w, pool_scale, w_out, norm2_g,
              w_router, b_router, w_gate_up, b_gate_up, w_down, b_down)
    depth = norm1_g.shape[0]
    for l in range(depth):
        x = _layer(x, *[p.reshape(p.shape[1:]) if depth == 1 else p[l] for p in params])
    return x
```

```python
import functools

import jax
import jax.numpy as jnp
from jax import lax
from jax.experimental import pallas as pl
from jax.experimental.pallas import tpu as pltpu

F32 = jnp.float32
BF16 = jnp.bfloat16
I32 = jnp.int32
U32 = jnp.uint32

D_MODEL = 2048
HEAD_DIM = 128
NSA_HEADS = 8
NSA_KV_HEADS = 2
HEADS_PER_GROUP = NSA_HEADS // NSA_KV_HEADS
NSA_WIDTH = NSA_HEADS * HEAD_DIM
N_BRANCH = 3
CMP_BLK = 32
CMP_STRIDE = 16
CMP_HIDDEN = 256
SLC_BLK = 64
SLC_SHIFT = SLC_BLK.bit_length() - 1
N_SEL = 16
WINDOW = 512
ROPE_THETA = 500000.0
ROT_DIM = HEAD_DIM // 4
POOL_WIDTH = 1024
POOL_WINDOWS = (2, 4, 8, 16)
POOL_GC = 256
N_EXPERTS = 32
TOP_K = 4
D_FF = 2048
SWIGLU_ALPHA = 1.702
SWIGLU_LIMIT = 7.0
RMS_EPS = 1e-5
QK_EPS = 1e-6
MASK_VALUE = -1e30
FORCE_VALUE = 1e30
LOG2_E = 1.4426950408889634

LANES = 128
N_KV_COLS = 6 * NSA_KV_HEADS
GATE_PAD = 16
N_BLK_PAD = 128
POOL_HALO = 16

TM_IN = 256
TQ = 256
TK = 256
SEL_UNROLL = 4
TM_MIX = 256
MIX_SUB = 256
TT_ROUTE = 1024
TM_G = 256
TN_G1 = 1024
TN_G2 = 2048
XW = D_MODEL // 2 // LANES
TT_DISP = 256
TT_COMB = 256
VMEM_LIMIT = 56 * 1024 * 1024


def _cparams(n_axes, vmem=VMEM_LIMIT):
    return pltpu.CompilerParams(
        dimension_semantics=("arbitrary",) * n_axes, vmem_limit_bytes=vmem)


def _dot(a, b):
    return jnp.dot(a, b, preferred_element_type=F32)


def _slab(ref, token, n_tokens, width):
    first = pl.multiple_of(token * width, width)
    return ref.at[pl.ds(first, n_tokens * width), :]


def _dot_nt(a, b):
    return lax.dot_general(a, b, (((1,), (1,)), ((), ())), preferred_element_type=F32)


def _inproj_kernel(x_ref, g_ref, w_ref, cos_ref, slo_ref, shi_ref, qg_ref, ksg_ref, kwg_ref,
                   qn_ref, kvc_ref, ks_ref, vst_ref, kw_ref, vwt_ref, gt_ref, pool_ref, *, nS):
    x = x_ref[...]
    y = x * lax.rsqrt(jnp.mean(x * x, axis=-1, keepdims=True) + RMS_EPS)
    h = (y * g_ref[...]).astype(BF16)
    cos, slo, shi = cos_ref[...], slo_ref[...], shi_ref[...]
    G = NSA_KV_HEADS
    head = lambda z, c: z[:, c * HEAD_DIM:(c + 1) * HEAD_DIM]

    scale = HEAD_DIM ** -0.5 * LOG2_E
    zq = _dot(h, w_ref[:, 0:NSA_WIDTH])
    for c in range(NSA_HEADS):
        qn_ref[:, c * HEAD_DIM:(c + 1) * HEAD_DIM] = (
            _norm_rope(head(zq, c), qg_ref[...], cos, slo, shi) * scale).astype(BF16)

    pair = lambda p: _dot(h, w_ref[:, NSA_WIDTH + p * G * HEAD_DIM:NSA_WIDTH + (p + 1) * G * HEAD_DIM])
    zkc, zvc = pair(0), pair(1)
    for g in range(G):
        kvc_ref[g] = head(zkc, g)
        kvc_ref[G + g] = head(zvc, g)
    zks, zvs, zkw, zvw = pair(2), pair(3), pair(4), pair(5)
    row = lax.rem(pl.program_id(0), nS) * TM_IN + lax.broadcasted_iota(I32, (TM_IN, N_BLK_PAD), 0)
    lane = lax.broadcasted_iota(I32, (TM_IN, N_BLK_PAD), 1)
    onehot = jnp.where((row >> SLC_SHIFT) == lane, 1.0, 0.0).astype(BF16)
    for g in range(G):
        ks_ref[g, :, 0:HEAD_DIM] = _norm_rope(head(zks, g), ksg_ref[...], cos, slo, shi).astype(BF16)
        ks_ref[g, :, HEAD_DIM:HEAD_DIM + N_BLK_PAD] = onehot
        vst_ref[g, 0] = head(zvs, g).T.astype(BF16)
        kw_ref[g] = _norm_rope(head(zkw, g), kwg_ref[...], cos, slo, shi).astype(BF16)
        vwt_ref[g, 0] = head(zvw, g).T.astype(BF16)

    base = NSA_WIDTH + N_KV_COLS * HEAD_DIM
    pool_ref[...] = _dot(h, w_ref[:, base:base + POOL_WIDTH])
    sig_t = jax.nn.sigmoid(_dot(h, w_ref[:, base + POOL_WIDTH:base + POOL_WIDTH + LANES])).T
    for g in range(G):
        gt_ref[g] = sig_t[g * GATE_PAD:(g + 1) * GATE_PAD]


def _inproj(x2, g1, w_perm, cos, slo, shi, qg, ksg, kwg, B, S):
    T = x2.shape[0]
    n_cols = w_perm.shape[1]
    nS = S // TM_IN
    G = NSA_KV_HEADS
    full = lambda shape: pl.BlockSpec(shape, lambda i: (0,) * len(shape))
    tab = pl.BlockSpec((TM_IN, LANES), lambda i: (lax.rem(i, nS), 0))
    bi = lambda i: (i // nS, lax.rem(i, nS))
    return pl.pallas_call(
        functools.partial(_inproj_kernel, nS=nS),
        grid=(T // TM_IN,),
        in_specs=[
            pl.BlockSpec((TM_IN, D_MODEL), lambda i: (i, 0)),
            full((1, D_MODEL)),
            full((D_MODEL, n_cols)),
            tab, tab, tab,
            full((1, HEAD_DIM)), full((1, HEAD_DIM)), full((1, HEAD_DIM)),
        ],
        out_specs=[
            pl.BlockSpec((TM_IN, NSA_WIDTH), lambda i: (i, 0)),
            pl.BlockSpec((2 * G, TM_IN, LANES), lambda i: (0, i, 0)),
            pl.BlockSpec((G, TM_IN, 2 * HEAD_DIM), lambda i: (*bi(i), 0)),
            pl.BlockSpec((G, 1, HEAD_DIM, TM_IN), lambda i: (*bi(i), 0, 0)),
            pl.BlockSpec((G, TM_IN, HEAD_DIM), lambda i: (*bi(i), 0)),
            pl.BlockSpec((G, 1, HEAD_DIM, TM_IN), lambda i: (*bi(i), 0, 0)),
            pl.BlockSpec((G, GATE_PAD, TM_IN), lambda i: (i // nS, 0, lax.rem(i, nS))),
            pl.BlockSpec((TM_IN, POOL_WIDTH), lambda i: (i, 0)),
        ],
        out_shape=[
            jax.ShapeDtypeStruct((T, NSA_WIDTH), BF16),
            jax.ShapeDtypeStruct((2 * G, T, LANES), F32),
            jax.ShapeDtypeStruct((B * G, S, 2 * HEAD_DIM), BF16),
            jax.ShapeDtypeStruct((B * G, nS, HEAD_DIM, TM_IN), BF16),
            jax.ShapeDtypeStruct((B * G, S, HEAD_DIM), BF16),
            jax.ShapeDtypeStruct((B * G, nS, HEAD_DIM, TM_IN), BF16),
            jax.ShapeDtypeStruct((B * G, GATE_PAD, S), F32),
            jax.ShapeDtypeStruct((T, POOL_WIDTH), F32),
        ],
        compiler_params=_cparams(1),
        name="inproj",
    )(x2, g1, w_perm, cos, slo, shi, qg, ksg, kwg)


def _norm_rope(x, g, cos, sin_lo, sin_hi):
    half = ROT_DIM // 2
    y = x * lax.rsqrt(jnp.mean(x * x, axis=-1, keepdims=True) + QK_EPS) * g
    return (y * cos + pltpu.roll(y, LANES - half, 1) * sin_lo
            + pltpu.roll(y, half, 1) * sin_hi)


def _compress_body(x_ref, pos_ref, w1_ref, b1_ref, w2_ref, b2_ref):
    nc = x_ref.shape[1] // CMP_STRIDE
    pair_w = 2 * HEAD_DIM
    first = jnp.zeros((nc, CMP_HIDDEN), F32)
    second = jnp.zeros((nc, CMP_HIDDEN), F32)
    for p in range(CMP_STRIDE // 2):
        x = jnp.concatenate([x_ref[0, pl.ds(2 * p + u, nc, stride=CMP_STRIDE), :] for u in range(2)], axis=1)
        cols = slice(p * pair_w, (p + 1) * pair_w)
        first = first + _dot((x + pos_ref[0:1, cols]).astype(BF16), w1_ref[0, cols, :])
        second = second + _dot((x + pos_ref[1:2, cols]).astype(BF16), w1_ref[1, cols, :])
    nxt = pltpu.roll(second, nc - 1, 0)
    hid = jax.nn.gelu(first + nxt + b1_ref[...])
    return _dot(hid.astype(BF16), w2_ref[...]) + b2_ref[...]


def _compress_k_kernel(x_ref, pos_ref, w1_ref, b1_ref, w2_ref, b2_ref,
                       g_ref, cos_ref, slo_ref, shi_ref, o_ref):
    out = _compress_body(x_ref, pos_ref, w1_ref, b1_ref, w2_ref, b2_ref)
    o_ref[0] = _norm_rope(out, g_ref[...], cos_ref[...], slo_ref[...], shi_ref[...]).astype(BF16)


def _compress_v_kernel(x_ref, pos_ref, w1_ref, b1_ref, w2_ref, b2_ref, o_ref):
    out = _compress_body(x_ref, pos_ref, w1_ref, b1_ref, w2_ref, b2_ref)
    o_ref[0] = out.T.astype(BF16)


def _compress(kvc, which, pos2, w1, b1, w2, b2, B, S, rope=None):
    NC = S // CMP_STRIDE
    G = NSA_KV_HEADS
    CW = CMP_STRIDE * HEAD_DIM
    common_specs = [
        pl.BlockSpec((1, S, HEAD_DIM), lambda b, g: (which * G + g, b, 0)),
        pl.BlockSpec((2, CW), lambda b, g: (0, 0)),
        pl.BlockSpec((2, CW, CMP_HIDDEN), lambda b, g: (0, 0, 0)),
        pl.BlockSpec((1, CMP_HIDDEN), lambda b, g: (0, 0)),
        pl.BlockSpec((CMP_HIDDEN, HEAD_DIM), lambda b, g: (0, 0)),
        pl.BlockSpec((1, HEAD_DIM), lambda b, g: (0, 0)),
    ]
    if rope is not None:
        gk, cos, slo, shi = rope
        tab = pl.BlockSpec((NC, LANES), lambda b, g: (0, 0))
        return pl.pallas_call(
            _compress_k_kernel,
            grid=(B, G),
            in_specs=common_specs + [pl.BlockSpec((1, HEAD_DIM), lambda b, g: (0, 0)), tab, tab, tab],
            out_specs=pl.BlockSpec((1, NC, HEAD_DIM), lambda b, g: (b * G + g, 0, 0)),
            out_shape=jax.ShapeDtypeStruct((B * G, NC, HEAD_DIM), BF16),
            compiler_params=_cparams(2),
            name="compress_k",
        )(kvc, pos2, w1, b1, w2, b2, gk, cos, slo, shi)
    return pl.pallas_call(
        _compress_v_kernel,
        grid=(B, G),
        in_specs=common_specs,
        out_specs=pl.BlockSpec((1, HEAD_DIM, NC), lambda b, g: (b * G + g, 0, 0)),
        out_shape=jax.ShapeDtypeStruct((B * G, HEAD_DIM, NC), BF16),
        compiler_params=_cparams(2),
        name="compress_v",
    )(kvc, pos2, w1, b1, w2, b2)


BIAS_DIAG, BIAS_FAR, BIAS_ALL, BIAS_NONE = 0, 1, 2, 3


def _tile_bias_table():
    k = lax.broadcasted_iota(I32, (TK, TQ), 0)
    t = lax.broadcasted_iota(I32, (TK, TQ), 1)
    neg = jnp.full((TK, TQ), MASK_VALUE, F32)
    zero = jnp.zeros((TK, TQ), F32)
    return jnp.stack([jnp.where(k <= t, zero, neg), jnp.where(k > t, zero, neg), neg, zero])


def _add_tile_bias(s, b):
    return jnp.concatenate([s[:, r * TQ:(r + 1) * TQ] + b for r in range(HEADS_PER_GROUP)], axis=1)


def _nsa_kernel(q_ref, kc_ref, vct_ref, ks_ref, vst_ref, kw_ref, vwt_ref, gt_ref, bias_ref, o_ref,
                qaug_ref, s_ref, cm_ref, m_ref, l_ref, acc_ref, oc_ref, ow_ref, *, S):
    R = HEADS_PER_GROUP
    NQ = R * TQ
    NC = S // CMP_STRIDE
    qi = pl.program_id(2)
    qs = qi * TQ

    q = q_ref[...].astype(F32)
    q_t = jnp.concatenate([q[:, r * HEAD_DIM:(r + 1) * HEAD_DIM].T for r in range(R)], axis=1).astype(BF16)
    col = lax.broadcasted_iota(I32, (1, NQ), 1)
    t_row = qs + (col & (TQ - 1))

    s_c = _dot(kc_ref[0], q_t)
    c_end = lax.broadcasted_iota(I32, (NC, 1), 0) * CMP_STRIDE + (CMP_BLK - 1)
    s_c = jnp.where(c_end <= t_row, s_c, MASK_VALUE)
    e_c = jnp.exp2(s_c - jnp.max(s_c, axis=0, keepdims=True))
    l_c = jnp.sum(e_c, axis=0, keepdims=True)
    p_c = e_c * jnp.where(t_row >= CMP_BLK - 1, 1.0 / l_c, 0.0)
    oc_ref[...] = _dot(vct_ref[0], p_c.astype(BF16))

    imp = p_c[:, 0:TQ]
    for r in range(1, R):
        imp = imp + p_c[:, r * TQ:(r + 1) * TQ]
    jj = lax.broadcasted_iota(I32, (N_BLK_PAD, NC), 0) * (SLC_BLK // CMP_STRIDE)
    nn = lax.broadcasted_iota(I32, (N_BLK_PAD, NC), 1)
    per = SLC_BLK // CMP_STRIDE
    fold = (jnp.where((nn >= jj) & (nn < jj + per), 1.0, 0.0)
            + jnp.where((nn >= jj - 1) & (nn < jj + per - 1), 1.0, 0.0)).astype(BF16)
    imp_hi = imp.astype(BF16)
    imp_lo = (imp - imp_hi.astype(F32)).astype(BF16)
    blk_score = _dot(fold, imp_hi) + _dot(fold, imp_lo)

    jb = lax.broadcasted_iota(I32, (N_BLK_PAD, TQ), 0)
    cur = (qs + lax.broadcasted_iota(I32, (N_BLK_PAD, TQ), 1)) >> SLC_SHIFT
    forced = (jb == 0) | (jb == cur) | (jb == cur - 1)
    val = jnp.where(forced, FORCE_VALUE, jnp.where(jb <= cur, blk_score, MASK_VALUE))
    jbf = jb.astype(F32)
    bias = jnp.where(forced, 0.0, MASK_VALUE)
    val = jnp.where(forced, -jnp.inf, val)
    for _ in range(min(N_SEL, S // SLC_BLK) - 3):
        mx = jnp.max(val, axis=0, keepdims=True)
        first = jnp.min(jnp.where(val == mx, jbf, float(N_BLK_PAD)), axis=0, keepdims=True)
        pick = jbf == first
        bias = jnp.where(pick, 0.0, bias)
        val = jnp.where(pick, -jnp.inf, val)
    qaug_ref[0:HEAD_DIM, :] = q_t
    qaug_ref[HEAD_DIM:HEAD_DIM + N_BLK_PAD, :] = jnp.concatenate([bias.astype(BF16)] * R, axis=1)

    w_tiles = ((jnp.maximum(qi - 2, 0), jnp.where(qi >= 2, BIAS_FAR, BIAS_ALL)),
               (jnp.maximum(qi - 1, 0), jnp.where(qi >= 1, BIAS_NONE, BIAS_ALL)),
               (qi, BIAS_DIAG))
    s_w = []
    for kt, bi in w_tiles:
        kwt = kw_ref[0, pl.ds(pl.multiple_of(kt * TK, TK), TK), :]
        s_w.append(_add_tile_bias(_dot(kwt, q_t), bias_ref[bi]))
    m_w = jnp.max(s_w[0], axis=0, keepdims=True)
    for s in s_w[1:]:
        m_w = jnp.maximum(m_w, jnp.max(s, axis=0, keepdims=True))
    l_w = jnp.zeros((1, NQ), F32)
    o_w = jnp.zeros((HEAD_DIM, NQ), F32)
    for (kt, _), s in zip(w_tiles, s_w):
        e = jnp.exp2(s - m_w)
        l_w = l_w + jnp.sum(e, axis=0, keepdims=True)
        o_w = o_w + _dot(vwt_ref[0, kt], e.astype(BF16))
    ow_ref[...] = o_w * (1.0 / l_w)

    m_ref[...] = jnp.full((1, NQ), MASK_VALUE, F32)
    l_ref[...] = jnp.zeros((1, NQ), F32)
    acc_ref[...] = jnp.zeros((HEAD_DIM, NQ), F32)

    def produce(kj, slot, causal):
        k = ks_ref[0, pl.ds(pl.multiple_of(kj * TK, TK), TK), :]
        s = _dot(k, qaug_ref[...])
        if causal:
            s = _add_tile_bias(s, bias_ref[BIAS_DIAG])
        s_ref[slot] = s
        cm_ref[slot] = jnp.max(s, axis=0, keepdims=True)

    def consume(kj, slot):
        m_old = m_ref[...]
        m_new = jnp.maximum(m_old, cm_ref[slot])
        alpha = jnp.exp2(m_old - m_new)
        p = jnp.exp2(s_ref[slot] - m_new)
        l_ref[...] = alpha * l_ref[...] + jnp.sum(p, axis=0, keepdims=True)
        acc_ref[...] = alpha * acc_ref[...] + _dot(vst_ref[0, kj], p.astype(BF16))
        m_ref[...] = m_new

    def stage(kj, slot, causal_next):
        produce(kj + 1, 1 - slot, causal_next)
        consume(kj, slot)

    @pl.when(qi == 0)
    def _():
        produce(0, 0, True)

    @pl.when(qi > 0)
    def _():
        produce(0, 0, False)

    def stage_group(pp, carry):
        for u in range(SEL_UNROLL):
            stage(SEL_UNROLL * pp + u, u & 1, False)
        return carry

    n_plain = jnp.maximum(qi - 1, 0)
    n_grouped = n_plain // SEL_UNROLL * SEL_UNROLL
    lax.fori_loop(0, n_plain // SEL_UNROLL, stage_group, 0)
    n_left = n_plain - n_grouped
    for u in range(0, SEL_UNROLL - 2, 2):
        @pl.when(n_left >= u + 2)
        def _(u=u):
            stage(n_grouped + u, 0, False)
            stage(n_grouped + u + 1, 1, False)

    @pl.when((n_left & 1) == 1)
    def _():
        stage(n_plain - 1, 0, False)

    @pl.when((qi >= 1) & ((n_plain & 1) == 0))
    def _():
        stage(qi - 1, 0, True)
        consume(qi, 1)

    @pl.when((qi >= 1) & ((n_plain & 1) == 1))
    def _():
        stage(qi - 1, 1, True)
        consume(qi, 0)

    @pl.when(qi == 0)
    def _():
        consume(0, 0)

    inv_l = 1.0 / l_ref[...]

    for r in range(R):
        sl = slice(r * TQ, (r + 1) * TQ)
        g0 = gt_ref[0, r * N_BRANCH + 0:r * N_BRANCH + 1, :]
        g1 = gt_ref[0, r * N_BRANCH + 1:r * N_BRANCH + 2, :]
        g2 = gt_ref[0, r * N_BRANCH + 2:r * N_BRANCH + 3, :]
        o = (g0 * oc_ref[:, sl] + (g1 * inv_l[:, sl]) * acc_ref[:, sl]
             + g2 * ow_ref[:, sl])
        o_ref[:, r * HEAD_DIM:(r + 1) * HEAD_DIM] = o.T.astype(BF16)


def _nsa(qn, kc, vct, ks, vst, kw, vwt, gt, B, S):
    G = NSA_KV_HEADS
    R = HEADS_PER_GROUP
    nQ = S // TQ
    nK = S // TK
    NC = S // CMP_STRIDE
    bg = lambda b, g, i: b * G + g
    return pl.pallas_call(
        functools.partial(_nsa_kernel, S=S),
        grid=(B, G, nQ),
        in_specs=[
            pl.BlockSpec((TQ, R * HEAD_DIM), lambda b, g, i: (b * nQ + i, g)),
            pl.BlockSpec((1, NC, HEAD_DIM), lambda b, g, i: (bg(b, g, i), 0, 0)),
            pl.BlockSpec((1, HEAD_DIM, NC), lambda b, g, i: (bg(b, g, i), 0, 0)),
            pl.BlockSpec((1, S, 2 * HEAD_DIM), lambda b, g, i: (bg(b, g, i), 0, 0)),
            pl.BlockSpec((1, nK, HEAD_DIM, TK), lambda b, g, i: (bg(b, g, i), 0, 0, 0)),
            pl.BlockSpec((1, S, HEAD_DIM), lambda b, g, i: (bg(b, g, i), 0, 0)),
            pl.BlockSpec((1, nK, HEAD_DIM, TK), lambda b, g, i: (bg(b, g, i), 0, 0, 0)),
            pl.BlockSpec((1, GATE_PAD, TQ), lambda b, g, i: (bg(b, g, i), 0, i)),
            pl.BlockSpec((4, TK, TQ), lambda b, g, i: (0, 0, 0)),
        ],
        out_specs=pl.BlockSpec((TQ, R * HEAD_DIM), lambda b, g, i: (b * nQ + i, g)),
        out_shape=jax.ShapeDtypeStruct((B * S, NSA_WIDTH), BF16),
        scratch_shapes=[
            pltpu.VMEM((2 * HEAD_DIM, R * TQ), BF16),
            pltpu.VMEM((2, TK, R * TQ), F32),
            pltpu.VMEM((2, 1, R * TQ), F32),
            pltpu.VMEM((1, R * TQ), F32),
            pltpu.VMEM((1, R * TQ), F32),
            pltpu.VMEM((HEAD_DIM, R * TQ), F32),
            pltpu.VMEM((HEAD_DIM, R * TQ), F32),
            pltpu.VMEM((HEAD_DIM, R * TQ), F32),
        ],
        compiler_params=_cparams(3),
        name="nsa",
    )(qn, kc, vct, ks, vst, kw, vwt, gt, _tile_bias_table())


def _mix_kernel(nsa_ref, pool_ref, halo_ref, x_ref, wo_ref, pw_ref, ps_ref, g2_ref,
                wr2_ref, br_ref, x1_ref, h2p_ref, logit_ref, ext_ref, *, S):
    nS = S // TM_MIX
    i = pl.program_id(0)
    t0 = lax.rem(i, nS) * TM_MIX
    ext_ref[0:POOL_HALO, :] = jnp.where(t0 == 0, 0.0, halo_ref[...])
    ext_ref[POOL_HALO:POOL_HALO + TM_MIX, :] = pool_ref[...]
    for sb in range(TM_MIX // MIX_SUB):
        r0 = sb * MIX_SUB
        rows = slice(r0, r0 + MIX_SUB)
        t = t0 + r0 + lax.broadcasted_iota(I32, (MIX_SUB, POOL_GC), 0)
        acc = _dot(nsa_ref[rows, :], wo_ref[0:NSA_WIDTH, :])
        for gi, w in enumerate(POOL_WINDOWS):
            cs = slice(gi * POOL_GC, (gi + 1) * POOL_GC)
            v = pool_ref[rows, cs]
            tot = v
            for k in range(1, w):
                tot = tot + ext_ref[POOL_HALO + r0 - k:POOL_HALO + r0 - k + MIX_SUB, cs]
            cnt = jnp.minimum(t + 1, w).astype(F32)
            d = tot / cnt - v
            y = _dot(d.astype(BF16), pw_ref[gi]) * ps_ref[:, cs]
            acc = acc + _dot(y.astype(BF16),
                             wo_ref[NSA_WIDTH + gi * POOL_GC:NSA_WIDTH + (gi + 1) * POOL_GC, :])
        x1 = x_ref[rows, :] + acc
        x1_ref[rows, :] = x1
        h2 = x1 * lax.rsqrt(jnp.mean(x1 * x1, axis=-1, keepdims=True) + RMS_EPS) * g2_ref[...]
        half = D_MODEL // 2
        packed = pltpu.pack_elementwise([h2[:, 0:half], h2[:, half:D_MODEL]], packed_dtype=BF16)
        for c in range(XW):
            h2p_ref[pl.ds(r0 * XW + c, MIX_SUB, stride=XW), :] = packed[:, c * LANES:(c + 1) * LANES]
        hi = h2.astype(BF16)
        lo = (h2 - hi.astype(F32)).astype(BF16)
        hl = _dot(hi, wr2_ref[...])
        logit_ref[rows, :] = (hl[:, 0:LANES] + hl[:, LANES:2 * LANES] + _dot(lo, wr2_ref[:, 0:LANES])
                              + br_ref[...])


def _mix(nsa_out, pool_in, x2, wo, pw, ps, g2, wr2, br, S):
    T = x2.shape[0]
    hb = TM_MIX // POOL_HALO
    full = lambda shape: pl.BlockSpec(shape, lambda i: (0,) * len(shape))
    return pl.pallas_call(
        functools.partial(_mix_kernel, S=S),
        grid=(T // TM_MIX,),
        in_specs=[
            pl.BlockSpec((TM_MIX, NSA_WIDTH), lambda i: (i, 0)),
            pl.BlockSpec((TM_MIX, POOL_WIDTH), lambda i: (i, 0)),
            pl.BlockSpec((POOL_HALO, POOL_WIDTH), lambda i: (jnp.maximum(i * hb - 1, 0), 0)),
            pl.BlockSpec((TM_MIX, D_MODEL), lambda i: (i, 0)),
            full((D_MODEL, D_MODEL)),
            full((len(POOL_WINDOWS), POOL_GC, POOL_GC)),
            full((1, POOL_WIDTH)),
            full((1, D_MODEL)),
            full((D_MODEL, 2 * LANES)),
            full((1, LANES)),
        ],
        out_specs=[
            pl.BlockSpec((TM_MIX, D_MODEL), lambda i: (i, 0)),
            pl.BlockSpec((TM_MIX * XW, LANES), lambda i: (i, 0)),
            pl.BlockSpec((TM_MIX, LANES), lambda i: (i, 0)),
        ],
        out_shape=[
            jax.ShapeDtypeStruct((T, D_MODEL), F32),
            jax.ShapeDtypeStruct((T * XW, LANES), U32),
            jax.ShapeDtypeStruct((T, LANES), F32),
        ],
        scratch_shapes=[pltpu.VMEM((POOL_HALO + TM_MIX, POOL_WIDTH), F32)],
        compiler_params=_cparams(1),
        name="mix",
    )(nsa_out, pool_in, pool_in, x2, wo, pw, ps, g2, wr2, br)


def _route_kernel(logit_ref, dest_ref, wt_ref, meta_ref, cnt_ref, run_ref, tri_ref, *, n_tiles):
    phase = pl.program_id(0)
    i = pl.program_id(1)
    TT = TT_ROUTE
    E = N_EXPERTS

    @pl.when((phase == 0) & (i == 0))
    def _():
        cnt_ref[...] = jnp.zeros_like(cnt_ref)
        run_ref[...] = jnp.zeros_like(run_ref)

    lt = logit_ref[...].T[0:E, :]
    ef = lax.broadcasted_iota(I32, (E, TT), 0).astype(F32)
    work = lt
    ids, vals, hots = [], [], []
    for _ in range(TOP_K):
        mx = jnp.max(work, axis=0, keepdims=True)
        eid = jnp.min(jnp.where(work == mx, ef, float(E)), axis=0, keepdims=True)
        hot = ef == eid
        ids.append(eid)
        vals.append(mx)
        hots.append(hot)
        work = jnp.where(hot, -jnp.inf, work)
    member = jnp.where(hots[0] | hots[1] | hots[2] | hots[3], 1.0, 0.0)
    tile_cnt = jnp.sum(member, axis=1, keepdims=True)

    @pl.when(phase == 0)
    def _():
        cnt_ref[...] = cnt_ref[...] + tile_cnt

    @pl.when(phase == 1)
    def _():
        ex = [jnp.exp(v - vals[0]) for v in vals]
        den = ex[0] + ex[1] + ex[2] + ex[3]
        cnt = jnp.broadcast_to(cnt_ref[...], (E, LANES))
        padded = jnp.ceil(cnt * (1.0 / TM_G)) * TM_G
        erow = lax.broadcasted_iota(I32, (E, LANES), 0)
        start = jnp.zeros((E, LANES), F32)
        running = jnp.zeros((1, LANES), F32)
        for e in range(E):
            start = jnp.where(erow == e, running, start)
            running = running + padded[e:e + 1, :]
        end = start + padded
        @pl.when(i == 0)
        def _():
            tri_ref[...] = jnp.where(lax.broadcasted_iota(I32, (TT, TT), 0)
                                     < lax.broadcasted_iota(I32, (TT, TT), 1), 1.0, 0.0).astype(BF16)

        before = _dot(member.astype(BF16), tri_ref[...]) + run_ref[...]
        slot = before + start[:, 0:1]
        for k in range(TOP_K):
            d = jnp.sum(jnp.where(hots[k], slot, 0.0), axis=0, keepdims=True)
            dest_ref[k:k + 1, :] = d.astype(I32)
            wt_ref[k:k + 1, :] = ex[k] / den
        run_ref[...] = run_ref[...] + tile_cnt

        @pl.when(i == 0)
        def _():
            lanes = meta_ref.shape[1]
            endw = jnp.broadcast_to(end[:, 0:1], (E, lanes))
            tile_row = (lax.broadcasted_iota(I32, (E, lanes), 1) * TM_G).astype(F32)
            owner = jnp.minimum(jnp.sum(jnp.where(endw <= tile_row, 1.0, 0.0), axis=0, keepdims=True),
                                float(E - 1))
            nxt = jnp.full((E, LANES), -1.0, F32)
            later = jnp.full((1, LANES), -1.0, F32)
            for e in reversed(range(E)):
                nxt = jnp.where(erow == e, later, nxt)
                later = jnp.where(cnt[e:e + 1, :] > 0.0, float(e), later)
            erow_w = lax.broadcasted_iota(I32, (E, lanes), 0).astype(F32)
            nxt_tile = jnp.sum(jnp.where(erow_w == owner, jnp.broadcast_to(nxt[:, 0:1], (E, lanes)), 0.0),
                               axis=0, keepdims=True)
            meta_ref[0:1, :] = owner.astype(I32)
            meta_ref[1:2, :] = jnp.where(tile_row[0:1, :] < running[:, 0:1], 1, 0).astype(I32)
            meta_ref[2:3, :] = nxt_tile.astype(I32)
            lane_w = lax.broadcasted_iota(I32, (E, lanes), 1).astype(F32)
            ends = jnp.sum(jnp.where(erow_w == lane_w, endw, 0.0), axis=0, keepdims=True)
            ends = jnp.where(lane_w[0:1, :] == float(E), running[:, 0:1] * (1.0 / TM_G), ends)
            meta_ref[3:4, :] = ends.astype(I32)
            meta_ref[4:8, :] = jnp.zeros((4, lanes), I32)


def _route(logits, n_tiles):
    T = logits.shape[0]
    nT = T // TT_ROUTE
    lanes = -(-n_tiles // LANES) * LANES
    return pl.pallas_call(
        functools.partial(_route_kernel, n_tiles=n_tiles),
        grid=(2, nT),
        in_specs=[pl.BlockSpec((TT_ROUTE, LANES), lambda p, i: (i, 0))],
        out_specs=[
            pl.BlockSpec((TOP_K, TT_ROUTE), lambda p, i: (0, i * p)),
            pl.BlockSpec((TOP_K, TT_ROUTE), lambda p, i: (0, i * p)),
            pl.BlockSpec((8, lanes), lambda p, i: (0, 0)),
        ],
        out_shape=[
            jax.ShapeDtypeStruct((TOP_K, T), I32),
            jax.ShapeDtypeStruct((TOP_K, T), F32),
            jax.ShapeDtypeStruct((8, lanes), I32),
        ],
        scratch_shapes=[pltpu.VMEM((N_EXPERTS, 1), F32), pltpu.VMEM((N_EXPERTS, 1), F32),
                        pltpu.VMEM((TT_ROUTE, TT_ROUTE), BF16)],
        compiler_params=_cparams(2),
        name="route",
    )(logits)


def _dispatch_kernel(ends_ref, dest_ref, h_ref, xs_ref, zero_ref, sem, zsem):
    @pl.when(pl.program_id(0) == 0)
    def _():
        zero_ref[...] = jnp.zeros_like(zero_ref)

        def tail_copy(e):
            return pltpu.make_async_copy(zero_ref, _slab(xs_ref, ends_ref[e] - TM_G, TM_G, XW), zsem)

        def nonempty(e):
            return ends_ref[e] > (ends_ref[e - 1] if e else 0)

        for e in range(N_EXPERTS):
            @pl.when(nonempty(e))
            def _(e=e):
                tail_copy(e).start()
        for e in range(N_EXPERTS):
            @pl.when(nonempty(e))
            def _(e=e):
                tail_copy(e).wait()

        def unused_copy(tile):
            return pltpu.make_async_copy(zero_ref, _slab(xs_ref, tile * TM_G, TM_G, XW), zsem)

        n_used = ends_ref[N_EXPERTS]
        n_all = xs_ref.shape[0] // (TM_G * XW)

        def start_unused(tile, carry):
            unused_copy(tile).start()
            return carry

        def wait_unused(tile, carry):
            unused_copy(tile).wait()
            return carry

        lax.fori_loop(n_used, n_all, start_unused, 0)
        lax.fori_loop(n_used, n_all, wait_unused, 0)

    def issue(t, carry):
        for k in range(TOP_K):
            pltpu.make_async_copy(_slab(h_ref, t, 1, XW), _slab(xs_ref, dest_ref[k, t], 1, XW),
                                  sem).start(priority=k % 2)
        return carry

    lax.fori_loop(0, TT_DISP, issue, 0)
    all_rows = _slab(xs_ref, 0, TOP_K * TT_DISP, XW)
    pltpu.make_async_copy(all_rows, all_rows, sem).wait()


def _dispatch(ends, dest, h2p, n_rows):
    T = h2p.shape[0] // XW
    grid_spec = pltpu.PrefetchScalarGridSpec(
        num_scalar_prefetch=1,
        grid=(T // TT_DISP,),
        in_specs=[
            pl.BlockSpec((TOP_K, TT_DISP), lambda i, ends: (0, i), memory_space=pltpu.SMEM),
            pl.BlockSpec((TT_DISP * XW, LANES), lambda i, ends: (i, 0)),
        ],
        out_specs=pl.BlockSpec(memory_space=pl.ANY),
        scratch_shapes=[pltpu.VMEM((TM_G * XW, LANES), U32), pltpu.SemaphoreType.DMA,
                        pltpu.SemaphoreType.DMA],
    )
    return pl.pallas_call(
        _dispatch_kernel,
        grid_spec=grid_spec,
        out_shape=jax.ShapeDtypeStruct((n_rows * XW, LANES), U32),
        compiler_params=_cparams(1),
        name="dispatch",
    )(ends, dest, h2p)


def _expert_changed(te_ref, i):
    return (i == 0) | (te_ref[i] != te_ref[jnp.maximum(i - 1, 0)])


WEIGHT_DMA_PRIORITY = 1


def _stream_expert_weights(te_ref, tv_ref, nx_ref, copies, wst_ref, wbf_ref, slot_ref):
    j = pl.program_id(0)
    i = pl.program_id(1)

    @pl.when((j == 0) & (i == 0))
    def _():
        slot_ref[0] = 0
        for c in copies(te_ref[0], 0, 0):
            c.start(priority=WEIGHT_DMA_PRIORITY)

    @pl.when((tv_ref[i] > 0) & _expert_changed(te_ref, i))
    def _():
        slot = slot_ref[0]
        e_next = nx_ref[i]

        @pl.when(e_next >= 0)
        def _():
            for c in copies(e_next, j, 1 - slot):
                c.start(priority=WEIGHT_DMA_PRIORITY)

        @pl.when((e_next < 0) & (j + 1 < pl.num_programs(0)))
        def _():
            for c in copies(te_ref[0], j + 1, 1 - slot):
                c.start(priority=WEIGHT_DMA_PRIORITY)

        for c in copies(te_ref[i], j, slot):
            c.wait()
        slot_ref[1] = slot
        slot_ref[0] = 1 - slot


W_CHUNK = 256


def _expert_matmul(x, wst_ref, wbf_ref, slot_ref, refresh):
    if not refresh:
        return _dot(x, wbf_ref[...])
    acc = None
    for k in range(x.shape[1] // W_CHUNK):
        rows = slice(k * W_CHUNK, (k + 1) * W_CHUNK)
        wbf_ref[rows, :] = wst_ref[slot_ref[1], rows, :].astype(BF16)
        part = _dot(x[:, rows], wbf_ref[rows, :])
        acc = part if acc is None else acc + part
    return acc


def _gemm1_kernel(te_ref, tv_ref, nx_ref, x_ref, bg_ref, bu_ref, w_hbm, act_ref,
                  wst_ref, wbf_ref, slot_ref, sem):
    i = pl.program_id(1)
    valid = tv_ref[i] > 0
    changed = _expert_changed(te_ref, i)

    def copies(e, j, slot):
        col = pl.multiple_of(j * TN_G1, TN_G1)
        return (pltpu.make_async_copy(w_hbm.at[e, :, pl.ds(col, TN_G1)],
                                      wst_ref.at[slot, :, pl.ds(0, TN_G1)], sem.at[slot]),
                pltpu.make_async_copy(w_hbm.at[e, :, pl.ds(D_FF + col, TN_G1)],
                                      wst_ref.at[slot, :, pl.ds(TN_G1, TN_G1)], sem.at[slot]))

    _stream_expert_weights(te_ref, tv_ref, nx_ref, copies, wst_ref, wbf_ref, slot_ref)

    def compute(refresh):
        xp = jnp.concatenate([x_ref[pl.ds(c, TM_G, stride=XW), :] for c in range(XW)], axis=1)
        lo = pltpu.unpack_elementwise(xp, index=0, packed_dtype=BF16, unpacked_dtype=F32).astype(BF16)
        hi = pltpu.unpack_elementwise(xp, index=1, packed_dtype=BF16, unpacked_dtype=F32).astype(BF16)
        gu = _expert_matmul(jnp.concatenate([lo, hi], axis=1), wst_ref, wbf_ref, slot_ref, refresh)
        gate = jnp.minimum(gu[:, 0:TN_G1] + bg_ref[0], SWIGLU_LIMIT)
        up = jnp.clip(gu[:, TN_G1:2 * TN_G1] + bu_ref[0], -SWIGLU_LIMIT, SWIGLU_LIMIT)
        act = (up + 1.0) * gate * jax.nn.sigmoid(SWIGLU_ALPHA * gate)
        act_ref[...] = act.astype(BF16)

    @pl.when(valid & changed)
    def _():
        wbf_ref[...] = wst_ref[slot_ref[1]].astype(BF16)

    @pl.when(valid)
    def _():
        compute(False)

    @pl.when(jnp.logical_not(valid))
    def _():
        act_ref[...] = jnp.zeros_like(act_ref)


def _gemm1(te, tv, nx, xs, w_gu, b_gu3, n_tiles):
    n_rows = xs.shape[0] // XW
    nJ = D_FF // TN_G1
    grid_spec = pltpu.PrefetchScalarGridSpec(
        num_scalar_prefetch=3,
        grid=(nJ, n_tiles),
        in_specs=[
            pl.BlockSpec((TM_G * XW, LANES), lambda j, i, te, tv, nx: (i, 0)),
            pl.BlockSpec((1, 1, TN_G1), lambda j, i, te, tv, nx: (te[i], 0, j)),
            pl.BlockSpec((1, 1, TN_G1), lambda j, i, te, tv, nx: (te[i], 0, nJ + j)),
            pl.BlockSpec(memory_space=pl.ANY),
        ],
        out_specs=pl.BlockSpec((TM_G, TN_G1), lambda j, i, te, tv, nx: (i, j)),
        scratch_shapes=[
            pltpu.VMEM((2, D_MODEL, 2 * TN_G1), F32),
            pltpu.VMEM((D_MODEL, 2 * TN_G1), BF16),
            pltpu.SMEM((2,), I32),
            pltpu.SemaphoreType.DMA((2,)),
        ],
    )
    return pl.pallas_call(
        _gemm1_kernel,
        grid_spec=grid_spec,
        out_shape=jax.ShapeDtypeStruct((n_rows, D_FF), BF16),
        compiler_params=_cparams(2),
        name="gemm1",
    )(te, tv, nx, xs, b_gu3, b_gu3, w_gu)


def _gemm2_kernel(te_ref, tv_ref, nx_ref, a_ref, b_ref, w_hbm, y_ref, wst_ref, wbf_ref, slot_ref, sem):
    i = pl.program_id(1)
    valid = tv_ref[i] > 0

    def copies(e, j, slot):
        col = pl.multiple_of(j * TN_G2, TN_G2)
        return (pltpu.make_async_copy(w_hbm.at[e, :, pl.ds(col, TN_G2)], wst_ref.at[slot], sem.at[slot]),)

    _stream_expert_weights(te_ref, tv_ref, nx_ref, copies, wst_ref, wbf_ref, slot_ref)

    def compute(refresh):
        y = _expert_matmul(a_ref[...], wst_ref, wbf_ref, slot_ref, refresh) + b_ref[0]
        half = D_MODEL // 2
        packed = pltpu.pack_elementwise([y[:, 0:half], y[:, half:D_MODEL]], packed_dtype=BF16)
        for c in range(XW):
            y_ref[pl.ds(c, TM_G, stride=XW), :] = packed[:, c * LANES:(c + 1) * LANES]

    changed = _expert_changed(te_ref, i)

    @pl.when(valid & changed)
    def _():
        compute(True)

    @pl.when(valid & jnp.logical_not(changed))
    def _():
        compute(False)

    @pl.when(jnp.logical_not(valid))
    def _():
        zero = jnp.zeros(y_ref.shape, F32)
        y_ref[...] = pltpu.pack_elementwise([zero, zero], packed_dtype=BF16)


def _gemm2(te, tv, nx, act, w_d, b_d3, n_tiles):
    n_rows = act.shape[0]
    nJ = D_MODEL // TN_G2
    grid_spec = pltpu.PrefetchScalarGridSpec(
        num_scalar_prefetch=3,
        grid=(nJ, n_tiles),
        in_specs=[
            pl.BlockSpec((TM_G, D_FF), lambda j, i, te, tv, nx: (i, 0)),
            pl.BlockSpec((1, 1, TN_G2), lambda j, i, te, tv, nx: (te[i], 0, j)),
            pl.BlockSpec(memory_space=pl.ANY),
        ],
        out_specs=pl.BlockSpec((TM_G * XW, LANES), lambda j, i, te, tv, nx: (i, 0)),
        scratch_shapes=[
            pltpu.VMEM((2, D_FF, TN_G2), F32),
            pltpu.VMEM((D_FF, TN_G2), BF16),
            pltpu.SMEM((2,), I32),
            pltpu.SemaphoreType.DMA((2,)),
        ],
    )
    return pl.pallas_call(
        _gemm2_kernel,
        grid_spec=grid_spec,
        out_shape=jax.ShapeDtypeStruct((n_rows * XW, LANES), U32),
        compiler_params=_cparams(2),
        name="gemm2",
    )(te, tv, nx, act, b_d3, w_d)


def _combine_kernel(dest_ref, dnext_ref, wt_ref, x1_ref, y_ref, o_ref, buf_ref, sem):
    i = pl.program_id(0)
    n = pl.num_programs(0)

    def gather(d_ref, slot):
        def issue(t, carry):
            for k in range(TOP_K):
                pltpu.make_async_copy(_slab(y_ref, d_ref[k, t], 1, XW), _slab(buf_ref.at[slot, k], t, 1, XW),
                                      sem.at[slot]).start(priority=k % 2)
            return carry
        lax.fori_loop(0, TT_COMB, issue, 0)

    def finish(slot):
        pltpu.make_async_copy(buf_ref.at[slot], buf_ref.at[slot], sem.at[slot]).wait()
        wts = [jnp.broadcast_to(wt_ref[:, k:k + 1], (TT_COMB, LANES)) for k in range(TOP_K)]
        for c in range(XW):
            lo0, hi0 = c * LANES, D_MODEL // 2 + c * LANES
            acc_lo = x1_ref[:, lo0:lo0 + LANES]
            acc_hi = x1_ref[:, hi0:hi0 + LANES]
            for k in range(TOP_K):
                words = buf_ref[slot, k, pl.ds(c, TT_COMB, stride=XW), :]
                acc_lo = acc_lo + wts[k] * pltpu.unpack_elementwise(
                    words, index=0, packed_dtype=BF16, unpacked_dtype=F32)
                acc_hi = acc_hi + wts[k] * pltpu.unpack_elementwise(
                    words, index=1, packed_dtype=BF16, unpacked_dtype=F32)
            o_ref[:, lo0:lo0 + LANES] = acc_lo
            o_ref[:, hi0:hi0 + LANES] = acc_hi

    @pl.when(i == 0)
    def _():
        gather(dest_ref, 0)

    for slot in range(2):
        @pl.when((i & 1) == slot)
        def _(slot=slot):
            @pl.when(i + 1 < n)
            def _():
                gather(dnext_ref, 1 - slot)
            finish(slot)


def _combine(dest, wt_tok, x1, y):
    T = x1.shape[0]
    nT = T // TT_COMB
    return pl.pallas_call(
        _combine_kernel,
        grid=(nT,),
        in_specs=[
            pl.BlockSpec((TOP_K, TT_COMB), lambda i: (0, i), memory_space=pltpu.SMEM),
            pl.BlockSpec((TOP_K, TT_COMB), lambda i: (0, jnp.minimum(i + 1, nT - 1)),
                         memory_space=pltpu.SMEM),
            pl.BlockSpec((TT_COMB, TOP_K), lambda i: (i, 0)),
            pl.BlockSpec((TT_COMB, D_MODEL), lambda i: (i, 0)),
            pl.BlockSpec(memory_space=pl.ANY),
        ],
        out_specs=pl.BlockSpec((TT_COMB, D_MODEL), lambda i: (i, 0)),
        out_shape=jax.ShapeDtypeStruct((T, D_MODEL), F32),
        scratch_shapes=[pltpu.VMEM((2, TOP_K, TT_COMB * XW, LANES), U32), pltpu.SemaphoreType.DMA((2,))],
        compiler_params=_cparams(1),
        name="combine",
    )(dest, dest, wt_tok, x1, y)


def _rope_tables(pos):
    half = ROT_DIM // 2
    inv_freq = ROPE_THETA ** (-jnp.arange(0, ROT_DIM, 2, dtype=F32) / ROT_DIM)
    ang = pos[:, None] * inv_freq[None, :]
    cos, sin = jnp.cos(ang), jnp.sin(ang)
    n = pos.shape[0]
    ones = jnp.ones((n, LANES - ROT_DIM), F32)
    zeros = jnp.zeros((n, LANES - ROT_DIM), F32)
    zh = jnp.zeros((n, half), F32)
    return (jnp.concatenate([cos, cos, ones], axis=1),
            jnp.concatenate([-sin, zh, zeros], axis=1),
            jnp.concatenate([zh, sin, zeros], axis=1))


def _permute_w_in(w_in):
    kv_end = NSA_WIDTH + N_KV_COLS * HEAD_DIM
    n_gate = NSA_HEADS * N_BRANCH
    per_g = HEADS_PER_GROUP * N_BRANCH
    gate = w_in[:, kv_end:kv_end + n_gate]
    pieces = [w_in[:, :kv_end], w_in[:, kv_end + n_gate:]]
    zpad = jnp.zeros((D_MODEL, GATE_PAD - per_g), w_in.dtype)
    for g in range(NSA_KV_HEADS):
        pieces += [gate[:, g * per_g:(g + 1) * per_g], zpad]
    pieces.append(jnp.zeros((D_MODEL, LANES - NSA_KV_HEADS * GATE_PAD), w_in.dtype))
    return jnp.concatenate(pieces, axis=1).astype(BF16)


def _layer(x, norm1_g, w_in, q_norm_g, k_norm_cmp_g, k_norm_slc_g, k_norm_win_g,
           cmp_k_pos, cmp_k_w1, cmp_k_b1, cmp_k_w2, cmp_k_b2,
           cmp_v_pos, cmp_v_w1, cmp_v_b1, cmp_v_w2, cmp_v_b2,
           pool_w, pool_scale, w_out, norm2_g,
           w_router, b_router, w_gate_up, b_gate_up, w_down, b_down):
    B, S, _ = x.shape
    T = B * S
    assert S % TQ == 0 and TQ == TK == TM_IN and WINDOW == 2 * TK and S // SLC_BLK <= N_BLK_PAD
    assert (S // CMP_STRIDE) % LANES == 0 and T % TT_ROUTE == 0
    NC = S // CMP_STRIDE
    x2 = x.reshape(T, D_MODEL)
    row = lambda v: v.reshape(1, -1)

    cos, slo, shi = _rope_tables(jnp.arange(S, dtype=F32))
    qn, kvc, ks, vst, kw, vwt, gt, pool_in = _inproj(
        x2, row(norm1_g), _permute_w_in(w_in), cos, slo, shi,
        row(q_norm_g), row(k_norm_slc_g), row(k_norm_win_g), B, S)

    c_end = (jnp.arange(NC, dtype=I32) * CMP_STRIDE + (CMP_BLK - 1)).astype(F32)
    ccos, cslo, cshi = _rope_tables(c_end)
    cw = CMP_STRIDE * HEAD_DIM
    kc = _compress(kvc, 0, cmp_k_pos.reshape(2, cw), cmp_k_w1.reshape(2, cw, CMP_HIDDEN).astype(BF16),
                   row(cmp_k_b1), cmp_k_w2.astype(BF16), row(cmp_k_b2), B, S,
                   rope=(row(k_norm_cmp_g), ccos, cslo, cshi))
    vct = _compress(kvc, 1, cmp_v_pos.reshape(2, cw), cmp_v_w1.reshape(2, cw, CMP_HIDDEN).astype(BF16),
                    row(cmp_v_b1), cmp_v_w2.astype(BF16), row(cmp_v_b2), B, S)

    nsa_out = _nsa(qn, kc, vct, ks, vst, kw, vwt, gt, B, S)

    wr_pad = jnp.pad(w_router, ((0, 0), (0, LANES - N_EXPERTS)))
    wr_hi = wr_pad.astype(BF16)
    wr_lo = (wr_pad - wr_hi.astype(F32)).astype(BF16)
    br_pad = jnp.concatenate([b_router.astype(F32), jnp.full((LANES - N_EXPERTS,), MASK_VALUE, F32)])
    x1, h2p, logits = _mix(nsa_out, pool_in, x2, w_out.astype(BF16), pool_w.astype(BF16),
                           row(pool_scale), row(norm2_g), jnp.concatenate([wr_hi, wr_lo], axis=1),
                           row(br_pad), S)

    n_tiles = T * TOP_K // TM_G + N_EXPERTS
    dest, wts, meta = _route(logits, n_tiles)
    te, tv, nx = meta[0, :n_tiles], meta[1, :n_tiles], meta[2, :n_tiles]
    xs = _dispatch(meta[3, :N_EXPERTS + 1], dest, h2p, n_tiles * TM_G)
    act = _gemm1(te, tv, nx, xs, w_gate_up, b_gate_up.reshape(N_EXPERTS, 1, 2 * D_FF), n_tiles)
    y = _gemm2(te, tv, nx, act, w_down, b_down.reshape(N_EXPERTS, 1, D_MODEL), n_tiles)
    out = _combine(dest, wts.T, x1, y)
    return out.reshape(B, S, D_MODEL)


def kernel(x, norm1_g, w_in, q_norm_g, k_norm_cmp_g, k_norm_slc_g, k_norm_win_g, cmp_k_pos, cmp_k_w1, cmp_k_b1, cmp_k_w2, cmp_k_b2, cmp_v_pos, cmp_v_w1, cmp_v_b1, cmp_v_w2, cmp_v_b2, pool_w, pool_scale, w_out, norm2_g, w_router, b_router, w_gate_up, b_gate_up, w_down, b_down):
    params = (norm1_g, w_in, q_norm_g, k_norm_cmp_g, k_norm_slc_g, k_norm_win_g,
              cmp_k_pos, cmp_k_w1, cmp_k_b1, cmp_k_w2, cmp_k_b2,
              cmp_v_pos, cmp_v_w1, cmp_v_b1, cmp_v_w2, cmp_v_b2,
              pool_w, pool_scale, w_out, norm2_g,
              w_router, b_router, w_gate_up, b_gate_up, w_down, b_down)
    depth = norm1_g.shape[0]
    for l in range(depth):
        x = _layer(x, *[p.reshape(p.shape[1:]) if depth == 1 else p[l] for p in params])
    return x
```

```python
import functools

import jax
import jax.numpy as jnp
from jax import lax
from jax.experimental import pallas as pl
from jax.experimental.pallas import tpu as pltpu

F32 = jnp.float32
BF16 = jnp.bfloat16
I32 = jnp.int32
U32 = jnp.uint32

D_MODEL = 2048
HEAD_DIM = 128
NSA_HEADS = 8
NSA_KV_HEADS = 2
HEADS_PER_GROUP = NSA_HEADS // NSA_KV_HEADS
NSA_WIDTH = NSA_HEADS * HEAD_DIM
N_BRANCH = 3
CMP_BLK = 32
CMP_STRIDE = 16
CMP_HIDDEN = 256
SLC_BLK = 64
SLC_SHIFT = SLC_BLK.bit_length() - 1
N_SEL = 16
WINDOW = 512
ROPE_THETA = 500000.0
ROT_DIM = HEAD_DIM // 4
POOL_WIDTH = 1024
POOL_WINDOWS = (2, 4, 8, 16)
POOL_GC = 256
N_EXPERTS = 32
TOP_K = 4
D_FF = 2048
SWIGLU_ALPHA = 1.702
SWIGLU_LIMIT = 7.0
RMS_EPS = 1e-5
QK_EPS = 1e-6
MASK_VALUE = -1e30
FORCE_VALUE = 1e30
LOG2_E = 1.4426950408889634

LANES = 128
N_KV_COLS = 6 * NSA_KV_HEADS
GATE_PAD = 16
N_BLK_PAD = 128
POOL_HALO = 16

TM_IN = 256
TQ = 256
TK = 256
SEL_UNROLL = 4
TM_MIX = 256
MIX_SUB = 256
TT_ROUTE = 1024
TM_G = 256
TN_G1 = 1024
TN_G2 = 2048
XW = D_MODEL // 2 // LANES
TT_DISP = 256
TT_COMB = 256
ROW_DMA_UNROLL = 4
VMEM_LIMIT = 56 * 1024 * 1024


def _cparams(n_axes, vmem=VMEM_LIMIT):
    return pltpu.CompilerParams(
        dimension_semantics=("arbitrary",) * n_axes, vmem_limit_bytes=vmem)


def _dot(a, b):
    return jnp.dot(a, b, preferred_element_type=F32)


def _slab(ref, token, n_tokens, width):
    first = pl.multiple_of(token * width, width)
    return ref.at[pl.ds(first, n_tokens * width), :]


def _dot_nt(a, b):
    return lax.dot_general(a, b, (((1,), (1,)), ((), ())), preferred_element_type=F32)


def _inproj_kernel(x_ref, g_ref, w_ref, cos_ref, slo_ref, shi_ref, qg_ref, ksg_ref, kwg_ref,
                   qn_ref, kvc_ref, ks_ref, vst_ref, kw_ref, vwt_ref, gt_ref, pool_ref, *, nS):
    x = x_ref[...]
    y = x * lax.rsqrt(jnp.mean(x * x, axis=-1, keepdims=True) + RMS_EPS)
    h = (y * g_ref[...]).astype(BF16)
    cos, slo, shi = cos_ref[...], slo_ref[...], shi_ref[...]
    G = NSA_KV_HEADS
    head = lambda z, c: z[:, c * HEAD_DIM:(c + 1) * HEAD_DIM]

    scale = HEAD_DIM ** -0.5 * LOG2_E
    zq = _dot(h, w_ref[:, 0:NSA_WIDTH])
    for c in range(NSA_HEADS):
        qn_ref[:, c * HEAD_DIM:(c + 1) * HEAD_DIM] = (
            _norm_rope(head(zq, c), qg_ref[...], cos, slo, shi) * scale).astype(BF16)

    pair = lambda p: _dot(h, w_ref[:, NSA_WIDTH + p * G * HEAD_DIM:NSA_WIDTH + (p + 1) * G * HEAD_DIM])
    zkc, zvc = pair(0), pair(1)
    for g in range(G):
        kvc_ref[g] = head(zkc, g)
        kvc_ref[G + g] = head(zvc, g)
    zks, zvs, zkw, zvw = pair(2), pair(3), pair(4), pair(5)
    row = lax.rem(pl.program_id(0), nS) * TM_IN + lax.broadcasted_iota(I32, (TM_IN, N_BLK_PAD), 0)
    lane = lax.broadcasted_iota(I32, (TM_IN, N_BLK_PAD), 1)
    onehot = jnp.where((row >> SLC_SHIFT) == lane, 1.0, 0.0).astype(BF16)
    for g in range(G):
        ks_ref[g, :, 0:HEAD_DIM] = _norm_rope(head(zks, g), ksg_ref[...], cos, slo, shi).astype(BF16)
        ks_ref[g, :, HEAD_DIM:HEAD_DIM + N_BLK_PAD] = onehot
        vst_ref[g, 0] = head(zvs, g).T.astype(BF16)
        kw_ref[g] = _norm_rope(head(zkw, g), kwg_ref[...], cos, slo, shi).astype(BF16)
        vwt_ref[g, 0] = head(zvw, g).T.astype(BF16)

    base = NSA_WIDTH + N_KV_COLS * HEAD_DIM
    pool_ref[...] = _dot(h, w_ref[:, base:base + POOL_WIDTH])
    sig_t = jax.nn.sigmoid(_dot(h, w_ref[:, base + POOL_WIDTH:base + POOL_WIDTH + LANES])).T
    for g in range(G):
        gt_ref[g] = sig_t[g * GATE_PAD:(g + 1) * GATE_PAD]


def _inproj(x2, g1, w_perm, cos, slo, shi, qg, ksg, kwg, B, S):
    T = x2.shape[0]
    n_cols = w_perm.shape[1]
    nS = S // TM_IN
    G = NSA_KV_HEADS
    full = lambda shape: pl.BlockSpec(shape, lambda i: (0,) * len(shape))
    tab = pl.BlockSpec((TM_IN, LANES), lambda i: (lax.rem(i, nS), 0))
    bi = lambda i: (i // nS, lax.rem(i, nS))
    return pl.pallas_call(
        functools.partial(_inproj_kernel, nS=nS),
        grid=(T // TM_IN,),
        in_specs=[
            pl.BlockSpec((TM_IN, D_MODEL), lambda i: (i, 0)),
            full((1, D_MODEL)),
            full((D_MODEL, n_cols)),
            tab, tab, tab,
            full((1, HEAD_DIM)), full((1, HEAD_DIM)), full((1, HEAD_DIM)),
        ],
        out_specs=[
            pl.BlockSpec((TM_IN, NSA_WIDTH), lambda i: (i, 0)),
            pl.BlockSpec((2 * G, TM_IN, LANES), lambda i: (0, i, 0)),
            pl.BlockSpec((G, TM_IN, 2 * HEAD_DIM), lambda i: (*bi(i), 0)),
            pl.BlockSpec((G, 1, HEAD_DIM, TM_IN), lambda i: (*bi(i), 0, 0)),
            pl.BlockSpec((G, TM_IN, HEAD_DIM), lambda i: (*bi(i), 0)),
            pl.BlockSpec((G, 1, HEAD_DIM, TM_IN), lambda i: (*bi(i), 0, 0)),
            pl.BlockSpec((G, GATE_PAD, TM_IN), lambda i: (i // nS, 0, lax.rem(i, nS))),
            pl.BlockSpec((TM_IN, POOL_WIDTH), lambda i: (i, 0)),
        ],
        out_shape=[
            jax.ShapeDtypeStruct((T, NSA_WIDTH), BF16),
            jax.ShapeDtypeStruct((2 * G, T, LANES), F32),
            jax.ShapeDtypeStruct((B * G, S, 2 * HEAD_DIM), BF16),
            jax.ShapeDtypeStruct((B * G, nS, HEAD_DIM, TM_IN), BF16),
            jax.ShapeDtypeStruct((B * G, S, HEAD_DIM), BF16),
            jax.ShapeDtypeStruct((B * G, nS, HEAD_DIM, TM_IN), BF16),
            jax.ShapeDtypeStruct((B * G, GATE_PAD, S), F32),
            jax.ShapeDtypeStruct((T, POOL_WIDTH), F32),
        ],
        compiler_params=_cparams(1),
        name="inproj",
    )(x2, g1, w_perm, cos, slo, shi, qg, ksg, kwg)


def _norm_rope(x, g, cos, sin_lo, sin_hi):
    half = ROT_DIM // 2
    y = x * lax.rsqrt(jnp.mean(x * x, axis=-1, keepdims=True) + QK_EPS) * g
    return (y * cos + pltpu.roll(y, LANES - half, 1) * sin_lo
            + pltpu.roll(y, half, 1) * sin_hi)


def _compress_body(x_ref, pos_ref, w1_ref, b1_ref, w2_ref, b2_ref):
    nc = x_ref.shape[1] // CMP_STRIDE
    pair_w = 2 * HEAD_DIM
    first = jnp.zeros((nc, CMP_HIDDEN), F32)
    second = jnp.zeros((nc, CMP_HIDDEN), F32)
    for p in range(CMP_STRIDE // 2):
        x = jnp.concatenate([x_ref[0, pl.ds(2 * p + u, nc, stride=CMP_STRIDE), :] for u in range(2)], axis=1)
        cols = slice(p * pair_w, (p + 1) * pair_w)
        first = first + _dot((x + pos_ref[0:1, cols]).astype(BF16), w1_ref[0, cols, :])
        second = second + _dot((x + pos_ref[1:2, cols]).astype(BF16), w1_ref[1, cols, :])
    nxt = pltpu.roll(second, nc - 1, 0)
    hid = jax.nn.gelu(first + nxt + b1_ref[...])
    return _dot(hid.astype(BF16), w2_ref[...]) + b2_ref[...]


def _compress_k_kernel(x_ref, pos_ref, w1_ref, b1_ref, w2_ref, b2_ref,
                       g_ref, cos_ref, slo_ref, shi_ref, o_ref):
    out = _compress_body(x_ref, pos_ref, w1_ref, b1_ref, w2_ref, b2_ref)
    o_ref[0] = _norm_rope(out, g_ref[...], cos_ref[...], slo_ref[...], shi_ref[...]).astype(BF16)


def _compress_v_kernel(x_ref, pos_ref, w1_ref, b1_ref, w2_ref, b2_ref, o_ref):
    out = _compress_body(x_ref, pos_ref, w1_ref, b1_ref, w2_ref, b2_ref)
    o_ref[0] = out.T.astype(BF16)


def _compress(kvc, which, pos2, w1, b1, w2, b2, B, S, rope=None):
    NC = S // CMP_STRIDE
    G = NSA_KV_HEADS
    CW = CMP_STRIDE * HEAD_DIM
    common_specs = [
        pl.BlockSpec((1, S, HEAD_DIM), lambda b, g: (which * G + g, b, 0)),
        pl.BlockSpec((2, CW), lambda b, g: (0, 0)),
        pl.BlockSpec((2, CW, CMP_HIDDEN), lambda b, g: (0, 0, 0)),
        pl.BlockSpec((1, CMP_HIDDEN), lambda b, g: (0, 0)),
        pl.BlockSpec((CMP_HIDDEN, HEAD_DIM), lambda b, g: (0, 0)),
        pl.BlockSpec((1, HEAD_DIM), lambda b, g: (0, 0)),
    ]
    if rope is not None:
        gk, cos, slo, shi = rope
        tab = pl.BlockSpec((NC, LANES), lambda b, g: (0, 0))
        return pl.pallas_call(
            _compress_k_kernel,
            grid=(B, G),
            in_specs=common_specs + [pl.BlockSpec((1, HEAD_DIM), lambda b, g: (0, 0)), tab, tab, tab],
            out_specs=pl.BlockSpec((1, NC, HEAD_DIM), lambda b, g: (b * G + g, 0, 0)),
            out_shape=jax.ShapeDtypeStruct((B * G, NC, HEAD_DIM), BF16),
            compiler_params=_cparams(2),
            name="compress_k",
        )(kvc, pos2, w1, b1, w2, b2, gk, cos, slo, shi)
    return pl.pallas_call(
        _compress_v_kernel,
        grid=(B, G),
        in_specs=common_specs,
        out_specs=pl.BlockSpec((1, HEAD_DIM, NC), lambda b, g: (b * G + g, 0, 0)),
        out_shape=jax.ShapeDtypeStruct((B * G, HEAD_DIM, NC), BF16),
        compiler_params=_cparams(2),
        name="compress_v",
    )(kvc, pos2, w1, b1, w2, b2)


BIAS_DIAG, BIAS_FAR, BIAS_ALL, BIAS_NONE = 0, 1, 2, 3


def _tile_bias_table():
    k = lax.broadcasted_iota(I32, (TK, TQ), 0)
    t = lax.broadcasted_iota(I32, (TK, TQ), 1)
    neg = jnp.full((TK, TQ), MASK_VALUE, F32)
    zero = jnp.zeros((TK, TQ), F32)
    return jnp.stack([jnp.where(k <= t, zero, neg), jnp.where(k > t, zero, neg), neg, zero])


def _add_tile_bias(s, b):
    return jnp.concatenate([s[:, r * TQ:(r + 1) * TQ] + b for r in range(HEADS_PER_GROUP)], axis=1)


def _nsa_kernel(q_ref, kc_ref, vct_ref, ks_ref, vst_ref, kw_ref, vwt_ref, gt_ref, bias_ref, o_ref,
                qaug_ref, s_ref, cm_ref, m_ref, l_ref, acc_ref, oc_ref, ow_ref, *, S):
    R = HEADS_PER_GROUP
    NQ = R * TQ
    NC = S // CMP_STRIDE
    qi = pl.program_id(2)
    qs = qi * TQ

    q = q_ref[...].astype(F32)
    q_t = jnp.concatenate([q[:, r * HEAD_DIM:(r + 1) * HEAD_DIM].T for r in range(R)], axis=1).astype(BF16)
    col = lax.broadcasted_iota(I32, (1, NQ), 1)
    t_row = qs + (col & (TQ - 1))

    qaug_ref[0:HEAD_DIM, :] = q_t

    def compressed_and_select(nc, nb):
        s_c = _dot(kc_ref[0, 0:nc, :], q_t)
        c_end = lax.broadcasted_iota(I32, (nc, 1), 0) * CMP_STRIDE + (CMP_BLK - 1)
        s_c = jnp.where(c_end <= t_row, s_c, MASK_VALUE)
        e_c = jnp.exp2(s_c - jnp.max(s_c, axis=0, keepdims=True))
        l_c = jnp.sum(e_c, axis=0, keepdims=True)
        p_c = e_c * jnp.where(t_row >= CMP_BLK - 1, 1.0 / l_c, 0.0)
        oc_ref[...] = _dot(vct_ref[0, :, 0:nc], p_c.astype(BF16))

        imp = p_c[:, 0:TQ]
        for r in range(1, R):
            imp = imp + p_c[:, r * TQ:(r + 1) * TQ]
        per = SLC_BLK // CMP_STRIDE
        jj = lax.broadcasted_iota(I32, (nb, nc), 0) * per
        nn = lax.broadcasted_iota(I32, (nb, nc), 1)
        fold = (jnp.where((nn >= jj) & (nn < jj + per), 1.0, 0.0)
                + jnp.where((nn >= jj - 1) & (nn < jj + per - 1), 1.0, 0.0)).astype(BF16)
        imp_hi = imp.astype(BF16)
        imp_lo = (imp - imp_hi.astype(F32)).astype(BF16)
        blk_score = _dot(fold, imp_hi) + _dot(fold, imp_lo)

        jb = lax.broadcasted_iota(I32, (nb, TQ), 0)
        cur = (qs + lax.broadcasted_iota(I32, (nb, TQ), 1)) >> SLC_SHIFT
        forced = (jb == 0) | (jb == cur) | (jb == cur - 1)
        jbf = jb.astype(F32)
        bias = jnp.where(forced, 0.0, MASK_VALUE)
        val = jnp.where(forced, -jnp.inf, jnp.where(jb <= cur, blk_score, MASK_VALUE))
        for _ in range(min(N_SEL, S // SLC_BLK) - 3):
            mx = jnp.max(val, axis=0, keepdims=True)
            first = jnp.min(jnp.where(val == mx, jbf, float(N_BLK_PAD)), axis=0, keepdims=True)
            pick = jbf == first
            bias = jnp.where(pick, 0.0, bias)
            val = jnp.where(pick, -jnp.inf, val)
        qaug_ref[HEAD_DIM:HEAD_DIM + nb, :] = jnp.concatenate([bias.astype(BF16)] * R, axis=1)
        if nb < N_BLK_PAD:
            qaug_ref[HEAD_DIM + nb:HEAD_DIM + N_BLK_PAD, :] = jnp.full(
                (N_BLK_PAD - nb, NQ), MASK_VALUE, BF16)

    def window_branch():
        w_tiles = ((jnp.maximum(qi - 2, 0), jnp.where(qi >= 2, BIAS_FAR, BIAS_ALL)),
                   (jnp.maximum(qi - 1, 0), jnp.where(qi >= 1, BIAS_NONE, BIAS_ALL)),
                   (qi, BIAS_DIAG))
        s_w = []
        for kt, bi in w_tiles:
            kwt = kw_ref[0, pl.ds(pl.multiple_of(kt * TK, TK), TK), :]
            s_w.append(_add_tile_bias(_dot(kwt, q_t), bias_ref[bi]))
        m_w = jnp.max(s_w[0], axis=0, keepdims=True)
        for s in s_w[1:]:
            m_w = jnp.maximum(m_w, jnp.max(s, axis=0, keepdims=True))
        l_w = jnp.zeros((1, NQ), F32)
        o_w = jnp.zeros((HEAD_DIM, NQ), F32)
        for (kt, _), s in zip(w_tiles, s_w):
            e = jnp.exp2(s - m_w)
            l_w = l_w + jnp.sum(e, axis=0, keepdims=True)
            o_w = o_w + _dot(vwt_ref[0, kt], e.astype(BF16))
        ow_ref[...] = o_w * (1.0 / l_w)

    first_half = (qi + 1) * TQ * 2 <= S

    @pl.when(first_half)
    def _():
        compressed_and_select(NC // 2, N_BLK_PAD // 2)
        window_branch()

    @pl.when(jnp.logical_not(first_half))
    def _():
        compressed_and_select(NC, N_BLK_PAD)
        window_branch()


    m_ref[...] = jnp.full((1, NQ), MASK_VALUE, F32)
    l_ref[...] = jnp.zeros((1, NQ), F32)
    acc_ref[...] = jnp.zeros((HEAD_DIM, NQ), F32)

    def produce(kj, slot, causal):
        k = ks_ref[0, pl.ds(pl.multiple_of(kj * TK, TK), TK), :]
        s = _dot(k, qaug_ref[...])
        if causal:
            s = _add_tile_bias(s, bias_ref[BIAS_DIAG])
        s_ref[slot] = s
        cm_ref[slot] = jnp.max(s, axis=0, keepdims=True)

    def consume(kj, slot):
        m_old = m_ref[...]
        m_new = jnp.maximum(m_old, cm_ref[slot])
        alpha = jnp.exp2(m_old - m_new)
        p = jnp.exp2(s_ref[slot] - m_new)
        l_ref[...] = alpha * l_ref[...] + jnp.sum(p, axis=0, keepdims=True)
        acc_ref[...] = alpha * acc_ref[...] + _dot(vst_ref[0, kj], p.astype(BF16))
        m_ref[...] = m_new

    def stage(kj, slot, causal_next):
        produce(kj + 1, 1 - slot, causal_next)
        consume(kj, slot)

    @pl.when(qi == 0)
    def _():
        produce(0, 0, True)

    @pl.when(qi > 0)
    def _():
        produce(0, 0, False)

    def stage_group(pp, carry):
        for u in range(SEL_UNROLL):
            stage(SEL_UNROLL * pp + u, u & 1, False)
        return carry

    n_plain = jnp.maximum(qi - 1, 0)
    n_grouped = n_plain // SEL_UNROLL * SEL_UNROLL
    lax.fori_loop(0, n_plain // SEL_UNROLL, stage_group, 0)
    n_left = n_plain - n_grouped
    for u in range(0, SEL_UNROLL - 2, 2):
        @pl.when(n_left >= u + 2)
        def _(u=u):
            stage(n_grouped + u, 0, False)
            stage(n_grouped + u + 1, 1, False)

    @pl.when((n_left & 1) == 1)
    def _():
        stage(n_plain - 1, 0, False)

    @pl.when((qi >= 1) & ((n_plain & 1) == 0))
    def _():
        stage(qi - 1, 0, True)
        consume(qi, 1)

    @pl.when((qi >= 1) & ((n_plain & 1) == 1))
    def _():
        stage(qi - 1, 1, True)
        consume(qi, 0)

    @pl.when(qi == 0)
    def _():
        consume(0, 0)

    inv_l = 1.0 / l_ref[...]

    for r in range(R):
        sl = slice(r * TQ, (r + 1) * TQ)
        g0 = gt_ref[0, r * N_BRANCH + 0:r * N_BRANCH + 1, :]
        g1 = gt_ref[0, r * N_BRANCH + 1:r * N_BRANCH + 2, :]
        g2 = gt_ref[0, r * N_BRANCH + 2:r * N_BRANCH + 3, :]
        o = (g0 * oc_ref[:, sl] + (g1 * inv_l[:, sl]) * acc_ref[:, sl]
             + g2 * ow_ref[:, sl])
        o_ref[:, r * HEAD_DIM:(r + 1) * HEAD_DIM] = o.T.astype(BF16)


def _nsa(qn, kc, vct, ks, vst, kw, vwt, gt, B, S):
    G = NSA_KV_HEADS
    R = HEADS_PER_GROUP
    nQ = S // TQ
    nK = S // TK
    NC = S // CMP_STRIDE
    bg = lambda b, g, i: b * G + g
    return pl.pallas_call(
        functools.partial(_nsa_kernel, S=S),
        grid=(B, G, nQ),
        in_specs=[
            pl.BlockSpec((TQ, R * HEAD_DIM), lambda b, g, i: (b * nQ + i, g)),
            pl.BlockSpec((1, NC, HEAD_DIM), lambda b, g, i: (bg(b, g, i), 0, 0)),
            pl.BlockSpec((1, HEAD_DIM, NC), lambda b, g, i: (bg(b, g, i), 0, 0)),
            pl.BlockSpec((1, S, 2 * HEAD_DIM), lambda b, g, i: (bg(b, g, i), 0, 0)),
            pl.BlockSpec((1, nK, HEAD_DIM, TK), lambda b, g, i: (bg(b, g, i), 0, 0, 0)),
            pl.BlockSpec((1, S, HEAD_DIM), lambda b, g, i: (bg(b, g, i), 0, 0)),
            pl.BlockSpec((1, nK, HEAD_DIM, TK), lambda b, g, i: (bg(b, g, i), 0, 0, 0)),
            pl.BlockSpec((1, GATE_PAD, TQ), lambda b, g, i: (bg(b, g, i), 0, i)),
            pl.BlockSpec((4, TK, TQ), lambda b, g, i: (0, 0, 0)),
        ],
        out_specs=pl.BlockSpec((TQ, R * HEAD_DIM), lambda b, g, i: (b * nQ + i, g)),
        out_shape=jax.ShapeDtypeStruct((B * S, NSA_WIDTH), BF16),
        scratch_shapes=[
            pltpu.VMEM((2 * HEAD_DIM, R * TQ), BF16),
            pltpu.VMEM((2, TK, R * TQ), F32),
            pltpu.VMEM((2, 1, R * TQ), F32),
            pltpu.VMEM((1, R * TQ), F32),
            pltpu.VMEM((1, R * TQ), F32),
            pltpu.VMEM((HEAD_DIM, R * TQ), F32),
            pltpu.VMEM((HEAD_DIM, R * TQ), F32),
            pltpu.VMEM((HEAD_DIM, R * TQ), F32),
        ],
        compiler_params=_cparams(3),
        name="nsa",
    )(qn, kc, vct, ks, vst, kw, vwt, gt, _tile_bias_table())


def _mix_kernel(nsa_ref, pool_ref, halo_ref, x_ref, wo_ref, pw_ref, ps_ref, g2_ref,
                wr2_ref, br_ref, x1_ref, h2p_ref, logit_ref, ext_ref, *, S):
    nS = S // TM_MIX
    i = pl.program_id(0)
    t0 = lax.rem(i, nS) * TM_MIX
    ext_ref[0:POOL_HALO, :] = jnp.where(t0 == 0, 0.0, halo_ref[...])
    ext_ref[POOL_HALO:POOL_HALO + TM_MIX, :] = pool_ref[...]
    for sb in range(TM_MIX // MIX_SUB):
        r0 = sb * MIX_SUB
        rows = slice(r0, r0 + MIX_SUB)
        t = t0 + r0 + lax.broadcasted_iota(I32, (MIX_SUB, POOL_GC), 0)
        acc = _dot(nsa_ref[rows, :], wo_ref[0:NSA_WIDTH, :])
        for gi, w in enumerate(POOL_WINDOWS):
            cs = slice(gi * POOL_GC, (gi + 1) * POOL_GC)
            v = pool_ref[rows, cs]
            tot = v
            for k in range(1, w):
                tot = tot + ext_ref[POOL_HALO + r0 - k:POOL_HALO + r0 - k + MIX_SUB, cs]
            cnt = jnp.minimum(t + 1, w).astype(F32)
            d = tot / cnt - v
            y = _dot(d.astype(BF16), pw_ref[gi]) * ps_ref[:, cs]
            acc = acc + _dot(y.astype(BF16),
                             wo_ref[NSA_WIDTH + gi * POOL_GC:NSA_WIDTH + (gi + 1) * POOL_GC, :])
        x1 = x_ref[rows, :] + acc
        x1_ref[rows, :] = x1
        h2 = x1 * lax.rsqrt(jnp.mean(x1 * x1, axis=-1, keepdims=True) + RMS_EPS) * g2_ref[...]
        half = D_MODEL // 2
        packed = pltpu.pack_elementwise([h2[:, 0:half], h2[:, half:D_MODEL]], packed_dtype=BF16)
        for c in range(XW):
            h2p_ref[pl.ds(r0 * XW + c, MIX_SUB, stride=XW), :] = packed[:, c * LANES:(c + 1) * LANES]
        hi = h2.astype(BF16)
        lo = (h2 - hi.astype(F32)).astype(BF16)
        hl = _dot(hi, wr2_ref[...])
        logit_ref[rows, :] = (hl[:, 0:LANES] + hl[:, LANES:2 * LANES] + _dot(lo, wr2_ref[:, 0:LANES])
                              + br_ref[...])


def _mix(nsa_out, pool_in, x2, wo, pw, ps, g2, wr2, br, S):
    T = x2.shape[0]
    hb = TM_MIX // POOL_HALO
    full = lambda shape: pl.BlockSpec(shape, lambda i: (0,) * len(shape))
    return pl.pallas_call(
        functools.partial(_mix_kernel, S=S),
        grid=(T // TM_MIX,),
        in_specs=[
            pl.BlockSpec((TM_MIX, NSA_WIDTH), lambda i: (i, 0)),
            pl.BlockSpec((TM_MIX, POOL_WIDTH), lambda i: (i, 0)),
            pl.BlockSpec((POOL_HALO, POOL_WIDTH), lambda i: (jnp.maximum(i * hb - 1, 0), 0)),
            pl.BlockSpec((TM_MIX, D_MODEL), lambda i: (i, 0)),
            full((D_MODEL, D_MODEL)),
            full((len(POOL_WINDOWS), POOL_GC, POOL_GC)),
            full((1, POOL_WIDTH)),
            full((1, D_MODEL)),
            full((D_MODEL, 2 * LANES)),
            full((1, LANES)),
        ],
        out_specs=[
            pl.BlockSpec((TM_MIX, D_MODEL), lambda i: (i, 0)),
            pl.BlockSpec((TM_MIX * XW, LANES), lambda i: (i, 0)),
            pl.BlockSpec((TM_MIX, LANES), lambda i: (i, 0)),
        ],
        out_shape=[
            jax.ShapeDtypeStruct((T, D_MODEL), F32),
            jax.ShapeDtypeStruct((T * XW, LANES), U32),
            jax.ShapeDtypeStruct((T, LANES), F32),
        ],
        scratch_shapes=[pltpu.VMEM((POOL_HALO + TM_MIX, POOL_WIDTH), F32)],
        compiler_params=_cparams(1),
        name="mix",
    )(nsa_out, pool_in, pool_in, x2, wo, pw, ps, g2, wr2, br)


def _route_kernel(logit_ref, dest_ref, wt_ref, meta_ref, cnt_ref, run_ref, tri_ref, *, n_tiles):
    phase = pl.program_id(0)
    i = pl.program_id(1)
    TT = TT_ROUTE
    E = N_EXPERTS

    @pl.when((phase == 0) & (i == 0))
    def _():
        cnt_ref[...] = jnp.zeros_like(cnt_ref)
        run_ref[...] = jnp.zeros_like(run_ref)

    lt = logit_ref[...].T[0:E, :]
    ef = lax.broadcasted_iota(I32, (E, TT), 0).astype(F32)
    work = lt
    ids, vals, hots = [], [], []
    for _ in range(TOP_K):
        mx = jnp.max(work, axis=0, keepdims=True)
        eid = jnp.min(jnp.where(work == mx, ef, float(E)), axis=0, keepdims=True)
        hot = ef == eid
        ids.append(eid)
        vals.append(mx)
        hots.append(hot)
        work = jnp.where(hot, -jnp.inf, work)
    member = jnp.where(hots[0] | hots[1] | hots[2] | hots[3], 1.0, 0.0)
    tile_cnt = jnp.sum(member, axis=1, keepdims=True)

    @pl.when(phase == 0)
    def _():
        cnt_ref[...] = cnt_ref[...] + tile_cnt

    @pl.when(phase == 1)
    def _():
        ex = [jnp.exp(v - vals[0]) for v in vals]
        den = ex[0] + ex[1] + ex[2] + ex[3]
        cnt = jnp.broadcast_to(cnt_ref[...], (E, LANES))
        padded = jnp.ceil(cnt * (1.0 / TM_G)) * TM_G
        erow = lax.broadcasted_iota(I32, (E, LANES), 0)
        start = jnp.zeros((E, LANES), F32)
        running = jnp.zeros((1, LANES), F32)
        for e in range(E):
            start = jnp.where(erow == e, running, start)
            running = running + padded[e:e + 1, :]
        end = start + padded
        @pl.when(i == 0)
        def _():
            tri_ref[...] = jnp.where(lax.broadcasted_iota(I32, (TT, TT), 0)
                                     < lax.broadcasted_iota(I32, (TT, TT), 1), 1.0, 0.0).astype(BF16)

        before = _dot(member.astype(BF16), tri_ref[...]) + run_ref[...]
        slot = before + start[:, 0:1]
        for k in range(TOP_K):
            d = jnp.sum(jnp.where(hots[k], slot, 0.0), axis=0, keepdims=True)
            dest_ref[k:k + 1, :] = d.astype(I32)
            wt_ref[k:k + 1, :] = ex[k] / den
        run_ref[...] = run_ref[...] + tile_cnt

        @pl.when(i == 0)
        def _():
            lanes = meta_ref.shape[1]
            endw = jnp.broadcast_to(end[:, 0:1], (E, lanes))
            tile_row = (lax.broadcasted_iota(I32, (E, lanes), 1) * TM_G).astype(F32)
            owner = jnp.minimum(jnp.sum(jnp.where(endw <= tile_row, 1.0, 0.0), axis=0, keepdims=True),
                                float(E - 1))
            nxt = jnp.full((E, LANES), -1.0, F32)
            later = jnp.full((1, LANES), -1.0, F32)
            for e in reversed(range(E)):
                nxt = jnp.where(erow == e, later, nxt)
                later = jnp.where(cnt[e:e + 1, :] > 0.0, float(e), later)
            erow_w = lax.broadcasted_iota(I32, (E, lanes), 0).astype(F32)
            nxt_tile = jnp.sum(jnp.where(erow_w == owner, jnp.broadcast_to(nxt[:, 0:1], (E, lanes)), 0.0),
                               axis=0, keepdims=True)
            meta_ref[0:1, :] = owner.astype(I32)
            meta_ref[1:2, :] = jnp.where(tile_row[0:1, :] < running[:, 0:1], 1, 0).astype(I32)
            meta_ref[2:3, :] = nxt_tile.astype(I32)
            lane_w = lax.broadcasted_iota(I32, (E, lanes), 1).astype(F32)
            ends = jnp.sum(jnp.where(erow_w == lane_w, endw, 0.0), axis=0, keepdims=True)
            ends = jnp.where(lane_w[0:1, :] == float(E), running[:, 0:1] * (1.0 / TM_G), ends)
            meta_ref[3:4, :] = ends.astype(I32)
            meta_ref[4:8, :] = jnp.zeros((4, lanes), I32)


def _route(logits, n_tiles):
    T = logits.shape[0]
    nT = T // TT_ROUTE
    lanes = -(-n_tiles // LANES) * LANES
    return pl.pallas_call(
        functools.partial(_route_kernel, n_tiles=n_tiles),
        grid=(2, nT),
        in_specs=[pl.BlockSpec((TT_ROUTE, LANES), lambda p, i: (i, 0))],
        out_specs=[
            pl.BlockSpec((TOP_K, TT_ROUTE), lambda p, i: (0, i * p)),
            pl.BlockSpec((TOP_K, TT_ROUTE), lambda p, i: (0, i * p)),
            pl.BlockSpec((8, lanes), lambda p, i: (0, 0)),
        ],
        out_shape=[
            jax.ShapeDtypeStruct((TOP_K, T), I32),
            jax.ShapeDtypeStruct((TOP_K, T), F32),
            jax.ShapeDtypeStruct((8, lanes), I32),
        ],
        scratch_shapes=[pltpu.VMEM((N_EXPERTS, 1), F32), pltpu.VMEM((N_EXPERTS, 1), F32),
                        pltpu.VMEM((TT_ROUTE, TT_ROUTE), BF16)],
        compiler_params=_cparams(2),
        name="route",
    )(logits)


def _dispatch_kernel(ends_ref, dest_ref, h_ref, xs_ref, zero_ref, sem, zsem):
    @pl.when(pl.program_id(0) == 0)
    def _():
        zero_ref[...] = jnp.zeros_like(zero_ref)

        def tail_copy(e):
            return pltpu.make_async_copy(zero_ref, _slab(xs_ref, ends_ref[e] - TM_G, TM_G, XW), zsem)

        def nonempty(e):
            return ends_ref[e] > (ends_ref[e - 1] if e else 0)

        for e in range(N_EXPERTS):
            @pl.when(nonempty(e))
            def _(e=e):
                tail_copy(e).start()
        for e in range(N_EXPERTS):
            @pl.when(nonempty(e))
            def _(e=e):
                tail_copy(e).wait()

        def unused_copy(tile):
            return pltpu.make_async_copy(zero_ref, _slab(xs_ref, tile * TM_G, TM_G, XW), zsem)

        n_used = ends_ref[N_EXPERTS]
        n_all = xs_ref.shape[0] // (TM_G * XW)

        def start_unused(tile, carry):
            unused_copy(tile).start()
            return carry

        def wait_unused(tile, carry):
            unused_copy(tile).wait()
            return carry

        lax.fori_loop(n_used, n_all, start_unused, 0)
        lax.fori_loop(n_used, n_all, wait_unused, 0)

    def issue(t, carry):
        for k in range(TOP_K):
            pltpu.make_async_copy(_slab(h_ref, t, 1, XW), _slab(xs_ref, dest_ref[k, t], 1, XW),
                                  sem).start(priority=k % 2)
        return carry

    lax.fori_loop(0, TT_DISP, issue, 0, unroll=ROW_DMA_UNROLL)
    all_rows = _slab(xs_ref, 0, TOP_K * TT_DISP, XW)
    pltpu.make_async_copy(all_rows, all_rows, sem).wait()


def _dispatch(ends, dest, h2p, n_rows):
    T = h2p.shape[0] // XW
    grid_spec = pltpu.PrefetchScalarGridSpec(
        num_scalar_prefetch=1,
        grid=(T // TT_DISP,),
        in_specs=[
            pl.BlockSpec((TOP_K, TT_DISP), lambda i, ends: (0, i), memory_space=pltpu.SMEM),
            pl.BlockSpec((TT_DISP * XW, LANES), lambda i, ends: (i, 0)),
        ],
        out_specs=pl.BlockSpec(memory_space=pl.ANY),
        scratch_shapes=[pltpu.VMEM((TM_G * XW, LANES), U32), pltpu.SemaphoreType.DMA,
                        pltpu.SemaphoreType.DMA],
    )
    return pl.pallas_call(
        _dispatch_kernel,
        grid_spec=grid_spec,
        out_shape=jax.ShapeDtypeStruct((n_rows * XW, LANES), U32),
        compiler_params=_cparams(1),
        name="dispatch",
    )(ends, dest, h2p)


def _expert_changed(te_ref, i):
    return (i == 0) | (te_ref[i] != te_ref[jnp.maximum(i - 1, 0)])


WEIGHT_DMA_PRIORITY = 1


def _stream_expert_weights(te_ref, tv_ref, nx_ref, copies, wst_ref, wbf_ref, slot_ref):
    j = pl.program_id(0)
    i = pl.program_id(1)

    @pl.when((j == 0) & (i == 0))
    def _():
        slot_ref[0] = 0
        for c in copies(te_ref[0], 0, 0):
            c.start(priority=WEIGHT_DMA_PRIORITY)

    @pl.when((tv_ref[i] > 0) & _expert_changed(te_ref, i))
    def _():
        slot = slot_ref[0]
        e_next = nx_ref[i]

        @pl.when(e_next >= 0)
        def _():
            for c in copies(e_next, j, 1 - slot):
                c.start(priority=WEIGHT_DMA_PRIORITY)

        @pl.when((e_next < 0) & (j + 1 < pl.num_programs(0)))
        def _():
            for c in copies(te_ref[0], j + 1, 1 - slot):
                c.start(priority=WEIGHT_DMA_PRIORITY)

        for c in copies(te_ref[i], j, slot):
            c.wait()
        slot_ref[1] = slot
        slot_ref[0] = 1 - slot


W_CHUNK = 256


def _expert_matmul(x, wst_ref, wbf_ref, slot_ref, refresh):
    if not refresh:
        return _dot(x, wbf_ref[...])
    acc = None
    for k in range(x.shape[1] // W_CHUNK):
        rows = slice(k * W_CHUNK, (k + 1) * W_CHUNK)
        wbf_ref[rows, :] = wst_ref[slot_ref[1], rows, :].astype(BF16)
        part = _dot(x[:, rows], wbf_ref[rows, :])
        acc = part if acc is None else acc + part
    return acc


def _gemm1_kernel(te_ref, tv_ref, nx_ref, x_ref, bg_ref, bu_ref, w_hbm, act_ref,
                  wst_ref, wbf_ref, slot_ref, sem):
    i = pl.program_id(1)
    valid = tv_ref[i] > 0
    changed = _expert_changed(te_ref, i)

    def copies(e, j, slot):
        col = pl.multiple_of(j * TN_G1, TN_G1)
        return (pltpu.make_async_copy(w_hbm.at[e, :, pl.ds(col, TN_G1)],
                                      wst_ref.at[slot, :, pl.ds(0, TN_G1)], sem.at[slot]),
                pltpu.make_async_copy(w_hbm.at[e, :, pl.ds(D_FF + col, TN_G1)],
                                      wst_ref.at[slot, :, pl.ds(TN_G1, TN_G1)], sem.at[slot]))

    _stream_expert_weights(te_ref, tv_ref, nx_ref, copies, wst_ref, wbf_ref, slot_ref)

    def compute(refresh):
        xp = jnp.concatenate([x_ref[pl.ds(c, TM_G, stride=XW), :] for c in range(XW)], axis=1)
        lo = pltpu.unpack_elementwise(xp, index=0, packed_dtype=BF16, unpacked_dtype=F32).astype(BF16)
        hi = pltpu.unpack_elementwise(xp, index=1, packed_dtype=BF16, unpacked_dtype=F32).astype(BF16)
        gu = _expert_matmul(jnp.concatenate([lo, hi], axis=1), wst_ref, wbf_ref, slot_ref, refresh)
        gate = jnp.minimum(gu[:, 0:TN_G1] + bg_ref[0], SWIGLU_LIMIT)
        up = jnp.clip(gu[:, TN_G1:2 * TN_G1] + bu_ref[0], -SWIGLU_LIMIT, SWIGLU_LIMIT)
        act = (up + 1.0) * gate * jax.nn.sigmoid(SWIGLU_ALPHA * gate)
        act_ref[...] = act.astype(BF16)

    @pl.when(valid & changed)
    def _():
        wbf_ref[...] = wst_ref[slot_ref[1]].astype(BF16)

    @pl.when(valid)
    def _():
        compute(False)

    @pl.when(jnp.logical_not(valid))
    def _():
        act_ref[...] = jnp.zeros_like(act_ref)


def _gemm1(te, tv, nx, xs, w_gu, b_gu3, n_tiles):
    n_rows = xs.shape[0] // XW
    nJ = D_FF // TN_G1
    grid_spec = pltpu.PrefetchScalarGridSpec(
        num_scalar_prefetch=3,
        grid=(nJ, n_tiles),
        in_specs=[
            pl.BlockSpec((TM_G * XW, LANES), lambda j, i, te, tv, nx: (i, 0)),
            pl.BlockSpec((1, 1, TN_G1), lambda j, i, te, tv, nx: (te[i], 0, j)),
            pl.BlockSpec((1, 1, TN_G1), lambda j, i, te, tv, nx: (te[i], 0, nJ + j)),
            pl.BlockSpec(memory_space=pl.ANY),
        ],
        out_specs=pl.BlockSpec((TM_G, TN_G1), lambda j, i, te, tv, nx: (i, j)),
        scratch_shapes=[
            pltpu.VMEM((2, D_MODEL, 2 * TN_G1), F32),
            pltpu.VMEM((D_MODEL, 2 * TN_G1), BF16),
            pltpu.SMEM((2,), I32),
            pltpu.SemaphoreType.DMA((2,)),
        ],
    )
    return pl.pallas_call(
        _gemm1_kernel,
        grid_spec=grid_spec,
        out_shape=jax.ShapeDtypeStruct((n_rows, D_FF), BF16),
        compiler_params=_cparams(2),
        name="gemm1",
    )(te, tv, nx, xs, b_gu3, b_gu3, w_gu)


def _gemm2_kernel(te_ref, tv_ref, nx_ref, a_ref, b_ref, w_hbm, y_ref, wst_ref, wbf_ref, slot_ref, sem):
    i = pl.program_id(1)
    valid = tv_ref[i] > 0

    def copies(e, j, slot):
        col = pl.multiple_of(j * TN_G2, TN_G2)
        return (pltpu.make_async_copy(w_hbm.at[e, :, pl.ds(col, TN_G2)], wst_ref.at[slot], sem.at[slot]),)

    _stream_expert_weights(te_ref, tv_ref, nx_ref, copies, wst_ref, wbf_ref, slot_ref)

    def compute(refresh):
        y = _expert_matmul(a_ref[...], wst_ref, wbf_ref, slot_ref, refresh) + b_ref[0]
        half = D_MODEL // 2
        packed = pltpu.pack_elementwise([y[:, 0:half], y[:, half:D_MODEL]], packed_dtype=BF16)
        for c in range(XW):
            y_ref[pl.ds(c, TM_G, stride=XW), :] = packed[:, c * LANES:(c + 1) * LANES]

    changed = _expert_changed(te_ref, i)

    @pl.when(valid & changed)
    def _():
        compute(True)

    @pl.when(valid & jnp.logical_not(changed))
    def _():
        compute(False)

    @pl.when(jnp.logical_not(valid))
    def _():
        zero = jnp.zeros(y_ref.shape, F32)
        y_ref[...] = pltpu.pack_elementwise([zero, zero], packed_dtype=BF16)


def _gemm2(te, tv, nx, act, w_d, b_d3, n_tiles):
    n_rows = act.shape[0]
    nJ = D_MODEL // TN_G2
    grid_spec = pltpu.PrefetchScalarGridSpec(
        num_scalar_prefetch=3,
        grid=(nJ, n_tiles),
        in_specs=[
            pl.BlockSpec((TM_G, D_FF), lambda j, i, te, tv, nx: (i, 0)),
            pl.BlockSpec((1, 1, TN_G2), lambda j, i, te, tv, nx: (te[i], 0, j)),
            pl.BlockSpec(memory_space=pl.ANY),
        ],
        out_specs=pl.BlockSpec((TM_G * XW, LANES), lambda j, i, te, tv, nx: (i, 0)),
        scratch_shapes=[
            pltpu.VMEM((2, D_FF, TN_G2), F32),
            pltpu.VMEM((D_FF, TN_G2), BF16),
            pltpu.SMEM((2,), I32),
            pltpu.SemaphoreType.DMA((2,)),
        ],
    )
    return pl.pallas_call(
        _gemm2_kernel,
        grid_spec=grid_spec,
        out_shape=jax.ShapeDtypeStruct((n_rows * XW, LANES), U32),
        compiler_params=_cparams(2),
        name="gemm2",
    )(te, tv, nx, act, b_d3, w_d)


def _combine_kernel(dest_ref, dnext_ref, wt_ref, x1_ref, y_ref, o_ref, buf_ref, sem):
    i = pl.program_id(0)
    n = pl.num_programs(0)

    def gather(d_ref, slot):
        def issue(t, carry):
            for k in range(TOP_K):
                pltpu.make_async_copy(_slab(y_ref, d_ref[k, t], 1, XW), _slab(buf_ref.at[slot, k], t, 1, XW),
                                      sem.at[slot]).start(priority=k % 2)
            return carry
        lax.fori_loop(0, TT_COMB, issue, 0, unroll=ROW_DMA_UNROLL)

    def finish(slot):
        pltpu.make_async_copy(buf_ref.at[slot], buf_ref.at[slot], sem.at[slot]).wait()
        wts = [jnp.broadcast_to(wt_ref[:, k:k + 1], (TT_COMB, LANES)) for k in range(TOP_K)]
        for c in range(XW):
            lo0, hi0 = c * LANES, D_MODEL // 2 + c * LANES
            acc_lo = x1_ref[:, lo0:lo0 + LANES]
            acc_hi = x1_ref[:, hi0:hi0 + LANES]
            for k in range(TOP_K):
                words = buf_ref[slot, k, pl.ds(c, TT_COMB, stride=XW), :]
                acc_lo = acc_lo + wts[k] * pltpu.unpack_elementwise(
                    words, index=0, packed_dtype=BF16, unpacked_dtype=F32)
                acc_hi = acc_hi + wts[k] * pltpu.unpack_elementwise(
                    words, index=1, packed_dtype=BF16, unpacked_dtype=F32)
            o_ref[:, lo0:lo0 + LANES] = acc_lo
            o_ref[:, hi0:hi0 + LANES] = acc_hi

    @pl.when(i == 0)
    def _():
        gather(dest_ref, 0)

    for slot in range(2):
        @pl.when((i & 1) == slot)
        def _(slot=slot):
            @pl.when(i + 1 < n)
            def _():
                gather(dnext_ref, 1 - slot)
            finish(slot)


def _combine(dest, wt_tok, x1, y):
    T = x1.shape[0]
    nT = T // TT_COMB
    return pl.pallas_call(
        _combine_kernel,
        grid=(nT,),
        in_specs=[
            pl.BlockSpec((TOP_K, TT_COMB), lambda i: (0, i), memory_space=pltpu.SMEM),
            pl.BlockSpec((TOP_K, TT_COMB), lambda i: (0, jnp.minimum(i + 1, nT - 1)),
                         memory_space=pltpu.SMEM),
            pl.BlockSpec((TT_COMB, TOP_K), lambda i: (i, 0)),
            pl.BlockSpec((TT_COMB, D_MODEL), lambda i: (i, 0)),
            pl.BlockSpec(memory_space=pl.ANY),
        ],
        out_specs=pl.BlockSpec((TT_COMB, D_MODEL), lambda i: (i, 0)),
        out_shape=jax.ShapeDtypeStruct((T, D_MODEL), F32),
        scratch_shapes=[pltpu.VMEM((2, TOP_K, TT_COMB * XW, LANES), U32), pltpu.SemaphoreType.DMA((2,))],
        compiler_params=_cparams(1),
        name="combine",
    )(dest, dest, wt_tok, x1, y)


def _rope_tables(pos):
    half = ROT_DIM // 2
    inv_freq = ROPE_THETA ** (-jnp.arange(0, ROT_DIM, 2, dtype=F32) / ROT_DIM)
    ang = pos[:, None] * inv_freq[None, :]
    cos, sin = jnp.cos(ang), jnp.sin(ang)
    n = pos.shape[0]
    ones = jnp.ones((n, LANES - ROT_DIM), F32)
    zeros = jnp.zeros((n, LANES - ROT_DIM), F32)
    zh = jnp.zeros((n, half), F32)
    return (jnp.concatenate([cos, cos, ones], axis=1),
            jnp.concatenate([-sin, zh, zeros], axis=1),
            jnp.concatenate([zh, sin, zeros], axis=1))


def _permute_w_in(w_in):
    kv_end = NSA_WIDTH + N_KV_COLS * HEAD_DIM
    n_gate = NSA_HEADS * N_BRANCH
    per_g = HEADS_PER_GROUP * N_BRANCH
    gate = w_in[:, kv_end:kv_end + n_gate]
    pieces = [w_in[:, :kv_end], w_in[:, kv_end + n_gate:]]
    zpad = jnp.zeros((D_MODEL, GATE_PAD - per_g), w_in.dtype)
    for g in range(NSA_KV_HEADS):
        pieces += [gate[:, g * per_g:(g + 1) * per_g], zpad]
    pieces.append(jnp.zeros((D_MODEL, LANES - NSA_KV_HEADS * GATE_PAD), w_in.dtype))
    return jnp.concatenate(pieces, axis=1).astype(BF16)


def _layer(x, norm1_g, w_in, q_norm_g, k_norm_cmp_g, k_norm_slc_g, k_norm_win_g,
           cmp_k_pos, cmp_k_w1, cmp_k_b1, cmp_k_w2, cmp_k_b2,
           cmp_v_pos, cmp_v_w1, cmp_v_b1, cmp_v_w2, cmp_v_b2,
           pool_w, pool_scale, w_out, norm2_g,
           w_router, b_router, w_gate_up, b_gate_up, w_down, b_down):
    B, S, _ = x.shape
    T = B * S
    assert S % TQ == 0 and TQ == TK == TM_IN and WINDOW == 2 * TK and S // SLC_BLK <= N_BLK_PAD
    assert (S // CMP_STRIDE) % LANES == 0 and T % TT_ROUTE == 0
    NC = S // CMP_STRIDE
    x2 = x.reshape(T, D_MODEL)
    row = lambda v: v.reshape(1, -1)

    cos, slo, shi = _rope_tables(jnp.arange(S, dtype=F32))
    qn, kvc, ks, vst, kw, vwt, gt, pool_in = _inproj(
        x2, row(norm1_g), _permute_w_in(w_in), cos, slo, shi,
        row(q_norm_g), row(k_norm_slc_g), row(k_norm_win_g), B, S)

    c_end = (jnp.arange(NC, dtype=I32) * CMP_STRIDE + (CMP_BLK - 1)).astype(F32)
    ccos, cslo, cshi = _rope_tables(c_end)
    cw = CMP_STRIDE * HEAD_DIM
    kc = _compress(kvc, 0, cmp_k_pos.reshape(2, cw), cmp_k_w1.reshape(2, cw, CMP_HIDDEN).astype(BF16),
                   row(cmp_k_b1), cmp_k_w2.astype(BF16), row(cmp_k_b2), B, S,
                   rope=(row(k_norm_cmp_g), ccos, cslo, cshi))
    vct = _compress(kvc, 1, cmp_v_pos.reshape(2, cw), cmp_v_w1.reshape(2, cw, CMP_HIDDEN).astype(BF16),
                    row(cmp_v_b1), cmp_v_w2.astype(BF16), row(cmp_v_b2), B, S)

    nsa_out = _nsa(qn, kc, vct, ks, vst, kw, vwt, gt, B, S)

    wr_pad = jnp.pad(w_router, ((0, 0), (0, LANES - N_EXPERTS)))
    wr_hi = wr_pad.astype(BF16)
    wr_lo = (wr_pad - wr_hi.astype(F32)).astype(BF16)
    br_pad = jnp.concatenate([b_router.astype(F32), jnp.full((LANES - N_EXPERTS,), MASK_VALUE, F32)])
    x1, h2p, logits = _mix(nsa_out, pool_in, x2, w_out.astype(BF16), pool_w.astype(BF16),
                           row(pool_scale), row(norm2_g), jnp.concatenate([wr_hi, wr_lo], axis=1),
                           row(br_pad), S)

    n_tiles = T * TOP_K // TM_G + N_EXPERTS
    dest, wts, meta = _route(logits, n_tiles)
    te, tv, nx = meta[0, :n_tiles], meta[1, :n_tiles], meta[2, :n_tiles]
    xs = _dispatch(meta[3, :N_EXPERTS + 1], dest, h2p, n_tiles * TM_G)
    act = _gemm1(te, tv, nx, xs, w_gate_up, b_gate_up.reshape(N_EXPERTS, 1, 2 * D_FF), n_tiles)
    y = _gemm2(te, tv, nx, act, w_down, b_down.reshape(N_EXPERTS, 1, D_MODEL), n_tiles)
    out = _combine(dest, wts.T, x1, y)
    return out.reshape(B, S, D_MODEL)


def kernel(x, norm1_g, w_in, q_norm_g, k_norm_cmp_g, k_norm_slc_g, k_norm_win_g, cmp_k_pos, cmp_k_w1, cmp_k_b1, cmp_k_w2, cmp_k_b2, cmp_v_pos, cmp_v_w1, cmp_v_b1, cmp_v_w2, cmp_v_b2, pool_w, pool_scale, w_out, norm2_g, w_router, b_router, w_gate_up, b_gate_up, w_down, b_down):
    params = (norm1_g, w_in, q_norm_g, k_norm_cmp_g, k_norm_slc_g, k_norm_win_g,
              cmp_k_pos, cmp_k_w1, cmp_k_b1, cmp_k_w2, cmp_k_b2,
              cmp_v_pos, cmp_v_w1, cmp_v_b1, cmp_v_w2, cmp_v_b2,
              pool_w, pool_scale, w_out, norm2_g,
              w_router, b_router, w_gate_up, b_gate_up, w_down, b_down)
    depth = norm1_g.shape[0]
    for l in range(depth):
        x = _layer(x, *[p.reshape(p.shape[1:]) if depth == 1 else p[l] for p in params])
    return x
```

```python
import functools

import jax
import jax.numpy as jnp
from jax import lax
from jax.experimental import pallas as pl
from jax.experimental.pallas import tpu as pltpu

F32 = jnp.float32
BF16 = jnp.bfloat16
I32 = jnp.int32
U32 = jnp.uint32

D_MODEL = 2048
HEAD_DIM = 128
NSA_HEADS = 8
NSA_KV_HEADS = 2
HEADS_PER_GROUP = NSA_HEADS // NSA_KV_HEADS
NSA_WIDTH = NSA_HEADS * HEAD_DIM
N_BRANCH = 3
CMP_BLK = 32
CMP_STRIDE = 16
CMP_HIDDEN = 256
SLC_BLK = 64
SLC_SHIFT = SLC_BLK.bit_length() - 1
N_SEL = 16
WINDOW = 512
ROPE_THETA = 500000.0
ROT_DIM = HEAD_DIM // 4
POOL_WIDTH = 1024
POOL_WINDOWS = (2, 4, 8, 16)
POOL_GC = 256
N_EXPERTS = 32
TOP_K = 4
D_FF = 2048
SWIGLU_ALPHA = 1.702
SWIGLU_LIMIT = 7.0
RMS_EPS = 1e-5
QK_EPS = 1e-6
MASK_VALUE = -1e30
FORCE_VALUE = 1e30
LOG2_E = 1.4426950408889634

LANES = 128
N_KV_COLS = 6 * NSA_KV_HEADS
GATE_PAD = 16
N_BLK_PAD = 128
POOL_HALO = 16

TM_IN = 256
TQ = 256
TK = 256
SEL_UNROLL = 4
TM_MIX = 256
MIX_SUB = 256
TT_ROUTE = 1024
TM_G = 256
TN_G1 = 1024
TN_G2 = 2048
XW = D_MODEL // 2 // LANES
TT_DISP = 512
TT_COMB = 512
ROW_DMA_UNROLL = 8
VMEM_LIMIT = 56 * 1024 * 1024


def _cparams(n_axes, vmem=VMEM_LIMIT):
    return pltpu.CompilerParams(
        dimension_semantics=("arbitrary",) * n_axes, vmem_limit_bytes=vmem)


def _dot(a, b):
    return jnp.dot(a, b, preferred_element_type=F32)


def _slab(ref, token, n_tokens, width):
    first = pl.multiple_of(token * width, width)
    return ref.at[pl.ds(first, n_tokens * width), :]


def _dot_nt(a, b):
    return lax.dot_general(a, b, (((1,), (1,)), ((), ())), preferred_element_type=F32)


def _inproj_kernel(x_ref, g_ref, w_ref, cos_ref, slo_ref, shi_ref, qg_ref, ksg_ref, kwg_ref,
                   qn_ref, kvc_ref, ks_ref, vst_ref, kw_ref, vwt_ref, gt_ref, pool_ref, *, nS):
    x = x_ref[...]
    y = x * lax.rsqrt(jnp.mean(x * x, axis=-1, keepdims=True) + RMS_EPS)
    h = (y * g_ref[...]).astype(BF16)
    cos, slo, shi = cos_ref[...], slo_ref[...], shi_ref[...]
    G = NSA_KV_HEADS
    head = lambda z, c: z[:, c * HEAD_DIM:(c + 1) * HEAD_DIM]

    scale = HEAD_DIM ** -0.5 * LOG2_E
    zq = _dot(h, w_ref[:, 0:NSA_WIDTH])
    for c in range(NSA_HEADS):
        qn_ref[:, c * HEAD_DIM:(c + 1) * HEAD_DIM] = (
            _norm_rope(head(zq, c), qg_ref[...], cos, slo, shi) * scale).astype(BF16)

    pair = lambda p: _dot(h, w_ref[:, NSA_WIDTH + p * G * HEAD_DIM:NSA_WIDTH + (p + 1) * G * HEAD_DIM])
    zkc, zvc = pair(0), pair(1)
    for g in range(G):
        kvc_ref[g] = head(zkc, g)
        kvc_ref[G + g] = head(zvc, g)
    zks, zvs, zkw, zvw = pair(2), pair(3), pair(4), pair(5)
    row = lax.rem(pl.program_id(0), nS) * TM_IN + lax.broadcasted_iota(I32, (TM_IN, N_BLK_PAD), 0)
    lane = lax.broadcasted_iota(I32, (TM_IN, N_BLK_PAD), 1)
    onehot = jnp.where((row >> SLC_SHIFT) == lane, 1.0, 0.0).astype(BF16)
    for g in range(G):
        ks_ref[g, :, 0:HEAD_DIM] = _norm_rope(head(zks, g), ksg_ref[...], cos, slo, shi).astype(BF16)
        ks_ref[g, :, HEAD_DIM:HEAD_DIM + N_BLK_PAD] = onehot
        vst_ref[g, 0] = head(zvs, g).T.astype(BF16)
        kw_ref[g] = _norm_rope(head(zkw, g), kwg_ref[...], cos, slo, shi).astype(BF16)
        vwt_ref[g, 0] = head(zvw, g).T.astype(BF16)

    base = NSA_WIDTH + N_KV_COLS * HEAD_DIM
    pool_ref[...] = _dot(h, w_ref[:, base:base + POOL_WIDTH])
    sig_t = jax.nn.sigmoid(_dot(h, w_ref[:, base + POOL_WIDTH:base + POOL_WIDTH + LANES])).T
    for g in range(G):
        gt_ref[g] = sig_t[g * GATE_PAD:(g + 1) * GATE_PAD]


def _inproj(x2, g1, w_perm, cos, slo, shi, qg, ksg, kwg, B, S):
    T = x2.shape[0]
    n_cols = w_perm.shape[1]
    nS = S // TM_IN
    G = NSA_KV_HEADS
    full = lambda shape: pl.BlockSpec(shape, lambda i: (0,) * len(shape))
    tab = pl.BlockSpec((TM_IN, LANES), lambda i: (lax.rem(i, nS), 0))
    bi = lambda i: (i // nS, lax.rem(i, nS))
    return pl.pallas_call(
        functools.partial(_inproj_kernel, nS=nS),
        grid=(T // TM_IN,),
        in_specs=[
            pl.BlockSpec((TM_IN, D_MODEL), lambda i: (i, 0)),
            full((1, D_MODEL)),
            full((D_MODEL, n_cols)),
            tab, tab, tab,
            full((1, HEAD_DIM)), full((1, HEAD_DIM)), full((1, HEAD_DIM)),
        ],
        out_specs=[
            pl.BlockSpec((TM_IN, NSA_WIDTH), lambda i: (i, 0)),
            pl.BlockSpec((2 * G, TM_IN, LANES), lambda i: (0, i, 0)),
            pl.BlockSpec((G, TM_IN, 2 * HEAD_DIM), lambda i: (*bi(i), 0)),
            pl.BlockSpec((G, 1, HEAD_DIM, TM_IN), lambda i: (*bi(i), 0, 0)),
            pl.BlockSpec((G, TM_IN, HEAD_DIM), lambda i: (*bi(i), 0)),
            pl.BlockSpec((G, 1, HEAD_DIM, TM_IN), lambda i: (*bi(i), 0, 0)),
            pl.BlockSpec((G, GATE_PAD, TM_IN), lambda i: (i // nS, 0, lax.rem(i, nS))),
            pl.BlockSpec((TM_IN, POOL_WIDTH), lambda i: (i, 0)),
        ],
        out_shape=[
            jax.ShapeDtypeStruct((T, NSA_WIDTH), BF16),
            jax.ShapeDtypeStruct((2 * G, T, LANES), F32),
            jax.ShapeDtypeStruct((B * G, S, 2 * HEAD_DIM), BF16),
            jax.ShapeDtypeStruct((B * G, nS, HEAD_DIM, TM_IN), BF16),
            jax.ShapeDtypeStruct((B * G, S, HEAD_DIM), BF16),
            jax.ShapeDtypeStruct((B * G, nS, HEAD_DIM, TM_IN), BF16),
            jax.ShapeDtypeStruct((B * G, GATE_PAD, S), F32),
            jax.ShapeDtypeStruct((T, POOL_WIDTH), F32),
        ],
        compiler_params=_cparams(1),
        name="inproj",
    )(x2, g1, w_perm, cos, slo, shi, qg, ksg, kwg)


def _norm_rope(x, g, cos, sin_lo, sin_hi):
    half = ROT_DIM // 2
    y = x * lax.rsqrt(jnp.mean(x * x, axis=-1, keepdims=True) + QK_EPS) * g
    return (y * cos + pltpu.roll(y, LANES - half, 1) * sin_lo
            + pltpu.roll(y, half, 1) * sin_hi)


def _compress_body(x_ref, pos_ref, w1_ref, b1_ref, w2_ref, b2_ref):
    nc = x_ref.shape[1] // CMP_STRIDE
    pair_w = 2 * HEAD_DIM
    first = jnp.zeros((nc, CMP_HIDDEN), F32)
    second = jnp.zeros((nc, CMP_HIDDEN), F32)
    for p in range(CMP_STRIDE // 2):
        x = jnp.concatenate([x_ref[0, pl.ds(2 * p + u, nc, stride=CMP_STRIDE), :] for u in range(2)], axis=1)
        cols = slice(p * pair_w, (p + 1) * pair_w)
        first = first + _dot((x + pos_ref[0:1, cols]).astype(BF16), w1_ref[0, cols, :])
        second = second + _dot((x + pos_ref[1:2, cols]).astype(BF16), w1_ref[1, cols, :])
    nxt = pltpu.roll(second, nc - 1, 0)
    hid = jax.nn.gelu(first + nxt + b1_ref[...])
    return _dot(hid.astype(BF16), w2_ref[...]) + b2_ref[...]


def _compress_k_kernel(x_ref, pos_ref, w1_ref, b1_ref, w2_ref, b2_ref,
                       g_ref, cos_ref, slo_ref, shi_ref, o_ref):
    out = _compress_body(x_ref, pos_ref, w1_ref, b1_ref, w2_ref, b2_ref)
    o_ref[0] = _norm_rope(out, g_ref[...], cos_ref[...], slo_ref[...], shi_ref[...]).astype(BF16)


def _compress_v_kernel(x_ref, pos_ref, w1_ref, b1_ref, w2_ref, b2_ref, o_ref):
    out = _compress_body(x_ref, pos_ref, w1_ref, b1_ref, w2_ref, b2_ref)
    o_ref[0] = out.T.astype(BF16)


def _compress(kvc, which, pos2, w1, b1, w2, b2, B, S, rope=None):
    NC = S // CMP_STRIDE
    G = NSA_KV_HEADS
    CW = CMP_STRIDE * HEAD_DIM
    common_specs = [
        pl.BlockSpec((1, S, HEAD_DIM), lambda b, g: (which * G + g, b, 0)),
        pl.BlockSpec((2, CW), lambda b, g: (0, 0)),
        pl.BlockSpec((2, CW, CMP_HIDDEN), lambda b, g: (0, 0, 0)),
        pl.BlockSpec((1, CMP_HIDDEN), lambda b, g: (0, 0)),
        pl.BlockSpec((CMP_HIDDEN, HEAD_DIM), lambda b, g: (0, 0)),
        pl.BlockSpec((1, HEAD_DIM), lambda b, g: (0, 0)),
    ]
    if rope is not None:
        gk, cos, slo, shi = rope
        tab = pl.BlockSpec((NC, LANES), lambda b, g: (0, 0))
        return pl.pallas_call(
            _compress_k_kernel,
            grid=(B, G),
            in_specs=common_specs + [pl.BlockSpec((1, HEAD_DIM), lambda b, g: (0, 0)), tab, tab, tab],
            out_specs=pl.BlockSpec((1, NC, HEAD_DIM), lambda b, g: (b * G + g, 0, 0)),
            out_shape=jax.ShapeDtypeStruct((B * G, NC, HEAD_DIM), BF16),
            compiler_params=_cparams(2),
            name="compress_k",
        )(kvc, pos2, w1, b1, w2, b2, gk, cos, slo, shi)
    return pl.pallas_call(
        _compress_v_kernel,
        grid=(B, G),
        in_specs=common_specs,
        out_specs=pl.BlockSpec((1, HEAD_DIM, NC), lambda b, g: (b * G + g, 0, 0)),
        out_shape=jax.ShapeDtypeStruct((B * G, HEAD_DIM, NC), BF16),
        compiler_params=_cparams(2),
        name="compress_v",
    )(kvc, pos2, w1, b1, w2, b2)


BIAS_DIAG, BIAS_FAR, BIAS_ALL, BIAS_NONE = 0, 1, 2, 3


def _tile_bias_table():
    k = lax.broadcasted_iota(I32, (TK, TQ), 0)
    t = lax.broadcasted_iota(I32, (TK, TQ), 1)
    neg = jnp.full((TK, TQ), MASK_VALUE, F32)
    zero = jnp.zeros((TK, TQ), F32)
    return jnp.stack([jnp.where(k <= t, zero, neg), jnp.where(k > t, zero, neg), neg, zero])


def _add_tile_bias(s, b):
    return jnp.concatenate([s[:, r * TQ:(r + 1) * TQ] + b for r in range(HEADS_PER_GROUP)], axis=1)


def _nsa_kernel(q_ref, kc_ref, vct_ref, ks_ref, vst_ref, kw_ref, vwt_ref, gt_ref, bias_ref, o_ref,
                qaug_ref, s_ref, cm_ref, m_ref, l_ref, acc_ref, oc_ref, ow_ref, *, S):
    R = HEADS_PER_GROUP
    NQ = R * TQ
    NC = S // CMP_STRIDE
    qi = pl.program_id(2)
    qs = qi * TQ

    q = q_ref[...].astype(F32)
    q_t = jnp.concatenate([q[:, r * HEAD_DIM:(r + 1) * HEAD_DIM].T for r in range(R)], axis=1).astype(BF16)
    col = lax.broadcasted_iota(I32, (1, NQ), 1)
    t_row = qs + (col & (TQ - 1))

    qaug_ref[0:HEAD_DIM, :] = q_t

    def compressed_and_select(nc, nb):
        s_c = _dot(kc_ref[0, 0:nc, :], q_t)
        c_end = lax.broadcasted_iota(I32, (nc, 1), 0) * CMP_STRIDE + (CMP_BLK - 1)
        s_c = jnp.where(c_end <= t_row, s_c, MASK_VALUE)
        e_c = jnp.exp2(s_c - jnp.max(s_c, axis=0, keepdims=True))
        l_c = jnp.sum(e_c, axis=0, keepdims=True)
        p_c = e_c * jnp.where(t_row >= CMP_BLK - 1, 1.0 / l_c, 0.0)
        oc_ref[...] = _dot(vct_ref[0, :, 0:nc], p_c.astype(BF16))

        imp = p_c[:, 0:TQ]
        for r in range(1, R):
            imp = imp + p_c[:, r * TQ:(r + 1) * TQ]
        per = SLC_BLK // CMP_STRIDE
        jj = lax.broadcasted_iota(I32, (nb, nc), 0) * per
        nn = lax.broadcasted_iota(I32, (nb, nc), 1)
        fold = (jnp.where((nn >= jj) & (nn < jj + per), 1.0, 0.0)
                + jnp.where((nn >= jj - 1) & (nn < jj + per - 1), 1.0, 0.0)).astype(BF16)
        imp_hi = imp.astype(BF16)
        imp_lo = (imp - imp_hi.astype(F32)).astype(BF16)
        blk_score = _dot(fold, imp_hi) + _dot(fold, imp_lo)

        jb = lax.broadcasted_iota(I32, (nb, TQ), 0)
        cur = (qs + lax.broadcasted_iota(I32, (nb, TQ), 1)) >> SLC_SHIFT
        forced = (jb == 0) | (jb == cur) | (jb == cur - 1)
        jbf = jb.astype(F32)
        bias = jnp.where(forced, 0.0, MASK_VALUE)
        val = jnp.where(forced, -jnp.inf, jnp.where(jb <= cur, blk_score, MASK_VALUE))
        for _ in range(min(N_SEL, S // SLC_BLK) - 3):
            mx = jnp.max(val, axis=0, keepdims=True)
            first = jnp.min(jnp.where(val == mx, jbf, float(N_BLK_PAD)), axis=0, keepdims=True)
            pick = jbf == first
            bias = jnp.where(pick, 0.0, bias)
            val = jnp.where(pick, -jnp.inf, val)
        qaug_ref[HEAD_DIM:HEAD_DIM + nb, :] = jnp.concatenate([bias.astype(BF16)] * R, axis=1)
        if nb < N_BLK_PAD:
            qaug_ref[HEAD_DIM + nb:HEAD_DIM + N_BLK_PAD, :] = jnp.full(
                (N_BLK_PAD - nb, NQ), MASK_VALUE, BF16)

    def window_branch():
        w_tiles = ((jnp.maximum(qi - 2, 0), jnp.where(qi >= 2, BIAS_FAR, BIAS_ALL)),
                   (jnp.maximum(qi - 1, 0), jnp.where(qi >= 1, BIAS_NONE, BIAS_ALL)),
                   (qi, BIAS_DIAG))
        s_w = []
        for kt, bi in w_tiles:
            kwt = kw_ref[0, pl.ds(pl.multiple_of(kt * TK, TK), TK), :]
            s_w.append(_add_tile_bias(_dot(kwt, q_t), bias_ref[bi]))
        m_w = jnp.max(s_w[0], axis=0, keepdims=True)
        for s in s_w[1:]:
            m_w = jnp.maximum(m_w, jnp.max(s, axis=0, keepdims=True))
        l_w = jnp.zeros((1, NQ), F32)
        o_w = jnp.zeros((HEAD_DIM, NQ), F32)
        for (kt, _), s in zip(w_tiles, s_w):
            e = jnp.exp2(s - m_w)
            l_w = l_w + jnp.sum(e, axis=0, keepdims=True)
            o_w = o_w + _dot(vwt_ref[0, kt], e.astype(BF16))
        ow_ref[...] = o_w * (1.0 / l_w)

    first_half = (qi + 1) * TQ * 2 <= S

    @pl.when(first_half)
    def _():
        compressed_and_select(NC // 2, N_BLK_PAD // 2)
        window_branch()

    @pl.when(jnp.logical_not(first_half))
    def _():
        compressed_and_select(NC, N_BLK_PAD)
        window_branch()


    m_ref[...] = jnp.full((1, NQ), MASK_VALUE, F32)
    l_ref[...] = jnp.zeros((1, NQ), F32)
    acc_ref[...] = jnp.zeros((HEAD_DIM, NQ), F32)

    def produce(kj, slot, causal):
        k = ks_ref[0, pl.ds(pl.multiple_of(kj * TK, TK), TK), :]
        s = _dot(k, qaug_ref[...])
        if causal:
            s = _add_tile_bias(s, bias_ref[BIAS_DIAG])
        s_ref[slot] = s
        cm_ref[slot] = jnp.max(s, axis=0, keepdims=True)

    def consume(kj, slot):
        m_old = m_ref[...]
        m_new = jnp.maximum(m_old, cm_ref[slot])
        alpha = jnp.exp2(m_old - m_new)
        p = jnp.exp2(s_ref[slot] - m_new)
        l_ref[...] = alpha * l_ref[...] + jnp.sum(p, axis=0, keepdims=True)
        acc_ref[...] = alpha * acc_ref[...] + _dot(vst_ref[0, kj], p.astype(BF16))
        m_ref[...] = m_new

    def stage(kj, slot, causal_next):
        produce(kj + 1, 1 - slot, causal_next)
        consume(kj, slot)

    @pl.when(qi == 0)
    def _():
        produce(0, 0, True)

    @pl.when(qi > 0)
    def _():
        produce(0, 0, False)

    def stage_group(pp, carry):
        for u in range(SEL_UNROLL):
            stage(SEL_UNROLL * pp + u, u & 1, False)
        return carry

    n_plain = jnp.maximum(qi - 1, 0)
    n_grouped = n_plain // SEL_UNROLL * SEL_UNROLL
    lax.fori_loop(0, n_plain // SEL_UNROLL, stage_group, 0)
    n_left = n_plain - n_grouped
    for u in range(0, SEL_UNROLL - 2, 2):
        @pl.when(n_left >= u + 2)
        def _(u=u):
            stage(n_grouped + u, 0, False)
            stage(n_grouped + u + 1, 1, False)

    @pl.when((n_left & 1) == 1)
    def _():
        stage(n_plain - 1, 0, False)

    @pl.when((qi >= 1) & ((n_plain & 1) == 0))
    def _():
        stage(qi - 1, 0, True)
        consume(qi, 1)

    @pl.when((qi >= 1) & ((n_plain & 1) == 1))
    def _():
        stage(qi - 1, 1, True)
        consume(qi, 0)

    @pl.when(qi == 0)
    def _():
        consume(0, 0)

    inv_l = 1.0 / l_ref[...]

    for r in range(R):
        sl = slice(r * TQ, (r + 1) * TQ)
        g0 = gt_ref[0, r * N_BRANCH + 0:r * N_BRANCH + 1, :]
        g1 = gt_ref[0, r * N_BRANCH + 1:r * N_BRANCH + 2, :]
        g2 = gt_ref[0, r * N_BRANCH + 2:r * N_BRANCH + 3, :]
        o = (g0 * oc_ref[:, sl] + (g1 * inv_l[:, sl]) * acc_ref[:, sl]
             + g2 * ow_ref[:, sl])
        o_ref[:, r * HEAD_DIM:(r + 1) * HEAD_DIM] = o.T.astype(BF16)


def _nsa(qn, kc, vct, ks, vst, kw, vwt, gt, B, S):
    G = NSA_KV_HEADS
    R = HEADS_PER_GROUP
    nQ = S // TQ
    nK = S // TK
    NC = S // CMP_STRIDE
    bg = lambda b, g, i: b * G + g
    return pl.pallas_call(
        functools.partial(_nsa_kernel, S=S),
        grid=(B, G, nQ),
        in_specs=[
            pl.BlockSpec((TQ, R * HEAD_DIM), lambda b, g, i: (b * nQ + i, g)),
            pl.BlockSpec((1, NC, HEAD_DIM), lambda b, g, i: (bg(b, g, i), 0, 0)),
            pl.BlockSpec((1, HEAD_DIM, NC), lambda b, g, i: (bg(b, g, i), 0, 0)),
            pl.BlockSpec((1, S, 2 * HEAD_DIM), lambda b, g, i: (bg(b, g, i), 0, 0)),
            pl.BlockSpec((1, nK, HEAD_DIM, TK), lambda b, g, i: (bg(b, g, i), 0, 0, 0)),
            pl.BlockSpec((1, S, HEAD_DIM), lambda b, g, i: (bg(b, g, i), 0, 0)),
            pl.BlockSpec((1, nK, HEAD_DIM, TK), lambda b, g, i: (bg(b, g, i), 0, 0, 0)),
            pl.BlockSpec((1, GATE_PAD, TQ), lambda b, g, i: (bg(b, g, i), 0, i)),
            pl.BlockSpec((4, TK, TQ), lambda b, g, i: (0, 0, 0)),
        ],
        out_specs=pl.BlockSpec((TQ, R * HEAD_DIM), lambda b, g, i: (b * nQ + i, g)),
        out_shape=jax.ShapeDtypeStruct((B * S, NSA_WIDTH), BF16),
        scratch_shapes=[
            pltpu.VMEM((2 * HEAD_DIM, R * TQ), BF16),
            pltpu.VMEM((2, TK, R * TQ), F32),
            pltpu.VMEM((2, 1, R * TQ), F32),
            pltpu.VMEM((1, R * TQ), F32),
            pltpu.VMEM((1, R * TQ), F32),
            pltpu.VMEM((HEAD_DIM, R * TQ), F32),
            pltpu.VMEM((HEAD_DIM, R * TQ), F32),
            pltpu.VMEM((HEAD_DIM, R * TQ), F32),
        ],
        compiler_params=_cparams(3),
        name="nsa",
    )(qn, kc, vct, ks, vst, kw, vwt, gt, _tile_bias_table())


def _mix_kernel(nsa_ref, pool_ref, halo_ref, x_ref, wo_ref, pw_ref, ps_ref, g2_ref,
                wr2_ref, br_ref, x1_ref, h2p_ref, logit_ref, ext_ref, *, S):
    nS = S // TM_MIX
    i = pl.program_id(0)
    t0 = lax.rem(i, nS) * TM_MIX
    ext_ref[0:POOL_HALO, :] = jnp.where(t0 == 0, 0.0, halo_ref[...])
    ext_ref[POOL_HALO:POOL_HALO + TM_MIX, :] = pool_ref[...]
    for sb in range(TM_MIX // MIX_SUB):
        r0 = sb * MIX_SUB
        rows = slice(r0, r0 + MIX_SUB)
        t = t0 + r0 + lax.broadcasted_iota(I32, (MIX_SUB, POOL_GC), 0)
        acc = _dot(nsa_ref[rows, :], wo_ref[0:NSA_WIDTH, :])
        for gi, w in enumerate(POOL_WINDOWS):
            cs = slice(gi * POOL_GC, (gi + 1) * POOL_GC)
            v = pool_ref[rows, cs]
            tot = v
            for k in range(1, w):
                tot = tot + ext_ref[POOL_HALO + r0 - k:POOL_HALO + r0 - k + MIX_SUB, cs]
            cnt = jnp.minimum(t + 1, w).astype(F32)
            d = tot / cnt - v
            y = _dot(d.astype(BF16), pw_ref[gi]) * ps_ref[:, cs]
            acc = acc + _dot(y.astype(BF16),
                             wo_ref[NSA_WIDTH + gi * POOL_GC:NSA_WIDTH + (gi + 1) * POOL_GC, :])
        x1 = x_ref[rows, :] + acc
        x1_ref[rows, :] = x1
        h2 = x1 * lax.rsqrt(jnp.mean(x1 * x1, axis=-1, keepdims=True) + RMS_EPS) * g2_ref[...]
        half = D_MODEL // 2
        packed = pltpu.pack_elementwise([h2[:, 0:half], h2[:, half:D_MODEL]], packed_dtype=BF16)
        for c in range(XW):
            h2p_ref[pl.ds(r0 * XW + c, MIX_SUB, stride=XW), :] = packed[:, c * LANES:(c + 1) * LANES]
        hi = h2.astype(BF16)
        lo = (h2 - hi.astype(F32)).astype(BF16)
        hl = _dot(hi, wr2_ref[...])
        logit_ref[rows, :] = (hl[:, 0:LANES] + hl[:, LANES:2 * LANES] + _dot(lo, wr2_ref[:, 0:LANES])
                              + br_ref[...])


def _mix(nsa_out, pool_in, x2, wo, pw, ps, g2, wr2, br, S):
    T = x2.shape[0]
    hb = TM_MIX // POOL_HALO
    full = lambda shape: pl.BlockSpec(shape, lambda i: (0,) * len(shape))
    return pl.pallas_call(
        functools.partial(_mix_kernel, S=S),
        grid=(T // TM_MIX,),
        in_specs=[
            pl.BlockSpec((TM_MIX, NSA_WIDTH), lambda i: (i, 0)),
            pl.BlockSpec((TM_MIX, POOL_WIDTH), lambda i: (i, 0)),
            pl.BlockSpec((POOL_HALO, POOL_WIDTH), lambda i: (jnp.maximum(i * hb - 1, 0), 0)),
            pl.BlockSpec((TM_MIX, D_MODEL), lambda i: (i, 0)),
            full((D_MODEL, D_MODEL)),
            full((len(POOL_WINDOWS), POOL_GC, POOL_GC)),
            full((1, POOL_WIDTH)),
            full((1, D_MODEL)),
            full((D_MODEL, 2 * LANES)),
            full((1, LANES)),
        ],
        out_specs=[
            pl.BlockSpec((TM_MIX, D_MODEL), lambda i: (i, 0)),
            pl.BlockSpec((TM_MIX * XW, LANES), lambda i: (i, 0)),
            pl.BlockSpec((TM_MIX, LANES), lambda i: (i, 0)),
        ],
        out_shape=[
            jax.ShapeDtypeStruct((T, D_MODEL), F32),
            jax.ShapeDtypeStruct((T * XW, LANES), U32),
            jax.ShapeDtypeStruct((T, LANES), F32),
        ],
        scratch_shapes=[pltpu.VMEM((POOL_HALO + TM_MIX, POOL_WIDTH), F32)],
        compiler_params=_cparams(1),
        name="mix",
    )(nsa_out, pool_in, pool_in, x2, wo, pw, ps, g2, wr2, br)


def _route_kernel(logit_ref, dest_ref, wt_ref, meta_ref, cnt_ref, run_ref, tri_ref, *, n_tiles):
    phase = pl.program_id(0)
    i = pl.program_id(1)
    TT = TT_ROUTE
    E = N_EXPERTS

    @pl.when((phase == 0) & (i == 0))
    def _():
        cnt_ref[...] = jnp.zeros_like(cnt_ref)
        run_ref[...] = jnp.zeros_like(run_ref)

    lt = logit_ref[...].T[0:E, :]
    ef = lax.broadcasted_iota(I32, (E, TT), 0).astype(F32)
    work = lt
    ids, vals, hots = [], [], []
    for _ in range(TOP_K):
        mx = jnp.max(work, axis=0, keepdims=True)
        eid = jnp.min(jnp.where(work == mx, ef, float(E)), axis=0, keepdims=True)
        hot = ef == eid
        ids.append(eid)
        vals.append(mx)
        hots.append(hot)
        work = jnp.where(hot, -jnp.inf, work)
    member = jnp.where(hots[0] | hots[1] | hots[2] | hots[3], 1.0, 0.0)
    tile_cnt = jnp.sum(member, axis=1, keepdims=True)

    @pl.when(phase == 0)
    def _():
        cnt_ref[...] = cnt_ref[...] + tile_cnt

    @pl.when(phase == 1)
    def _():
        ex = [jnp.exp(v - vals[0]) for v in vals]
        den = ex[0] + ex[1] + ex[2] + ex[3]
        cnt = jnp.broadcast_to(cnt_ref[...], (E, LANES))
        padded = jnp.ceil(cnt * (1.0 / TM_G)) * TM_G
        erow = lax.broadcasted_iota(I32, (E, LANES), 0)
        start = jnp.zeros((E, LANES), F32)
        running = jnp.zeros((1, LANES), F32)
        for e in range(E):
            start = jnp.where(erow == e, running, start)
            running = running + padded[e:e + 1, :]
        end = start + padded
        @pl.when(i == 0)
        def _():
            tri_ref[...] = jnp.where(lax.broadcasted_iota(I32, (TT, TT), 0)
                                     < lax.broadcasted_iota(I32, (TT, TT), 1), 1.0, 0.0).astype(BF16)

        before = _dot(member.astype(BF16), tri_ref[...]) + run_ref[...]
        slot = before + start[:, 0:1]
        for k in range(TOP_K):
            d = jnp.sum(jnp.where(hots[k], slot, 0.0), axis=0, keepdims=True)
            dest_ref[k:k + 1, :] = d.astype(I32)
            wt_ref[k:k + 1, :] = ex[k] / den
        run_ref[...] = run_ref[...] + tile_cnt

        @pl.when(i == 0)
        def _():
            lanes = meta_ref.shape[1]
            endw = jnp.broadcast_to(end[:, 0:1], (E, lanes))
            tile_row = (lax.broadcasted_iota(I32, (E, lanes), 1) * TM_G).astype(F32)
            owner = jnp.minimum(jnp.sum(jnp.where(endw <= tile_row, 1.0, 0.0), axis=0, keepdims=True),
                                float(E - 1))
            nxt = jnp.full((E, LANES), -1.0, F32)
            later = jnp.full((1, LANES), -1.0, F32)
            for e in reversed(range(E)):
                nxt = jnp.where(erow == e, later, nxt)
                later = jnp.where(cnt[e:e + 1, :] > 0.0, float(e), later)
            erow_w = lax.broadcasted_iota(I32, (E, lanes), 0).astype(F32)
            nxt_tile = jnp.sum(jnp.where(erow_w == owner, jnp.broadcast_to(nxt[:, 0:1], (E, lanes)), 0.0),
                               axis=0, keepdims=True)
            meta_ref[0:1, :] = owner.astype(I32)
            meta_ref[1:2, :] = jnp.where(tile_row[0:1, :] < running[:, 0:1], 1, 0).astype(I32)
            meta_ref[2:3, :] = nxt_tile.astype(I32)
            lane_w = lax.broadcasted_iota(I32, (E, lanes), 1).astype(F32)
            ends = jnp.sum(jnp.where(erow_w == lane_w, endw, 0.0), axis=0, keepdims=True)
            ends = jnp.where(lane_w[0:1, :] == float(E), running[:, 0:1] * (1.0 / TM_G), ends)
            meta_ref[3:4, :] = ends.astype(I32)
            meta_ref[4:8, :] = jnp.zeros((4, lanes), I32)


def _route(logits, n_tiles):
    T = logits.shape[0]
    nT = T // TT_ROUTE
    lanes = -(-n_tiles // LANES) * LANES
    return pl.pallas_call(
        functools.partial(_route_kernel, n_tiles=n_tiles),
        grid=(2, nT),
        in_specs=[pl.BlockSpec((TT_ROUTE, LANES), lambda p, i: (i, 0))],
        out_specs=[
            pl.BlockSpec((TOP_K, TT_ROUTE), lambda p, i: (0, i * p)),
            pl.BlockSpec((TOP_K, TT_ROUTE), lambda p, i: (0, i * p)),
            pl.BlockSpec((8, lanes), lambda p, i: (0, 0)),
        ],
        out_shape=[
            jax.ShapeDtypeStruct((TOP_K, T), I32),
            jax.ShapeDtypeStruct((TOP_K, T), F32),
            jax.ShapeDtypeStruct((8, lanes), I32),
        ],
        scratch_shapes=[pltpu.VMEM((N_EXPERTS, 1), F32), pltpu.VMEM((N_EXPERTS, 1), F32),
                        pltpu.VMEM((TT_ROUTE, TT_ROUTE), BF16)],
        compiler_params=_cparams(2),
        name="route",
    )(logits)


def _dispatch_kernel(ends_ref, dest_ref, h_ref, xs_ref, zero_ref, sem, zsem):
    @pl.when(pl.program_id(0) == 0)
    def _():
        zero_ref[...] = jnp.zeros_like(zero_ref)

        def tail_copy(e):
            return pltpu.make_async_copy(zero_ref, _slab(xs_ref, ends_ref[e] - TM_G, TM_G, XW), zsem)

        def nonempty(e):
            return ends_ref[e] > (ends_ref[e - 1] if e else 0)

        for e in range(N_EXPERTS):
            @pl.when(nonempty(e))
            def _(e=e):
                tail_copy(e).start()
        for e in range(N_EXPERTS):
            @pl.when(nonempty(e))
            def _(e=e):
                tail_copy(e).wait()

        def unused_copy(tile):
            return pltpu.make_async_copy(zero_ref, _slab(xs_ref, tile * TM_G, TM_G, XW), zsem)

        n_used = ends_ref[N_EXPERTS]
        n_all = xs_ref.shape[0] // (TM_G * XW)

        def start_unused(tile, carry):
            unused_copy(tile).start()
            return carry

        def wait_unused(tile, carry):
            unused_copy(tile).wait()
            return carry

        lax.fori_loop(n_used, n_all, start_unused, 0)
        lax.fori_loop(n_used, n_all, wait_unused, 0)

    def issue(t, carry):
        for k in range(TOP_K):
            pltpu.make_async_copy(_slab(h_ref, t, 1, XW), _slab(xs_ref, dest_ref[k, t], 1, XW),
                                  sem).start(priority=k % 2)
        return carry

    lax.fori_loop(0, TT_DISP, issue, 0, unroll=ROW_DMA_UNROLL)
    all_rows = _slab(xs_ref, 0, TOP_K * TT_DISP, XW)
    pltpu.make_async_copy(all_rows, all_rows, sem).wait()


def _dispatch(ends, dest, h2p, n_rows):
    T = h2p.shape[0] // XW
    grid_spec = pltpu.PrefetchScalarGridSpec(
        num_scalar_prefetch=1,
        grid=(T // TT_DISP,),
        in_specs=[
            pl.BlockSpec((TOP_K, TT_DISP), lambda i, ends: (0, i), memory_space=pltpu.SMEM),
            pl.BlockSpec((TT_DISP * XW, LANES), lambda i, ends: (i, 0)),
        ],
        out_specs=pl.BlockSpec(memory_space=pl.ANY),
        scratch_shapes=[pltpu.VMEM((TM_G * XW, LANES), U32), pltpu.SemaphoreType.DMA,
                        pltpu.SemaphoreType.DMA],
    )
    return pl.pallas_call(
        _dispatch_kernel,
        grid_spec=grid_spec,
        out_shape=jax.ShapeDtypeStruct((n_rows * XW, LANES), U32),
        compiler_params=_cparams(1),
        name="dispatch",
    )(ends, dest, h2p)


def _expert_changed(te_ref, i):
    return (i == 0) | (te_ref[i] != te_ref[jnp.maximum(i - 1, 0)])


WEIGHT_DMA_PRIORITY = 1


def _stream_expert_weights(te_ref, tv_ref, nx_ref, copies, wst_ref, wbf_ref, slot_ref):
    j = pl.program_id(0)
    i = pl.program_id(1)

    @pl.when((j == 0) & (i == 0))
    def _():
        slot_ref[0] = 0
        for c in copies(te_ref[0], 0, 0):
            c.start(priority=WEIGHT_DMA_PRIORITY)

    @pl.when((tv_ref[i] > 0) & _expert_changed(te_ref, i))
    def _():
        slot = slot_ref[0]
        e_next = nx_ref[i]

        @pl.when(e_next >= 0)
        def _():
            for c in copies(e_next, j, 1 - slot):
                c.start(priority=WEIGHT_DMA_PRIORITY)

        @pl.when((e_next < 0) & (j + 1 < pl.num_programs(0)))
        def _():
            for c in copies(te_ref[0], j + 1, 1 - slot):
                c.start(priority=WEIGHT_DMA_PRIORITY)

        for c in copies(te_ref[i], j, slot):
            c.wait()
        slot_ref[1] = slot
        slot_ref[0] = 1 - slot


W_CHUNK = 256


def _expert_matmul(x, wst_ref, wbf_ref, slot_ref, refresh):
    if not refresh:
        return _dot(x, wbf_ref[...])
    acc = None
    for k in range(x.shape[1] // W_CHUNK):
        rows = slice(k * W_CHUNK, (k + 1) * W_CHUNK)
        wbf_ref[rows, :] = wst_ref[slot_ref[1], rows, :].astype(BF16)
        part = _dot(x[:, rows], wbf_ref[rows, :])
        acc = part if acc is None else acc + part
    return acc


def _gemm1_kernel(te_ref, tv_ref, nx_ref, x_ref, bg_ref, bu_ref, w_hbm, act_ref,
                  wst_ref, wbf_ref, slot_ref, sem):
    i = pl.program_id(1)
    valid = tv_ref[i] > 0
    changed = _expert_changed(te_ref, i)

    def copies(e, j, slot):
        col = pl.multiple_of(j * TN_G1, TN_G1)
        return (pltpu.make_async_copy(w_hbm.at[e, :, pl.ds(col, TN_G1)],
                                      wst_ref.at[slot, :, pl.ds(0, TN_G1)], sem.at[slot]),
                pltpu.make_async_copy(w_hbm.at[e, :, pl.ds(D_FF + col, TN_G1)],
                                      wst_ref.at[slot, :, pl.ds(TN_G1, TN_G1)], sem.at[slot]))

    _stream_expert_weights(te_ref, tv_ref, nx_ref, copies, wst_ref, wbf_ref, slot_ref)

    def compute(refresh):
        xp = jnp.concatenate([x_ref[pl.ds(c, TM_G, stride=XW), :] for c in range(XW)], axis=1)
        lo = pltpu.unpack_elementwise(xp, index=0, packed_dtype=BF16, unpacked_dtype=F32).astype(BF16)
        hi = pltpu.unpack_elementwise(xp, index=1, packed_dtype=BF16, unpacked_dtype=F32).astype(BF16)
        gu = _expert_matmul(jnp.concatenate([lo, hi], axis=1), wst_ref, wbf_ref, slot_ref, refresh)
        gate = jnp.minimum(gu[:, 0:TN_G1] + bg_ref[0], SWIGLU_LIMIT)
        up = jnp.clip(gu[:, TN_G1:2 * TN_G1] + bu_ref[0], -SWIGLU_LIMIT, SWIGLU_LIMIT)
        act = (up + 1.0) * gate * jax.nn.sigmoid(SWIGLU_ALPHA * gate)
        act_ref[...] = act.astype(BF16)

    @pl.when(valid & changed)
    def _():
        wbf_ref[...] = wst_ref[slot_ref[1]].astype(BF16)

    @pl.when(valid)
    def _():
        compute(False)

    @pl.when(jnp.logical_not(valid))
    def _():
        act_ref[...] = jnp.zeros_like(act_ref)


def _gemm1(te, tv, nx, xs, w_gu, b_gu3, n_tiles):
    n_rows = xs.shape[0] // XW
    nJ = D_FF // TN_G1
    grid_spec = pltpu.PrefetchScalarGridSpec(
        num_scalar_prefetch=3,
        grid=(nJ, n_tiles),
        in_specs=[
            pl.BlockSpec((TM_G * XW, LANES), lambda j, i, te, tv, nx: (i, 0)),
            pl.BlockSpec((1, 1, TN_G1), lambda j, i, te, tv, nx: (te[i], 0, j)),
            pl.BlockSpec((1, 1, TN_G1), lambda j, i, te, tv, nx: (te[i], 0, nJ + j)),
            pl.BlockSpec(memory_space=pl.ANY),
        ],
        out_specs=pl.BlockSpec((TM_G, TN_G1), lambda j, i, te, tv, nx: (i, j)),
        scratch_shapes=[
            pltpu.VMEM((2, D_MODEL, 2 * TN_G1), F32),
            pltpu.VMEM((D_MODEL, 2 * TN_G1), BF16),
            pltpu.SMEM((2,), I32),
            pltpu.SemaphoreType.DMA((2,)),
        ],
    )
    return pl.pallas_call(
        _gemm1_kernel,
        grid_spec=grid_spec,
        out_shape=jax.ShapeDtypeStruct((n_rows, D_FF), BF16),
        compiler_params=_cparams(2),
        name="gemm1",
    )(te, tv, nx, xs, b_gu3, b_gu3, w_gu)


def _gemm2_kernel(te_ref, tv_ref, nx_ref, a_ref, b_ref, w_hbm, y_ref, wst_ref, wbf_ref, slot_ref, sem):
    i = pl.program_id(1)
    valid = tv_ref[i] > 0

    def copies(e, j, slot):
        col = pl.multiple_of(j * TN_G2, TN_G2)
        return (pltpu.make_async_copy(w_hbm.at[e, :, pl.ds(col, TN_G2)], wst_ref.at[slot], sem.at[slot]),)

    _stream_expert_weights(te_ref, tv_ref, nx_ref, copies, wst_ref, wbf_ref, slot_ref)

    def compute(refresh):
        y = _expert_matmul(a_ref[...], wst_ref, wbf_ref, slot_ref, refresh) + b_ref[0]
        half = D_MODEL // 2
        packed = pltpu.pack_elementwise([y[:, 0:half], y[:, half:D_MODEL]], packed_dtype=BF16)
        for c in range(XW):
            y_ref[pl.ds(c, TM_G, stride=XW), :] = packed[:, c * LANES:(c + 1) * LANES]

    changed = _expert_changed(te_ref, i)

    @pl.when(valid & changed)
    def _():
        compute(True)

    @pl.when(valid & jnp.logical_not(changed))
    def _():
        compute(False)

    @pl.when(jnp.logical_not(valid))
    def _():
        zero = jnp.zeros(y_ref.shape, F32)
        y_ref[...] = pltpu.pack_elementwise([zero, zero], packed_dtype=BF16)


def _gemm2(te, tv, nx, act, w_d, b_d3, n_tiles):
    n_rows = act.shape[0]
    nJ = D_MODEL // TN_G2
    grid_spec = pltpu.PrefetchScalarGridSpec(
        num_scalar_prefetch=3,
        grid=(nJ, n_tiles),
        in_specs=[
            pl.BlockSpec((TM_G, D_FF), lambda j, i, te, tv, nx: (i, 0)),
            pl.BlockSpec((1, 1, TN_G2), lambda j, i, te, tv, nx: (te[i], 0, j)),
            pl.BlockSpec(memory_space=pl.ANY),
        ],
        out_specs=pl.BlockSpec((TM_G * XW, LANES), lambda j, i, te, tv, nx: (i, 0)),
        scratch_shapes=[
            pltpu.VMEM((2, D_FF, TN_G2), F32),
            pltpu.VMEM((D_FF, TN_G2), BF16),
            pltpu.SMEM((2,), I32),
            pltpu.SemaphoreType.DMA((2,)),
        ],
    )
    return pl.pallas_call(
        _gemm2_kernel,
        grid_spec=grid_spec,
        out_shape=jax.ShapeDtypeStruct((n_rows * XW, LANES), U32),
        compiler_params=_cparams(2),
        name="gemm2",
    )(te, tv, nx, act, b_d3, w_d)


def _combine_kernel(dest_ref, dnext_ref, wt_ref, x1_ref, y_ref, o_ref, buf_ref, sem):
    i = pl.program_id(0)
    n = pl.num_programs(0)

    def gather(d_ref, slot):
        def issue(t, carry):
            for k in range(TOP_K):
                pltpu.make_async_copy(_slab(y_ref, d_ref[k, t], 1, XW), _slab(buf_ref.at[slot, k], t, 1, XW),
                                      sem.at[slot]).start(priority=k % 2)
            return carry
        lax.fori_loop(0, TT_COMB, issue, 0, unroll=ROW_DMA_UNROLL)

    def finish(slot):
        pltpu.make_async_copy(buf_ref.at[slot], buf_ref.at[slot], sem.at[slot]).wait()
        wts = [jnp.broadcast_to(wt_ref[:, k:k + 1], (TT_COMB, LANES)) for k in range(TOP_K)]
        for c in range(XW):
            lo0, hi0 = c * LANES, D_MODEL // 2 + c * LANES
            acc_lo = x1_ref[:, lo0:lo0 + LANES]
            acc_hi = x1_ref[:, hi0:hi0 + LANES]
            for k in range(TOP_K):
                words = buf_ref[slot, k, pl.ds(c, TT_COMB, stride=XW), :]
                acc_lo = acc_lo + wts[k] * pltpu.unpack_elementwise(
                    words, index=0, packed_dtype=BF16, unpacked_dtype=F32)
                acc_hi = acc_hi + wts[k] * pltpu.unpack_elementwise(
                    words, index=1, packed_dtype=BF16, unpacked_dtype=F32)
            o_ref[:, lo0:lo0 + LANES] = acc_lo
            o_ref[:, hi0:hi0 + LANES] = acc_hi

    @pl.when(i == 0)
    def _():
        gather(dest_ref, 0)

    for slot in range(2):
        @pl.when((i & 1) == slot)
        def _(slot=slot):
            @pl.when(i + 1 < n)
            def _():
                gather(dnext_ref, 1 - slot)
            finish(slot)


def _combine(dest, wt_tok, x1, y):
    T = x1.shape[0]
    nT = T // TT_COMB
    return pl.pallas_call(
        _combine_kernel,
        grid=(nT,),
        in_specs=[
            pl.BlockSpec((TOP_K, TT_COMB), lambda i: (0, i), memory_space=pltpu.SMEM),
            pl.BlockSpec((TOP_K, TT_COMB), lambda i: (0, jnp.minimum(i + 1, nT - 1)),
                         memory_space=pltpu.SMEM),
            pl.BlockSpec((TT_COMB, TOP_K), lambda i: (i, 0)),
            pl.BlockSpec((TT_COMB, D_MODEL), lambda i: (i, 0)),
            pl.BlockSpec(memory_space=pl.ANY),
        ],
        out_specs=pl.BlockSpec((TT_COMB, D_MODEL), lambda i: (i, 0)),
        out_shape=jax.ShapeDtypeStruct((T, D_MODEL), F32),
        scratch_shapes=[pltpu.VMEM((2, TOP_K, TT_COMB * XW, LANES), U32), pltpu.SemaphoreType.DMA((2,))],
        compiler_params=_cparams(1),
        name="combine",
    )(dest, dest, wt_tok, x1, y)


def _rope_tables(pos):
    half = ROT_DIM // 2
    inv_freq = ROPE_THETA ** (-jnp.arange(0, ROT_DIM, 2, dtype=F32) / ROT_DIM)
    ang = pos[:, None] * inv_freq[None, :]
    cos, sin = jnp.cos(ang), jnp.sin(ang)
    n = pos.shape[0]
    ones = jnp.ones((n, LANES - ROT_DIM), F32)
    zeros = jnp.zeros((n, LANES - ROT_DIM), F32)
    zh = jnp.zeros((n, half), F32)
    return (jnp.concatenate([cos, cos, ones], axis=1),
            jnp.concatenate([-sin, zh, zeros], axis=1),
            jnp.concatenate([zh, sin, zeros], axis=1))


def _permute_w_in(w_in):
    kv_end = NSA_WIDTH + N_KV_COLS * HEAD_DIM
    n_gate = NSA_HEADS * N_BRANCH
    per_g = HEADS_PER_GROUP * N_BRANCH
    gate = w_in[:, kv_end:kv_end + n_gate]
    pieces = [w_in[:, :kv_end], w_in[:, kv_end + n_gate:]]
    zpad = jnp.zeros((D_MODEL, GATE_PAD - per_g), w_in.dtype)
    for g in range(NSA_KV_HEADS):
        pieces += [gate[:, g * per_g:(g + 1) * per_g], zpad]
    pieces.append(jnp.zeros((D_MODEL, LANES - NSA_KV_HEADS * GATE_PAD), w_in.dtype))
    return jnp.concatenate(pieces, axis=1).astype(BF16)


def _layer(x, norm1_g, w_in, q_norm_g, k_norm_cmp_g, k_norm_slc_g, k_norm_win_g,
           cmp_k_pos, cmp_k_w1, cmp_k_b1, cmp_k_w2, cmp_k_b2,
           cmp_v_pos, cmp_v_w1, cmp_v_b1, cmp_v_w2, cmp_v_b2,
           pool_w, pool_scale, w_out, norm2_g,
           w_router, b_router, w_gate_up, b_gate_up, w_down, b_down):
    B, S, _ = x.shape
    T = B * S
    assert S % TQ == 0 and TQ == TK == TM_IN and WINDOW == 2 * TK and S // SLC_BLK <= N_BLK_PAD
    assert (S // CMP_STRIDE) % LANES == 0 and T % TT_ROUTE == 0
    NC = S // CMP_STRIDE
    x2 = x.reshape(T, D_MODEL)
    row = lambda v: v.reshape(1, -1)

    cos, slo, shi = _rope_tables(jnp.arange(S, dtype=F32))
    qn, kvc, ks, vst, kw, vwt, gt, pool_in = _inproj(
        x2, row(norm1_g), _permute_w_in(w_in), cos, slo, shi,
        row(q_norm_g), row(k_norm_slc_g), row(k_norm_win_g), B, S)

    c_end = (jnp.arange(NC, dtype=I32) * CMP_STRIDE + (CMP_BLK - 1)).astype(F32)
    ccos, cslo, cshi = _rope_tables(c_end)
    cw = CMP_STRIDE * HEAD_DIM
    kc = _compress(kvc, 0, cmp_k_pos.reshape(2, cw), cmp_k_w1.reshape(2, cw, CMP_HIDDEN).astype(BF16),
                   row(cmp_k_b1), cmp_k_w2.astype(BF16), row(cmp_k_b2), B, S,
                   rope=(row(k_norm_cmp_g), ccos, cslo, cshi))
    vct = _compress(kvc, 1, cmp_v_pos.reshape(2, cw), cmp_v_w1.reshape(2, cw, CMP_HIDDEN).astype(BF16),
                    row(cmp_v_b1), cmp_v_w2.astype(BF16), row(cmp_v_b2), B, S)

    nsa_out = _nsa(qn, kc, vct, ks, vst, kw, vwt, gt, B, S)

    wr_pad = jnp.pad(w_router, ((0, 0), (0, LANES - N_EXPERTS)))
    wr_hi = wr_pad.astype(BF16)
    wr_lo = (wr_pad - wr_hi.astype(F32)).astype(BF16)
    br_pad = jnp.concatenate([b_router.astype(F32), jnp.full((LANES - N_EXPERTS,), MASK_VALUE, F32)])
    x1, h2p, logits = _mix(nsa_out, pool_in, x2, w_out.astype(BF16), pool_w.astype(BF16),
                           row(pool_scale), row(norm2_g), jnp.concatenate([wr_hi, wr_lo], axis=1),
                           row(br_pad), S)

    n_tiles = T * TOP_K // TM_G + N_EXPERTS
    dest, wts, meta = _route(logits, n_tiles)
    te, tv, nx = meta[0, :n_tiles], meta[1, :n_tiles], meta[2, :n_tiles]
    xs = _dispatch(meta[3, :N_EXPERTS + 1], dest, h2p, n_tiles * TM_G)
    act = _gemm1(te, tv, nx, xs, w_gate_up, b_gate_up.reshape(N_EXPERTS, 1, 2 * D_FF), n_tiles)
    y = _gemm2(te, tv, nx, act, w_down, b_down.reshape(N_EXPERTS, 1, D_MODEL), n_tiles)
    out = _combine(dest, wts.T, x1, y)
    return out.reshape(B, S, D_MODEL)


def kernel(x, norm1_g, w_in, q_norm_g, k_norm_cmp_g, k_norm_slc_g, k_norm_win_g, cmp_k_pos, cmp_k_w1, cmp_k_b1, cmp_k_w2, cmp_k_b2, cmp_v_pos, cmp_v_w1, cmp_v_b1, cmp_v_w2, cmp_v_b2, pool_w, pool_scale, w_out, norm2_g, w_router, b_router, w_gate_up, b_gate_up, w_down, b_down):
    params = (norm1_g, w_in, q_norm_g, k_norm_cmp_g, k_norm_slc_g, k_norm_win_g,
              cmp_k_pos, cmp_k_w1, cmp_k_b1, cmp_k_w2, cmp_k_b2,
              cmp_v_pos, cmp_v_w1, cmp_v_b1, cmp_v_w2, cmp_v_b2,
              pool_w, pool_scale, w_out, norm2_g,
              w_router, b_router, w_gate_up, b_gate_up, w_down, b_down)
    depth = norm1_g.shape[0]
    for l in range(depth):
        x = _layer(x, *[p.reshape(p.shape[1:]) if depth == 1 else p[l] for p in params])
    return x
```

```python
import functools

import jax
import jax.numpy as jnp
from jax import lax
from jax.experimental import pallas as pl
from jax.experimental.pallas import tpu as pltpu

F32 = jnp.float32
BF16 = jnp.bfloat16
I32 = jnp.int32
U32 = jnp.uint32

D_MODEL = 2048
HEAD_DIM = 128
NSA_HEADS = 8
NSA_KV_HEADS = 2
HEADS_PER_GROUP = NSA_HEADS // NSA_KV_HEADS
NSA_WIDTH = NSA_HEADS * HEAD_DIM
N_BRANCH = 3
CMP_BLK = 32
CMP_STRIDE = 16
CMP_HIDDEN = 256
SLC_BLK = 64
SLC_SHIFT = SLC_BLK.bit_length() - 1
N_SEL = 16
WINDOW = 512
ROPE_THETA = 500000.0
ROT_DIM = HEAD_DIM // 4
POOL_WIDTH = 1024
POOL_WINDOWS = (2, 4, 8, 16)
POOL_GC = 256
N_EXPERTS = 32
TOP_K = 4
D_FF = 2048
SWIGLU_ALPHA = 1.702
SWIGLU_LIMIT = 7.0
RMS_EPS = 1e-5
QK_EPS = 1e-6
MASK_VALUE = -1e30
FORCE_VALUE = 1e30
LOG2_E = 1.4426950408889634

LANES = 128
N_KV_COLS = 6 * NSA_KV_HEADS
GATE_PAD = 16
N_BLK_PAD = 128
POOL_HALO = 16

TM_IN = 256
TQ = 256
TK = 256
SEL_UNROLL = 4
TM_MIX = 256
MIX_SUB = 256
TT_ROUTE = 1024
TM_G = 256
TN_G1 = 1024
TN_G2 = 2048
XW = D_MODEL // 2 // LANES
TT_DISP = 512
TT_COMB = 256
ROW_DMA_UNROLL = 8
VMEM_LIMIT = 56 * 1024 * 1024


def _cparams(n_axes, vmem=VMEM_LIMIT):
    return pltpu.CompilerParams(
        dimension_semantics=("arbitrary",) * n_axes, vmem_limit_bytes=vmem)


def _dot(a, b):
    return jnp.dot(a, b, preferred_element_type=F32)


def _slab(ref, token, n_tokens, width):
    first = pl.multiple_of(token * width, width)
    return ref.at[pl.ds(first, n_tokens * width), :]


def _dot_nt(a, b):
    return lax.dot_general(a, b, (((1,), (1,)), ((), ())), preferred_element_type=F32)


def _inproj_kernel(x_ref, g_ref, w_ref, cos_ref, slo_ref, shi_ref, qg_ref, ksg_ref, kwg_ref,
                   qn_ref, kvc_ref, ks_ref, vst_ref, kw_ref, vwt_ref, gt_ref, pool_ref, *, nS):
    x = x_ref[...]
    y = x * lax.rsqrt(jnp.mean(x * x, axis=-1, keepdims=True) + RMS_EPS)
    h = (y * g_ref[...]).astype(BF16)
    cos, slo, shi = cos_ref[...], slo_ref[...], shi_ref[...]
    G = NSA_KV_HEADS
    head = lambda z, c: z[:, c * HEAD_DIM:(c + 1) * HEAD_DIM]

    scale = HEAD_DIM ** -0.5 * LOG2_E
    zq = _dot(h, w_ref[:, 0:NSA_WIDTH])
    for c in range(NSA_HEADS):
        qn_ref[:, c * HEAD_DIM:(c + 1) * HEAD_DIM] = (
            _norm_rope(head(zq, c), qg_ref[...], cos, slo, shi) * scale).astype(BF16)

    pair = lambda p: _dot(h, w_ref[:, NSA_WIDTH + p * G * HEAD_DIM:NSA_WIDTH + (p + 1) * G * HEAD_DIM])
    zkc, zvc = pair(0), pair(1)
    for g in range(G):
        kvc_ref[g] = head(zkc, g)
        kvc_ref[G + g] = head(zvc, g)
    zks, zvs, zkw, zvw = pair(2), pair(3), pair(4), pair(5)
    row = lax.rem(pl.program_id(0), nS) * TM_IN + lax.broadcasted_iota(I32, (TM_IN, N_BLK_PAD), 0)
    lane = lax.broadcasted_iota(I32, (TM_IN, N_BLK_PAD), 1)
    onehot = jnp.where((row >> SLC_SHIFT) == lane, 1.0, 0.0).astype(BF16)
    for g in range(G):
        ks_ref[g, :, 0:HEAD_DIM] = _norm_rope(head(zks, g), ksg_ref[...], cos, slo, shi).astype(BF16)
        ks_ref[g, :, HEAD_DIM:HEAD_DIM + N_BLK_PAD] = onehot
        vst_ref[g, 0] = head(zvs, g).T.astype(BF16)
        kw_ref[g] = _norm_rope(head(zkw, g), kwg_ref[...], cos, slo, shi).astype(BF16)
        vwt_ref[g, 0] = head(zvw, g).T.astype(BF16)

    base = NSA_WIDTH + N_KV_COLS * HEAD_DIM
    pool_ref[...] = _dot(h, w_ref[:, base:base + POOL_WIDTH])
    sig_t = jax.nn.sigmoid(_dot(h, w_ref[:, base + POOL_WIDTH:base + POOL_WIDTH + LANES])).T
    for g in range(G):
        gt_ref[g] = sig_t[g * GATE_PAD:(g + 1) * GATE_PAD]


def _inproj(x2, g1, w_perm, cos, slo, shi, qg, ksg, kwg, B, S):
    T = x2.shape[0]
    n_cols = w_perm.shape[1]
    nS = S // TM_IN
    G = NSA_KV_HEADS
    full = lambda shape: pl.BlockSpec(shape, lambda i: (0,) * len(shape))
    tab = pl.BlockSpec((TM_IN, LANES), lambda i: (lax.rem(i, nS), 0))
    bi = lambda i: (i // nS, lax.rem(i, nS))
    return pl.pallas_call(
        functools.partial(_inproj_kernel, nS=nS),
        grid=(T // TM_IN,),
        in_specs=[
            pl.BlockSpec((TM_IN, D_MODEL), lambda i: (i, 0)),
            full((1, D_MODEL)),
            full((D_MODEL, n_cols)),
            tab, tab, tab,
            full((1, HEAD_DIM)), full((1, HEAD_DIM)), full((1, HEAD_DIM)),
        ],
        out_specs=[
            pl.BlockSpec((TM_IN, NSA_WIDTH), lambda i: (i, 0)),
            pl.BlockSpec((2 * G, TM_IN, LANES), lambda i: (0, i, 0)),
            pl.BlockSpec((G, TM_IN, 2 * HEAD_DIM), lambda i: (*bi(i), 0)),
            pl.BlockSpec((G, 1, HEAD_DIM, TM_IN), lambda i: (*bi(i), 0, 0)),
            pl.BlockSpec((G, TM_IN, HEAD_DIM), lambda i: (*bi(i), 0)),
            pl.BlockSpec((G, 1, HEAD_DIM, TM_IN), lambda i: (*bi(i), 0, 0)),
            pl.BlockSpec((G, GATE_PAD, TM_IN), lambda i: (i // nS, 0, lax.rem(i, nS))),
            pl.BlockSpec((TM_IN, POOL_WIDTH), lambda i: (i, 0)),
        ],
        out_shape=[
            jax.ShapeDtypeStruct((T, NSA_WIDTH), BF16),
            jax.ShapeDtypeStruct((2 * G, T, LANES), F32),
            jax.ShapeDtypeStruct((B * G, S, 2 * HEAD_DIM), BF16),
            jax.ShapeDtypeStruct((B * G, nS, HEAD_DIM, TM_IN), BF16),
            jax.ShapeDtypeStruct((B * G, S, HEAD_DIM), BF16),
            jax.ShapeDtypeStruct((B * G, nS, HEAD_DIM, TM_IN), BF16),
            jax.ShapeDtypeStruct((B * G, GATE_PAD, S), F32),
            jax.ShapeDtypeStruct((T, POOL_WIDTH), F32),
        ],
        compiler_params=_cparams(1),
        name="inproj",
    )(x2, g1, w_perm, cos, slo, shi, qg, ksg, kwg)


def _norm_rope(x, g, cos, sin_lo, sin_hi):
    half = ROT_DIM // 2
    y = x * lax.rsqrt(jnp.mean(x * x, axis=-1, keepdims=True) + QK_EPS) * g
    return (y * cos + pltpu.roll(y, LANES - half, 1) * sin_lo
            + pltpu.roll(y, half, 1) * sin_hi)


def _compress_body(x_ref, pos_ref, w1_ref, b1_ref, w2_ref, b2_ref):
    nc = x_ref.shape[1] // CMP_STRIDE
    pair_w = 2 * HEAD_DIM
    first = jnp.zeros((nc, CMP_HIDDEN), F32)
    second = jnp.zeros((nc, CMP_HIDDEN), F32)
    for p in range(CMP_STRIDE // 2):
        x = jnp.concatenate([x_ref[0, pl.ds(2 * p + u, nc, stride=CMP_STRIDE), :] for u in range(2)], axis=1)
        cols = slice(p * pair_w, (p + 1) * pair_w)
        first = first + _dot((x + pos_ref[0:1, cols]).astype(BF16), w1_ref[0, cols, :])
        second = second + _dot((x + pos_ref[1:2, cols]).astype(BF16), w1_ref[1, cols, :])
    nxt = pltpu.roll(second, nc - 1, 0)
    hid = jax.nn.gelu(first + nxt + b1_ref[...])
    return _dot(hid.astype(BF16), w2_ref[...]) + b2_ref[...]


def _compress_k_kernel(x_ref, pos_ref, w1_ref, b1_ref, w2_ref, b2_ref,
                       g_ref, cos_ref, slo_ref, shi_ref, o_ref):
    out = _compress_body(x_ref, pos_ref, w1_ref, b1_ref, w2_ref, b2_ref)
    o_ref[0] = _norm_rope(out, g_ref[...], cos_ref[...], slo_ref[...], shi_ref[...]).astype(BF16)


def _compress_v_kernel(x_ref, pos_ref, w1_ref, b1_ref, w2_ref, b2_ref, o_ref):
    out = _compress_body(x_ref, pos_ref, w1_ref, b1_ref, w2_ref, b2_ref)
    o_ref[0] = out.T.astype(BF16)


def _compress(kvc, which, pos2, w1, b1, w2, b2, B, S, rope=None):
    NC = S // CMP_STRIDE
    G = NSA_KV_HEADS
    CW = CMP_STRIDE * HEAD_DIM
    common_specs = [
        pl.BlockSpec((1, S, HEAD_DIM), lambda b, g: (which * G + g, b, 0)),
        pl.BlockSpec((2, CW), lambda b, g: (0, 0)),
        pl.BlockSpec((2, CW, CMP_HIDDEN), lambda b, g: (0, 0, 0)),
        pl.BlockSpec((1, CMP_HIDDEN), lambda b, g: (0, 0)),
        pl.BlockSpec((CMP_HIDDEN, HEAD_DIM), lambda b, g: (0, 0)),
        pl.BlockSpec((1, HEAD_DIM), lambda b, g: (0, 0)),
    ]
    if rope is not None:
        gk, cos, slo, shi = rope
        tab = pl.BlockSpec((NC, LANES), lambda b, g: (0, 0))
        return pl.pallas_call(
            _compress_k_kernel,
            grid=(B, G),
            in_specs=common_specs + [pl.BlockSpec((1, HEAD_DIM), lambda b, g: (0, 0)), tab, tab, tab],
            out_specs=pl.BlockSpec((1, NC, HEAD_DIM), lambda b, g: (b * G + g, 0, 0)),
            out_shape=jax.ShapeDtypeStruct((B * G, NC, HEAD_DIM), BF16),
            compiler_params=_cparams(2),
            name="compress_k",
        )(kvc, pos2, w1, b1, w2, b2, gk, cos, slo, shi)
    return pl.pallas_call(
        _compress_v_kernel,
        grid=(B, G),
        in_specs=common_specs,
        out_specs=pl.BlockSpec((1, HEAD_DIM, NC), lambda b, g: (b * G + g, 0, 0)),
        out_shape=jax.ShapeDtypeStruct((B * G, HEAD_DIM, NC), BF16),
        compiler_params=_cparams(2),
        name="compress_v",
    )(kvc, pos2, w1, b1, w2, b2)


BIAS_DIAG, BIAS_FAR, BIAS_ALL, BIAS_NONE = 0, 1, 2, 3


def _tile_bias_table():
    k = lax.broadcasted_iota(I32, (TK, TQ), 0)
    t = lax.broadcasted_iota(I32, (TK, TQ), 1)
    neg = jnp.full((TK, TQ), MASK_VALUE, F32)
    zero = jnp.zeros((TK, TQ), F32)
    return jnp.stack([jnp.where(k <= t, zero, neg), jnp.where(k > t, zero, neg), neg, zero])


def _add_tile_bias(s, b):
    return jnp.concatenate([s[:, r * TQ:(r + 1) * TQ] + b for r in range(HEADS_PER_GROUP)], axis=1)


def _nsa_kernel(q_ref, kc_ref, vct_ref, ks_ref, vst_ref, kw_ref, vwt_ref, gt_ref, bias_ref, o_ref,
                qaug_ref, s_ref, cm_ref, m_ref, l_ref, acc_ref, oc_ref, ow_ref, *, S):
    R = HEADS_PER_GROUP
    NQ = R * TQ
    NC = S // CMP_STRIDE
    qi = pl.program_id(2)
    qs = qi * TQ

    q = q_ref[...].astype(F32)
    q_t = jnp.concatenate([q[:, r * HEAD_DIM:(r + 1) * HEAD_DIM].T for r in range(R)], axis=1).astype(BF16)
    col = lax.broadcasted_iota(I32, (1, NQ), 1)
    t_row = qs + (col & (TQ - 1))

    qaug_ref[0:HEAD_DIM, :] = q_t

    def compressed_and_select(nc, nb):
        s_c = _dot(kc_ref[0, 0:nc, :], q_t)
        c_end = lax.broadcasted_iota(I32, (nc, 1), 0) * CMP_STRIDE + (CMP_BLK - 1)
        s_c = jnp.where(c_end <= t_row, s_c, MASK_VALUE)
        e_c = jnp.exp2(s_c - jnp.max(s_c, axis=0, keepdims=True))
        l_c = jnp.sum(e_c, axis=0, keepdims=True)
        p_c = e_c * jnp.where(t_row >= CMP_BLK - 1, 1.0 / l_c, 0.0)
        oc_ref[...] = _dot(vct_ref[0, :, 0:nc], p_c.astype(BF16))

        imp = p_c[:, 0:TQ]
        for r in range(1, R):
            imp = imp + p_c[:, r * TQ:(r + 1) * TQ]
        per = SLC_BLK // CMP_STRIDE
        jj = lax.broadcasted_iota(I32, (nb, nc), 0) * per
        nn = lax.broadcasted_iota(I32, (nb, nc), 1)
        fold = (jnp.where((nn >= jj) & (nn < jj + per), 1.0, 0.0)
                + jnp.where((nn >= jj - 1) & (nn < jj + per - 1), 1.0, 0.0)).astype(BF16)
        imp_hi = imp.astype(BF16)
        imp_lo = (imp - imp_hi.astype(F32)).astype(BF16)
        blk_score = _dot(fold, imp_hi) + _dot(fold, imp_lo)

        jb = lax.broadcasted_iota(I32, (nb, TQ), 0)
        cur = (qs + lax.broadcasted_iota(I32, (nb, TQ), 1)) >> SLC_SHIFT
        forced = (jb == 0) | (jb == cur) | (jb == cur - 1)
        jbf = jb.astype(F32)
        bias = jnp.where(forced, 0.0, MASK_VALUE)
        val = jnp.where(forced, -jnp.inf, jnp.where(jb <= cur, blk_score, MASK_VALUE))
        for _ in range(min(N_SEL, S // SLC_BLK) - 3):
            mx = jnp.max(val, axis=0, keepdims=True)
            first = jnp.min(jnp.where(val == mx, jbf, float(N_BLK_PAD)), axis=0, keepdims=True)
            pick = jbf == first
            bias = jnp.where(pick, 0.0, bias)
            val = jnp.where(pick, -jnp.inf, val)
        qaug_ref[HEAD_DIM:HEAD_DIM + nb, :] = jnp.concatenate([bias.astype(BF16)] * R, axis=1)
        if nb < N_BLK_PAD:
            qaug_ref[HEAD_DIM + nb:HEAD_DIM + N_BLK_PAD, :] = jnp.full(
                (N_BLK_PAD - nb, NQ), MASK_VALUE, BF16)

    def window_branch():
        w_tiles = ((jnp.maximum(qi - 2, 0), jnp.where(qi >= 2, BIAS_FAR, BIAS_ALL)),
                   (jnp.maximum(qi - 1, 0), jnp.where(qi >= 1, BIAS_NONE, BIAS_ALL)),
                   (qi, BIAS_DIAG))
        s_w = []
        for kt, bi in w_tiles:
            kwt = kw_ref[0, pl.ds(pl.multiple_of(kt * TK, TK), TK), :]
            s_w.append(_add_tile_bias(_dot(kwt, q_t), bias_ref[bi]))
        m_w = jnp.max(s_w[0], axis=0, keepdims=True)
        for s in s_w[1:]:
            m_w = jnp.maximum(m_w, jnp.max(s, axis=0, keepdims=True))
        l_w = jnp.zeros((1, NQ), F32)
        o_w = jnp.zeros((HEAD_DIM, NQ), F32)
        for (kt, _), s in zip(w_tiles, s_w):
            e = jnp.exp2(s - m_w)
            l_w = l_w + jnp.sum(e, axis=0, keepdims=True)
            o_w = o_w + _dot(vwt_ref[0, kt], e.astype(BF16))
        ow_ref[...] = o_w * (1.0 / l_w)

    first_half = (qi + 1) * TQ * 2 <= S

    @pl.when(first_half)
    def _():
        compressed_and_select(NC // 2, N_BLK_PAD // 2)
        window_branch()

    @pl.when(jnp.logical_not(first_half))
    def _():
        compressed_and_select(NC, N_BLK_PAD)
        window_branch()


    m_ref[...] = jnp.full((1, NQ), MASK_VALUE, F32)
    l_ref[...] = jnp.zeros((1, NQ), F32)
    acc_ref[...] = jnp.zeros((HEAD_DIM, NQ), F32)

    def produce(kj, slot, causal):
        k = ks_ref[0, pl.ds(pl.multiple_of(kj * TK, TK), TK), :]
        s = _dot(k, qaug_ref[...])
        if causal:
            s = _add_tile_bias(s, bias_ref[BIAS_DIAG])
        s_ref[slot] = s
        cm_ref[slot] = jnp.max(s, axis=0, keepdims=True)

    def consume(kj, slot):
        m_old = m_ref[...]
        m_new = jnp.maximum(m_old, cm_ref[slot])
        alpha = jnp.exp2(m_old - m_new)
        p = jnp.exp2(s_ref[slot] - m_new)
        l_ref[...] = alpha * l_ref[...] + jnp.sum(p, axis=0, keepdims=True)
        acc_ref[...] = alpha * acc_ref[...] + _dot(vst_ref[0, kj], p.astype(BF16))
        m_ref[...] = m_new

    def stage(kj, slot, causal_next):
        produce(kj + 1, 1 - slot, causal_next)
        consume(kj, slot)

    @pl.when(qi == 0)
    def _():
        produce(0, 0, True)

    @pl.when(qi > 0)
    def _():
        produce(0, 0, False)

    def stage_group(pp, carry):
        for u in range(SEL_UNROLL):
            stage(SEL_UNROLL * pp + u, u & 1, False)
        return carry

    n_plain = jnp.maximum(qi - 1, 0)
    n_grouped = n_plain // SEL_UNROLL * SEL_UNROLL
    lax.fori_loop(0, n_plain // SEL_UNROLL, stage_group, 0)
    n_left = n_plain - n_grouped
    for u in range(0, SEL_UNROLL - 2, 2):
        @pl.when(n_left >= u + 2)
        def _(u=u):
            stage(n_grouped + u, 0, False)
            stage(n_grouped + u + 1, 1, False)

    @pl.when((n_left & 1) == 1)
    def _():
        stage(n_plain - 1, 0, False)

    @pl.when((qi >= 1) & ((n_plain & 1) == 0))
    def _():
        stage(qi - 1, 0, True)
        consume(qi, 1)

    @pl.when((qi >= 1) & ((n_plain & 1) == 1))
    def _():
        stage(qi - 1, 1, True)
        consume(qi, 0)

    @pl.when(qi == 0)
    def _():
        consume(0, 0)

    inv_l = 1.0 / l_ref[...]

    for r in range(R):
        sl = slice(r * TQ, (r + 1) * TQ)
        g0 = gt_ref[0, r * N_BRANCH + 0:r * N_BRANCH + 1, :]
        g1 = gt_ref[0, r * N_BRANCH + 1:r * N_BRANCH + 2, :]
        g2 = gt_ref[0, r * N_BRANCH + 2:r * N_BRANCH + 3, :]
        o = (g0 * oc_ref[:, sl] + (g1 * inv_l[:, sl]) * acc_ref[:, sl]
             + g2 * ow_ref[:, sl])
        o_ref[:, r * HEAD_DIM:(r + 1) * HEAD_DIM] = o.T.astype(BF16)


def _nsa(qn, kc, vct, ks, vst, kw, vwt, gt, B, S):
    G = NSA_KV_HEADS
    R = HEADS_PER_GROUP
    nQ = S // TQ
    nK = S // TK
    NC = S // CMP_STRIDE
    bg = lambda b, g, i: b * G + g
    return pl.pallas_call(
        functools.partial(_nsa_kernel, S=S),
        grid=(B, G, nQ),
        in_specs=[
            pl.BlockSpec((TQ, R * HEAD_DIM), lambda b, g, i: (b * nQ + i, g)),
            pl.BlockSpec((1, NC, HEAD_DIM), lambda b, g, i: (bg(b, g, i), 0, 0)),
            pl.BlockSpec((1, HEAD_DIM, NC), lambda b, g, i: (bg(b, g, i), 0, 0)),
            pl.BlockSpec((1, S, 2 * HEAD_DIM), lambda b, g, i: (bg(b, g, i), 0, 0)),
            pl.BlockSpec((1, nK, HEAD_DIM, TK), lambda b, g, i: (bg(b, g, i), 0, 0, 0)),
            pl.BlockSpec((1, S, HEAD_DIM), lambda b, g, i: (bg(b, g, i), 0, 0)),
            pl.BlockSpec((1, nK, HEAD_DIM, TK), lambda b, g, i: (bg(b, g, i), 0, 0, 0)),
            pl.BlockSpec((1, GATE_PAD, TQ), lambda b, g, i: (bg(b, g, i), 0, i)),
            pl.BlockSpec((4, TK, TQ), lambda b, g, i: (0, 0, 0)),
        ],
        out_specs=pl.BlockSpec((TQ, R * HEAD_DIM), lambda b, g, i: (b * nQ + i, g)),
        out_shape=jax.ShapeDtypeStruct((B * S, NSA_WIDTH), BF16),
        scratch_shapes=[
            pltpu.VMEM((2 * HEAD_DIM, R * TQ), BF16),
            pltpu.VMEM((2, TK, R * TQ), F32),
            pltpu.VMEM((2, 1, R * TQ), F32),
            pltpu.VMEM((1, R * TQ), F32),
            pltpu.VMEM((1, R * TQ), F32),
            pltpu.VMEM((HEAD_DIM, R * TQ), F32),
            pltpu.VMEM((HEAD_DIM, R * TQ), F32),
            pltpu.VMEM((HEAD_DIM, R * TQ), F32),
        ],
        compiler_params=_cparams(3),
        name="nsa",
    )(qn, kc, vct, ks, vst, kw, vwt, gt, _tile_bias_table())


def _mix_kernel(nsa_ref, pool_ref, halo_ref, x_ref, wo_ref, pw_ref, ps_ref, g2_ref,
                wr2_ref, br_ref, x1_ref, h2p_ref, logit_ref, ext_ref, *, S):
    nS = S // TM_MIX
    i = pl.program_id(0)
    t0 = lax.rem(i, nS) * TM_MIX
    ext_ref[0:POOL_HALO, :] = jnp.where(t0 == 0, 0.0, halo_ref[...])
    ext_ref[POOL_HALO:POOL_HALO + TM_MIX, :] = pool_ref[...]
    for sb in range(TM_MIX // MIX_SUB):
        r0 = sb * MIX_SUB
        rows = slice(r0, r0 + MIX_SUB)
        t = t0 + r0 + lax.broadcasted_iota(I32, (MIX_SUB, POOL_GC), 0)
        acc = _dot(nsa_ref[rows, :], wo_ref[0:NSA_WIDTH, :])
        for gi, w in enumerate(POOL_WINDOWS):
            cs = slice(gi * POOL_GC, (gi + 1) * POOL_GC)
            v = pool_ref[rows, cs]
            tot = v
            for k in range(1, w):
                tot = tot + ext_ref[POOL_HALO + r0 - k:POOL_HALO + r0 - k + MIX_SUB, cs]
            cnt = jnp.minimum(t + 1, w).astype(F32)
            d = tot / cnt - v
            y = _dot(d.astype(BF16), pw_ref[gi]) * ps_ref[:, cs]
            acc = acc + _dot(y.astype(BF16),
                             wo_ref[NSA_WIDTH + gi * POOL_GC:NSA_WIDTH + (gi + 1) * POOL_GC, :])
        x1 = x_ref[rows, :] + acc
        x1_ref[rows, :] = x1
        h2 = x1 * lax.rsqrt(jnp.mean(x1 * x1, axis=-1, keepdims=True) + RMS_EPS) * g2_ref[...]
        half = D_MODEL // 2
        packed = pltpu.pack_elementwise([h2[:, 0:half], h2[:, half:D_MODEL]], packed_dtype=BF16)
        for c in range(XW):
            h2p_ref[pl.ds(r0 * XW + c, MIX_SUB, stride=XW), :] = packed[:, c * LANES:(c + 1) * LANES]
        hi = h2.astype(BF16)
        lo = (h2 - hi.astype(F32)).astype(BF16)
        hl = _dot(hi, wr2_ref[...])
        logit_ref[rows, :] = (hl[:, 0:LANES] + hl[:, LANES:2 * LANES] + _dot(lo, wr2_ref[:, 0:LANES])
                              + br_ref[...])


def _mix(nsa_out, pool_in, x2, wo, pw, ps, g2, wr2, br, S):
    T = x2.shape[0]
    hb = TM_MIX // POOL_HALO
    full = lambda shape: pl.BlockSpec(shape, lambda i: (0,) * len(shape))
    return pl.pallas_call(
        functools.partial(_mix_kernel, S=S),
        grid=(T // TM_MIX,),
        in_specs=[
            pl.BlockSpec((TM_MIX, NSA_WIDTH), lambda i: (i, 0)),
            pl.BlockSpec((TM_MIX, POOL_WIDTH), lambda i: (i, 0)),
            pl.BlockSpec((POOL_HALO, POOL_WIDTH), lambda i: (jnp.maximum(i * hb - 1, 0), 0)),
            pl.BlockSpec((TM_MIX, D_MODEL), lambda i: (i, 0)),
            full((D_MODEL, D_MODEL)),
            full((len(POOL_WINDOWS), POOL_GC, POOL_GC)),
            full((1, POOL_WIDTH)),
            full((1, D_MODEL)),
            full((D_MODEL, 2 * LANES)),
            full((1, LANES)),
        ],
        out_specs=[
            pl.BlockSpec((TM_MIX, D_MODEL), lambda i: (i, 0)),
            pl.BlockSpec((TM_MIX * XW, LANES), lambda i: (i, 0)),
            pl.BlockSpec((TM_MIX, LANES), lambda i: (i, 0)),
        ],
        out_shape=[
            jax.ShapeDtypeStruct((T, D_MODEL), F32),
            jax.ShapeDtypeStruct((T * XW, LANES), U32),
            jax.ShapeDtypeStruct((T, LANES), F32),
        ],
        scratch_shapes=[pltpu.VMEM((POOL_HALO + TM_MIX, POOL_WIDTH), F32)],
        compiler_params=_cparams(1),
        name="mix",
    )(nsa_out, pool_in, pool_in, x2, wo, pw, ps, g2, wr2, br)


def _route_kernel(logit_ref, dest_ref, wt_ref, meta_ref, cnt_ref, run_ref, tri_ref, *, n_tiles):
    phase = pl.program_id(0)
    i = pl.program_id(1)
    TT = TT_ROUTE
    E = N_EXPERTS

    @pl.when((phase == 0) & (i == 0))
    def _():
        cnt_ref[...] = jnp.zeros_like(cnt_ref)
        run_ref[...] = jnp.zeros_like(run_ref)

    lt = logit_ref[...].T[0:E, :]
    ef = lax.broadcasted_iota(I32, (E, TT), 0).astype(F32)
    work = lt
    ids, vals, hots = [], [], []
    for _ in range(TOP_K):
        mx = jnp.max(work, axis=0, keepdims=True)
        eid = jnp.min(jnp.where(work == mx, ef, float(E)), axis=0, keepdims=True)
        hot = ef == eid
        ids.append(eid)
        vals.append(mx)
        hots.append(hot)
        work = jnp.where(hot, -jnp.inf, work)
    member = jnp.where(hots[0] | hots[1] | hots[2] | hots[3], 1.0, 0.0)
    tile_cnt = jnp.sum(member, axis=1, keepdims=True)

    @pl.when(phase == 0)
    def _():
        cnt_ref[...] = cnt_ref[...] + tile_cnt

    @pl.when(phase == 1)
    def _():
        ex = [jnp.exp(v - vals[0]) for v in vals]
        den = ex[0] + ex[1] + ex[2] + ex[3]
        cnt = jnp.broadcast_to(cnt_ref[...], (E, LANES))
        padded = jnp.ceil(cnt * (1.0 / TM_G)) * TM_G
        erow = lax.broadcasted_iota(I32, (E, LANES), 0)
        start = jnp.zeros((E, LANES), F32)
        running = jnp.zeros((1, LANES), F32)
        for e in range(E):
            start = jnp.where(erow == e, running, start)
            running = running + padded[e:e + 1, :]
        end = start + padded
        @pl.when(i == 0)
        def _():
            tri_ref[...] = jnp.where(lax.broadcasted_iota(I32, (TT, TT), 0)
                                     < lax.broadcasted_iota(I32, (TT, TT), 1), 1.0, 0.0).astype(BF16)

        before = _dot(member.astype(BF16), tri_ref[...]) + run_ref[...]
        slot = before + start[:, 0:1]
        for k in range(TOP_K):
            d = jnp.sum(jnp.where(hots[k], slot, 0.0), axis=0, keepdims=True)
            dest_ref[k:k + 1, :] = d.astype(I32)
            wt_ref[k:k + 1, :] = ex[k] / den
        run_ref[...] = run_ref[...] + tile_cnt

        @pl.when(i == 0)
        def _():
            lanes = meta_ref.shape[1]
            endw = jnp.broadcast_to(end[:, 0:1], (E, lanes))
            tile_row = (lax.broadcasted_iota(I32, (E, lanes), 1) * TM_G).astype(F32)
            owner = jnp.minimum(jnp.sum(jnp.where(endw <= tile_row, 1.0, 0.0), axis=0, keepdims=True),
                                float(E - 1))
            nxt = jnp.full((E, LANES), -1.0, F32)
            later = jnp.full((1, LANES), -1.0, F32)
            for e in reversed(range(E)):
                nxt = jnp.where(erow == e, later, nxt)
                later = jnp.where(cnt[e:e + 1, :] > 0.0, float(e), later)
            erow_w = lax.broadcasted_iota(I32, (E, lanes), 0).astype(F32)
            nxt_tile = jnp.sum(jnp.where(erow_w == owner, jnp.broadcast_to(nxt[:, 0:1], (E, lanes)), 0.0),
                               axis=0, keepdims=True)
            meta_ref[0:1, :] = owner.astype(I32)
            meta_ref[1:2, :] = jnp.where(tile_row[0:1, :] < running[:, 0:1], 1, 0).astype(I32)
            meta_ref[2:3, :] = nxt_tile.astype(I32)
            lane_w = lax.broadcasted_iota(I32, (E, lanes), 1).astype(F32)
            ends = jnp.sum(jnp.where(erow_w == lane_w, endw, 0.0), axis=0, keepdims=True)
            ends = jnp.where(lane_w[0:1, :] == float(E), running[:, 0:1] * (1.0 / TM_G), ends)
            meta_ref[3:4, :] = ends.astype(I32)
            meta_ref[4:8, :] = jnp.zeros((4, lanes), I32)


def _route(logits, n_tiles):
    T = logits.shape[0]
    nT = T // TT_ROUTE
    lanes = -(-n_tiles // LANES) * LANES
    return pl.pallas_call(
        functools.partial(_route_kernel, n_tiles=n_tiles),
        grid=(2, nT),
        in_specs=[pl.BlockSpec((TT_ROUTE, LANES), lambda p, i: (i, 0))],
        out_specs=[
            pl.BlockSpec((TOP_K, TT_ROUTE), lambda p, i: (0, i * p)),
            pl.BlockSpec((TOP_K, TT_ROUTE), lambda p, i: (0, i * p)),
            pl.BlockSpec((8, lanes), lambda p, i: (0, 0)),
        ],
        out_shape=[
            jax.ShapeDtypeStruct((TOP_K, T), I32),
            jax.ShapeDtypeStruct((TOP_K, T), F32),
            jax.ShapeDtypeStruct((8, lanes), I32),
        ],
        scratch_shapes=[pltpu.VMEM((N_EXPERTS, 1), F32), pltpu.VMEM((N_EXPERTS, 1), F32),
                        pltpu.VMEM((TT_ROUTE, TT_ROUTE), BF16)],
        compiler_params=_cparams(2),
        name="route",
    )(logits)


def _dispatch_kernel(ends_ref, dest_ref, h_ref, xs_ref, zero_ref, sem, zsem):
    @pl.when(pl.program_id(0) == 0)
    def _():
        zero_ref[...] = jnp.zeros_like(zero_ref)

        def tail_copy(e):
            return pltpu.make_async_copy(zero_ref, _slab(xs_ref, ends_ref[e] - TM_G, TM_G, XW), zsem)

        def nonempty(e):
            return ends_ref[e] > (ends_ref[e - 1] if e else 0)

        for e in range(N_EXPERTS):
            @pl.when(nonempty(e))
            def _(e=e):
                tail_copy(e).start()
        for e in range(N_EXPERTS):
            @pl.when(nonempty(e))
            def _(e=e):
                tail_copy(e).wait()

        def unused_copy(tile):
            return pltpu.make_async_copy(zero_ref, _slab(xs_ref, tile * TM_G, TM_G, XW), zsem)

        n_used = ends_ref[N_EXPERTS]
        n_all = xs_ref.shape[0] // (TM_G * XW)

        def start_unused(tile, carry):
            unused_copy(tile).start()
            return carry

        def wait_unused(tile, carry):
            unused_copy(tile).wait()
            return carry

        lax.fori_loop(n_used, n_all, start_unused, 0)
        lax.fori_loop(n_used, n_all, wait_unused, 0)

    def issue(t, carry):
        for k in range(TOP_K):
            pltpu.make_async_copy(_slab(h_ref, t, 1, XW), _slab(xs_ref, dest_ref[k, t], 1, XW),
                                  sem).start(priority=k % 2)
        return carry

    lax.fori_loop(0, TT_DISP, issue, 0, unroll=ROW_DMA_UNROLL)
    all_rows = _slab(xs_ref, 0, TOP_K * TT_DISP, XW)
    pltpu.make_async_copy(all_rows, all_rows, sem).wait()


def _dispatch(ends, dest, h2p, n_rows):
    T = h2p.shape[0] // XW
    grid_spec = pltpu.PrefetchScalarGridSpec(
        num_scalar_prefetch=1,
        grid=(T // TT_DISP,),
        in_specs=[
            pl.BlockSpec((TOP_K, TT_DISP), lambda i, ends: (0, i), memory_space=pltpu.SMEM),
            pl.BlockSpec((TT_DISP * XW, LANES), lambda i, ends: (i, 0)),
        ],
        out_specs=pl.BlockSpec(memory_space=pl.ANY),
        scratch_shapes=[pltpu.VMEM((TM_G * XW, LANES), U32), pltpu.SemaphoreType.DMA,
                        pltpu.SemaphoreType.DMA],
    )
    return pl.pallas_call(
        _dispatch_kernel,
        grid_spec=grid_spec,
        out_shape=jax.ShapeDtypeStruct((n_rows * XW, LANES), U32),
        compiler_params=_cparams(1),
        name="dispatch",
    )(ends, dest, h2p)


def _expert_changed(te_ref, i):
    return (i == 0) | (te_ref[i] != te_ref[jnp.maximum(i - 1, 0)])


WEIGHT_DMA_PRIORITY = 1


def _stream_expert_weights(te_ref, tv_ref, nx_ref, copies, wst_ref, wbf_ref, slot_ref):
    j = pl.program_id(0)
    i = pl.program_id(1)

    @pl.when((j == 0) & (i == 0))
    def _():
        slot_ref[0] = 0
        for c in copies(te_ref[0], 0, 0):
            c.start(priority=WEIGHT_DMA_PRIORITY)

    @pl.when((tv_ref[i] > 0) & _expert_changed(te_ref, i))
    def _():
        slot = slot_ref[0]
        e_next = nx_ref[i]

        @pl.when(e_next >= 0)
        def _():
            for c in copies(e_next, j, 1 - slot):
                c.start(priority=WEIGHT_DMA_PRIORITY)

        @pl.when((e_next < 0) & (j + 1 < pl.num_programs(0)))
        def _():
            for c in copies(te_ref[0], j + 1, 1 - slot):
                c.start(priority=WEIGHT_DMA_PRIORITY)

        for c in copies(te_ref[i], j, slot):
            c.wait()
        slot_ref[1] = slot
        slot_ref[0] = 1 - slot


W_CHUNK = 256


def _expert_matmul(x, wst_ref, wbf_ref, slot_ref, refresh):
    if not refresh:
        return _dot(x, wbf_ref[...])
    acc = None
    for k in range(x.shape[1] // W_CHUNK):
        rows = slice(k * W_CHUNK, (k + 1) * W_CHUNK)
        wbf_ref[rows, :] = wst_ref[slot_ref[1], rows, :].astype(BF16)
        part = _dot(x[:, rows], wbf_ref[rows, :])
        acc = part if acc is None else acc + part
    return acc


def _gemm1_kernel(te_ref, tv_ref, nx_ref, x_ref, bg_ref, bu_ref, w_hbm, act_ref,
                  wst_ref, wbf_ref, slot_ref, sem):
    i = pl.program_id(1)
    valid = tv_ref[i] > 0
    changed = _expert_changed(te_ref, i)

    def copies(e, j, slot):
        col = pl.multiple_of(j * TN_G1, TN_G1)
        return (pltpu.make_async_copy(w_hbm.at[e, :, pl.ds(col, TN_G1)],
                                      wst_ref.at[slot, :, pl.ds(0, TN_G1)], sem.at[slot]),
                pltpu.make_async_copy(w_hbm.at[e, :, pl.ds(D_FF + col, TN_G1)],
                                      wst_ref.at[slot, :, pl.ds(TN_G1, TN_G1)], sem.at[slot]))

    _stream_expert_weights(te_ref, tv_ref, nx_ref, copies, wst_ref, wbf_ref, slot_ref)

    def compute(refresh):
        xp = jnp.concatenate([x_ref[pl.ds(c, TM_G, stride=XW), :] for c in range(XW)], axis=1)
        lo = pltpu.unpack_elementwise(xp, index=0, packed_dtype=BF16, unpacked_dtype=F32).astype(BF16)
        hi = pltpu.unpack_elementwise(xp, index=1, packed_dtype=BF16, unpacked_dtype=F32).astype(BF16)
        gu = _expert_matmul(jnp.concatenate([lo, hi], axis=1), wst_ref, wbf_ref, slot_ref, refresh)
        gate = jnp.minimum(gu[:, 0:TN_G1] + bg_ref[0], SWIGLU_LIMIT)
        up = jnp.clip(gu[:, TN_G1:2 * TN_G1] + bu_ref[0], -SWIGLU_LIMIT, SWIGLU_LIMIT)
        act = (up + 1.0) * gate * jax.nn.sigmoid(SWIGLU_ALPHA * gate)
        act_ref[...] = act.astype(BF16)

    @pl.when(valid & changed)
    def _():
        wbf_ref[...] = wst_ref[slot_ref[1]].astype(BF16)

    @pl.when(valid)
    def _():
        compute(False)

    @pl.when(jnp.logical_not(valid))
    def _():
        act_ref[...] = jnp.zeros_like(act_ref)


def _gemm1(te, tv, nx, xs, w_gu, b_gu3, n_tiles):
    n_rows = xs.shape[0] // XW
    nJ = D_FF // TN_G1
    grid_spec = pltpu.PrefetchScalarGridSpec(
        num_scalar_prefetch=3,
        grid=(nJ, n_tiles),
        in_specs=[
            pl.BlockSpec((TM_G * XW, LANES), lambda j, i, te, tv, nx: (i, 0)),
            pl.BlockSpec((1, 1, TN_G1), lambda j, i, te, tv, nx: (te[i], 0, j)),
            pl.BlockSpec((1, 1, TN_G1), lambda j, i, te, tv, nx: (te[i], 0, nJ + j)),
            pl.BlockSpec(memory_space=pl.ANY),
        ],
        out_specs=pl.BlockSpec((TM_G, TN_G1), lambda j, i, te, tv, nx: (i, j)),
        scratch_shapes=[
            pltpu.VMEM((2, D_MODEL, 2 * TN_G1), F32),
            pltpu.VMEM((D_MODEL, 2 * TN_G1), BF16),
            pltpu.SMEM((2,), I32),
            pltpu.SemaphoreType.DMA((2,)),
        ],
    )
    return pl.pallas_call(
        _gemm1_kernel,
        grid_spec=grid_spec,
        out_shape=jax.ShapeDtypeStruct((n_rows, D_FF), BF16),
        compiler_params=_cparams(2),
        name="gemm1",
    )(te, tv, nx, xs, b_gu3, b_gu3, w_gu)


def _gemm2_kernel(te_ref, tv_ref, nx_ref, a_ref, b_ref, w_hbm, y_ref, wst_ref, wbf_ref, slot_ref, sem):
    i = pl.program_id(1)
    valid = tv_ref[i] > 0

    def copies(e, j, slot):
        col = pl.multiple_of(j * TN_G2, TN_G2)
        return (pltpu.make_async_copy(w_hbm.at[e, :, pl.ds(col, TN_G2)], wst_ref.at[slot], sem.at[slot]),)

    _stream_expert_weights(te_ref, tv_ref, nx_ref, copies, wst_ref, wbf_ref, slot_ref)

    def compute(refresh):
        y = _expert_matmul(a_ref[...], wst_ref, wbf_ref, slot_ref, refresh) + b_ref[0]
        half = D_MODEL // 2
        packed = pltpu.pack_elementwise([y[:, 0:half], y[:, half:D_MODEL]], packed_dtype=BF16)
        for c in range(XW):
            y_ref[pl.ds(c, TM_G, stride=XW), :] = packed[:, c * LANES:(c + 1) * LANES]

    changed = _expert_changed(te_ref, i)

    @pl.when(valid & changed)
    def _():
        compute(True)

    @pl.when(valid & jnp.logical_not(changed))
    def _():
        compute(False)

    @pl.when(jnp.logical_not(valid))
    def _():
        zero = jnp.zeros(y_ref.shape, F32)
        y_ref[...] = pltpu.pack_elementwise([zero, zero], packed_dtype=BF16)


def _gemm2(te, tv, nx, act, w_d, b_d3, n_tiles):
    n_rows = act.shape[0]
    nJ = D_MODEL // TN_G2
    grid_spec = pltpu.PrefetchScalarGridSpec(
        num_scalar_prefetch=3,
        grid=(nJ, n_tiles),
        in_specs=[
            pl.BlockSpec((TM_G, D_FF), lambda j, i, te, tv, nx: (i, 0)),
            pl.BlockSpec((1, 1, TN_G2), lambda j, i, te, tv, nx: (te[i], 0, j)),
            pl.BlockSpec(memory_space=pl.ANY),
        ],
        out_specs=pl.BlockSpec((TM_G * XW, LANES), lambda j, i, te, tv, nx: (i, 0)),
        scratch_shapes=[
            pltpu.VMEM((2, D_FF, TN_G2), F32),
            pltpu.VMEM((D_FF, TN_G2), BF16),
            pltpu.SMEM((2,), I32),
            pltpu.SemaphoreType.DMA((2,)),
        ],
    )
    return pl.pallas_call(
        _gemm2_kernel,
        grid_spec=grid_spec,
        out_shape=jax.ShapeDtypeStruct((n_rows * XW, LANES), U32),
        compiler_params=_cparams(2),
        name="gemm2",
    )(te, tv, nx, act, b_d3, w_d)


def _combine_kernel(dest_ref, dnext_ref, wt_ref, x1_ref, y_ref, o_ref, buf_ref, sem):
    i = pl.program_id(0)
    n = pl.num_programs(0)

    def gather(d_ref, slot):
        def issue(t, carry):
            for k in range(TOP_K):
                pltpu.make_async_copy(_slab(y_ref, d_ref[k, t], 1, XW), _slab(buf_ref.at[slot, k], t, 1, XW),
                                      sem.at[slot]).start(priority=k % 2)
            return carry
        lax.fori_loop(0, TT_COMB, issue, 0, unroll=ROW_DMA_UNROLL)

    def finish(slot):
        pltpu.make_async_copy(buf_ref.at[slot], buf_ref.at[slot], sem.at[slot]).wait()
        wts = [jnp.broadcast_to(wt_ref[:, k:k + 1], (TT_COMB, LANES)) for k in range(TOP_K)]
        for c in range(XW):
            lo0, hi0 = c * LANES, D_MODEL // 2 + c * LANES
            acc_lo = x1_ref[:, lo0:lo0 + LANES]
            acc_hi = x1_ref[:, hi0:hi0 + LANES]
            for k in range(TOP_K):
                words = buf_ref[slot, k, pl.ds(c, TT_COMB, stride=XW), :]
                acc_lo = acc_lo + wts[k] * pltpu.unpack_elementwise(
                    words, index=0, packed_dtype=BF16, unpacked_dtype=F32)
                acc_hi = acc_hi + wts[k] * pltpu.unpack_elementwise(
                    words, index=1, packed_dtype=BF16, unpacked_dtype=F32)
            o_ref[:, lo0:lo0 + LANES] = acc_lo
            o_ref[:, hi0:hi0 + LANES] = acc_hi

    @pl.when(i == 0)
    def _():
        gather(dest_ref, 0)

    for slot in range(2):
        @pl.when((i & 1) == slot)
        def _(slot=slot):
            @pl.when(i + 1 < n)
            def _():
                gather(dnext_ref, 1 - slot)
            finish(slot)


def _combine(dest, wt_tok, x1, y):
    T = x1.shape[0]
    nT = T // TT_COMB
    return pl.pallas_call(
        _combine_kernel,
        grid=(nT,),
        in_specs=[
            pl.BlockSpec((TOP_K, TT_COMB), lambda i: (0, i), memory_space=pltpu.SMEM),
            pl.BlockSpec((TOP_K, TT_COMB), lambda i: (0, jnp.minimum(i + 1, nT - 1)),
                         memory_space=pltpu.SMEM),
            pl.BlockSpec((TT_COMB, TOP_K), lambda i: (i, 0)),
            pl.BlockSpec((TT_COMB, D_MODEL), lambda i: (i, 0)),
            pl.BlockSpec(memory_space=pl.ANY),
        ],
        out_specs=pl.BlockSpec((TT_COMB, D_MODEL), lambda i: (i, 0)),
        out_shape=jax.ShapeDtypeStruct((T, D_MODEL), F32),
        scratch_shapes=[pltpu.VMEM((2, TOP_K, TT_COMB * XW, LANES), U32), pltpu.SemaphoreType.DMA((2,))],
        compiler_params=_cparams(1),
        name="combine",
    )(dest, dest, wt_tok, x1, y)


def _rope_tables(pos):
    half = ROT_DIM // 2
    inv_freq = ROPE_THETA ** (-jnp.arange(0, ROT_DIM, 2, dtype=F32) / ROT_DIM)
    ang = pos[:, None] * inv_freq[None, :]
    cos, sin = jnp.cos(ang), jnp.sin(ang)
    n = pos.shape[0]
    ones = jnp.ones((n, LANES - ROT_DIM), F32)
    zeros = jnp.zeros((n, LANES - ROT_DIM), F32)
    zh = jnp.zeros((n, half), F32)
    return (jnp.concatenate([cos, cos, ones], axis=1),
            jnp.concatenate([-sin, zh, zeros], axis=1),
            jnp.concatenate([zh, sin, zeros], axis=1))


def _permute_w_in(w_in):
    kv_end = NSA_WIDTH + N_KV_COLS * HEAD_DIM
    n_gate = NSA_HEADS * N_BRANCH
    per_g = HEADS_PER_GROUP * N_BRANCH
    gate = w_in[:, kv_end:kv_end + n_gate]
    pieces = [w_in[:, :kv_end], w_in[:, kv_end + n_gate:]]
    zpad = jnp.zeros((D_MODEL, GATE_PAD - per_g), w_in.dtype)
    for g in range(NSA_KV_HEADS):
        pieces += [gate[:, g * per_g:(g + 1) * per_g], zpad]
    pieces.append(jnp.zeros((D_MODEL, LANES - NSA_KV_HEADS * GATE_PAD), w_in.dtype))
    return jnp.concatenate(pieces, axis=1).astype(BF16)


def _layer(x, norm1_g, w_in, q_norm_g, k_norm_cmp_g, k_norm_slc_g, k_norm_win_g,
           cmp_k_pos, cmp_k_w1, cmp_k_b1, cmp_k_w2, cmp_k_b2,
           cmp_v_pos, cmp_v_w1, cmp_v_b1, cmp_v_w2, cmp_v_b2,
           pool_w, pool_scale, w_out, norm2_g,
           w_router, b_router, w_gate_up, b_gate_up, w_down, b_down):
    B, S, _ = x.shape
    T = B * S
    assert S % TQ == 0 and TQ == TK == TM_IN and WINDOW == 2 * TK and S // SLC_BLK <= N_BLK_PAD
    assert (S // CMP_STRIDE) % LANES == 0 and T % TT_ROUTE == 0
    NC = S // CMP_STRIDE
    x2 = x.reshape(T, D_MODEL)
    row = lambda v: v.reshape(1, -1)

    cos, slo, shi = _rope_tables(jnp.arange(S, dtype=F32))
    qn, kvc, ks, vst, kw, vwt, gt, pool_in = _inproj(
        x2, row(norm1_g), _permute_w_in(w_in), cos, slo, shi,
        row(q_norm_g), row(k_norm_slc_g), row(k_norm_win_g), B, S)

    c_end = (jnp.arange(NC, dtype=I32) * CMP_STRIDE + (CMP_BLK - 1)).astype(F32)
    ccos, cslo, cshi = _rope_tables(c_end)
    cw = CMP_STRIDE * HEAD_DIM
    kc = _compress(kvc, 0, cmp_k_pos.reshape(2, cw), cmp_k_w1.reshape(2, cw, CMP_HIDDEN).astype(BF16),
                   row(cmp_k_b1), cmp_k_w2.astype(BF16), row(cmp_k_b2), B, S,
                   rope=(row(k_norm_cmp_g), ccos, cslo, cshi))
    vct = _compress(kvc, 1, cmp_v_pos.reshape(2, cw), cmp_v_w1.reshape(2, cw, CMP_HIDDEN).astype(BF16),
                    row(cmp_v_b1), cmp_v_w2.astype(BF16), row(cmp_v_b2), B, S)

    nsa_out = _nsa(qn, kc, vct, ks, vst, kw, vwt, gt, B, S)

    wr_pad = jnp.pad(w_router, ((0, 0), (0, LANES - N_EXPERTS)))
    wr_hi = wr_pad.astype(BF16)
    wr_lo = (wr_pad - wr_hi.astype(F32)).astype(BF16)
    br_pad = jnp.concatenate([b_router.astype(F32), jnp.full((LANES - N_EXPERTS,), MASK_VALUE, F32)])
    x1, h2p, logits = _mix(nsa_out, pool_in, x2, w_out.astype(BF16), pool_w.astype(BF16),
                           row(pool_scale), row(norm2_g), jnp.concatenate([wr_hi, wr_lo], axis=1),
                           row(br_pad), S)

    n_tiles = T * TOP_K // TM_G + N_EXPERTS
    dest, wts, meta = _route(logits, n_tiles)
    te, tv, nx = meta[0, :n_tiles], meta[1, :n_tiles], meta[2, :n_tiles]
    xs = _dispatch(meta[3, :N_EXPERTS + 1], dest, h2p, n_tiles * TM_G)
    act = _gemm1(te, tv, nx, xs, w_gate_up, b_gate_up.reshape(N_EXPERTS, 1, 2 * D_FF), n_tiles)
    y = _gemm2(te, tv, nx, act, w_down, b_down.reshape(N_EXPERTS, 1, D_MODEL), n_tiles)
    out = _combine(dest, wts.T, x1, y)
    return out.reshape(B, S, D_MODEL)


def kernel(x, norm1_g, w_in, q_norm_g, k_norm_cmp_g, k_norm_slc_g, k_norm_win_g, cmp_k_pos, cmp_k_w1, cmp_k_b1, cmp_k_w2, cmp_k_b2, cmp_v_pos, cmp_v_w1, cmp_v_b1, cmp_v_w2, cmp_v_b2, pool_w, pool_scale, w_out, norm2_g, w_router, b_router, w_gate_up, b_gate_up, w_down, b_down):
    params = (norm1_g, w_in, q_norm_g, k_norm_cmp_g, k_norm_slc_g, k_norm_win_g,
              cmp_k_pos, cmp_k_w1, cmp_k_b1, cmp_k_w2, cmp_k_b2,
              cmp_v_pos, cmp_v_w1, cmp_v_b1, cmp_v_w2, cmp_v_b2,
              pool_w, pool_scale, w_out, norm2_g,
              w_router, b_router, w_gate_up, b_gate_up, w_down, b_down)
    depth = norm1_g.shape[0]
    for l in range(depth):
        x = _layer(x, *[p.reshape(p.shape[1:]) if depth == 1 else p[l] for p in params])
    return x
```
